```python
import jax, jax.numpy as jnp
from jax import lax
import numpy as np

D_MODEL = 1024
BATCH = 2
SEQ = 8192
DEPTH = 2
DEC_BATCH = 128
DEC_SEQ = 4
PAST_LEN = 2048
PAGE_SIZE = 128

N_MIXERS = 2
N_A = (DEPTH + 1) // 2
N_B = DEPTH // 2
HEAD_DIM = 64
N_HEADS = D_MODEL // HEAD_DIM
DECAY_LORA = 64
AAA_LORA = 64
GATE_LORA = 160
GN_EPS = 64e-5
N_MIX = 6
MOBA_BLOCK = 256
MOBA_TOPK = 3
Q_CHUNK = 128
ROPE_THETA = 500000.0
ROT_DIM = HEAD_DIM // 4
D_FF = 2816
CONV_W = 3
PLE_DIM = 256
RMS_EPS = 1e-6
NEG_BIG = -1e30

kernel_name = "rwkv7_moba_convffn_ple_step"


def rms_norm(x, g):
    xf = x.astype(jnp.float32)
    y = xf * lax.rsqrt(jnp.mean(xf * xf, axis=-1, keepdims=True) + RMS_EPS)
    return (y * g.astype(jnp.float32)).astype(x.dtype)


def partial_rope(x, pos):
    half = ROT_DIM // 2
    inv = ROPE_THETA ** (-2.0 * jnp.arange(half, dtype=jnp.float32) / ROT_DIM)
    ang = pos.astype(jnp.float32)[:, None] * inv[None, :]
    cos = jnp.cos(ang)[None, :, None, :]
    sin = jnp.sin(ang)[None, :, None, :]
    xf = x.astype(jnp.float32)
    x1 = xf[..., :half]
    x2 = xf[..., half:ROT_DIM]
    out = jnp.concatenate([x1 * cos - x2 * sin, x2 * cos + x1 * sin, xf[..., ROT_DIM:]], axis=-1)
    return out.astype(x.dtype)


def rwkv7_mix(xn, shift0, s0, mix, w_rkv, w0, w1, w2, a0, a1, a2, g1, g2, k_k, k_a, r_k, ln_w, ln_b, w_o):
    B, T, D = xn.shape
    f32 = jnp.float32
    x_prev = jnp.concatenate([shift0[:, None, :].astype(xn.dtype), xn[:, :-1]], axis=1)
    xm = xn[None] + (x_prev - xn)[None] * mix[:, None, None, :]
    r, k, v = jnp.einsum('nbtd,nde->nbte', xm[:3], w_rkv)
    w_log = -jax.nn.softplus(-(w0 + jnp.tanh(xm[3] @ w1) @ w2).astype(f32)) - 0.5
    decay = jnp.exp(-jnp.exp(w_log))
    a = jax.nn.sigmoid((a0 + (xm[4] @ a1) @ a2).astype(f32))
    g = jax.nn.sigmoid(xm[5] @ g1) @ g2
    heads = lambda t: t.reshape(B, T, N_HEADS, HEAD_DIM)
    kf = k.astype(f32)
    kk = heads(kf * k_k)
    kk = kk / jnp.maximum(jnp.sqrt(jnp.sum(kk * kk, axis=-1, keepdims=True)), 1e-12)
    kf = heads(kf * (1.0 + (a - 1.0) * k_a))
    rf = heads(r.astype(f32))
    vf = heads(v.astype(f32))
    ah = heads(a)
    dh = heads(decay)
    a_vec = -kk
    b_vec = kk * ah

    def step(S, inp):
        r_t, w_t, k_t, v_t, a_t, b_t = inp
        sa = jnp.einsum('bhij,bhj->bhi', S, a_t)
        S = S * w_t[:, :, None, :] + sa[..., None] * b_t[:, :, None, :] + v_t[..., None] * k_t[:, :, None, :]
        return S, jnp.einsum('bhij,bhj->bhi', S, r_t)

    sw = lambda t: jnp.swapaxes(t, 0, 1)
    s_fin, y = lax.scan(step, s0.astype(f32), (sw(rf), sw(dh), sw(kf), sw(vf), sw(a_vec), sw(b_vec)))
    y = sw(y)
    mu = jnp.mean(y, axis=-1, keepdims=True)
    var = jnp.mean(jnp.square(y - mu), axis=-1, keepdims=True)
    yn = ((y - mu) * lax.rsqrt(var + GN_EPS)).reshape(B, T, D) * ln_w.astype(f32) + ln_b.astype(f32)
    bonus = jnp.sum(rf * kf * r_k.astype(f32), axis=-1, keepdims=True) * vf
    o = (yn + bonus.reshape(B, T, D)).astype(xn.dtype) * g
    return o @ w_o, s_fin.astype(s0.dtype), xn[:, -1]


def moba_chunk(q, k_blocks, v_blocks, k_means, q_pos):
    B, NB, BLK, H, _ = k_blocks.shape
    Q = q.shape[1]
    f32 = jnp.float32
    q_blk = q_pos // MOBA_BLOCK
    qf = q.astype(f32)
    gate = jnp.einsum('bqhd,bnhd->bhqn', qf, k_means)
    past = jnp.arange(NB)[None, :] < q_blk[:, None]
    gate = jnp.where(past, gate, -jnp.inf)
    _, top_idx = lax.top_k(gate, MOBA_TOPK)
    own = jnp.broadcast_to(q_blk.astype(top_idx.dtype)[None, None, :, None], (B, H, Q, 1))
    sel = jnp.concatenate([top_idx, own], axis=-1)
    slot_ok = jnp.arange(MOBA_TOPK)[None, :] < q_blk[:, None]
    own_ok = (q_blk[:, None] * MOBA_BLOCK + jnp.arange(BLK)[None, :]) <= q_pos[:, None]
    bi = jnp.arange(B)[:, None, None]
    hi = jnp.arange(H)[None, :, None]
    scores = []
    for s in range(MOBA_TOPK + 1):
        ks = k_blocks[bi, sel[..., s], :, hi, :]
        sc = jnp.einsum('bqhd,bhqkd->bhqk', qf, ks.astype(f32))
        ok = own_ok if s == MOBA_TOPK else jnp.broadcast_to(slot_ok[:, s:s + 1], (Q, BLK))
        scores.append(jnp.where(ok[None, None], sc, NEG_BIG))
    probs = jax.nn.softmax(jnp.concatenate(scores, axis=-1), axis=-1)
    out = None
    for s in range(MOBA_TOPK + 1):
        vs = v_blocks[bi, sel[..., s], :, hi, :]
        term = jnp.einsum('bhqk,bhqkd->bqhd', probs[..., s * BLK:(s + 1) * BLK], vs.astype(f32))
        out = term if out is None else out + term
    return out.astype(q.dtype)


def moba_attention(q, k, v, q_pos, chunk):
    B, L, H, Dh = k.shape
    nb = max(-(-L // MOBA_BLOCK), MOBA_TOPK)
    pad = nb * MOBA_BLOCK - L
    kb = jnp.pad(k, ((0, 0), (0, pad), (0, 0), (0, 0))).reshape(B, nb, MOBA_BLOCK, H, Dh)
    vb = jnp.pad(v, ((0, 0), (0, pad), (0, 0), (0, 0))).reshape(B, nb, MOBA_BLOCK, H, Dh)
    k_means = jnp.mean(kb.astype(jnp.float32), axis=2)
    Tq = q.shape[1]
    nc = Tq // chunk
    qc = jnp.moveaxis(q.reshape(B, nc, chunk, H, Dh), 1, 0)
    pc = q_pos.reshape(nc, chunk)
    out = lax.map(lambda a: moba_chunk(a[0], kb, vb, k_means, a[1]), (qc, pc))
    return jnp.moveaxis(out, 0, 1).reshape(B, Tq, H, Dh)


def moba_mix(xn, pos, k_past, v_past, w_qkv, w_o, chunk):
    B, T, D = xn.shape
    qkv = (xn @ w_qkv).reshape(B, T, 3, N_HEADS, HEAD_DIM)
    q = partial_rope(qkv[:, :, 0], pos) * (HEAD_DIM ** -0.5)
    k = partial_rope(qkv[:, :, 1], pos)
    v = qkv[:, :, 2]
    if k_past is None:
        k_all, v_all = k, v
    else:
        k_all = jnp.concatenate([k_past.astype(k.dtype), k], axis=1)
        v_all = jnp.concatenate([v_past.astype(v.dtype), v], axis=1)
    o = moba_attention(q, k_all, v_all, pos, chunk)
    return o.reshape(B, T, D) @ w_o, k, v


def conv_ffn(hn, conv_prev, w_up, conv_w, conv_b, w_down):
    u = hn @ w_up
    T = u.shape[1]
    up = jnp.concatenate([conv_prev.astype(u.dtype), u], axis=1)
    c = conv_b
    for j in range(CONV_W):
        c = c + conv_w[j] * up[:, j:j + T]
    gate, val = jnp.split(c, 2, axis=-1)
    return (jax.nn.silu(gate) * val) @ w_down, up[:, -(CONV_W - 1):]


def setup_inputs(seed: int = 0) -> dict:
    key = jax.random.key(seed)
    keys = iter(jax.random.split(key, 48))
    f32 = jnp.float32
    D, H, Dh, F2 = D_MODEL, N_HEADS, HEAD_DIM, 2 * D_FF
    n_pages = PAST_LEN // PAGE_SIZE
    pool = (DEC_BATCH * n_pages * 5) // 4

    def nrm(shape, scale):
        return jax.random.normal(next(keys), shape, f32) * scale

    def unif(shape, lo, hi):
        return jax.random.uniform(next(keys), shape, f32, minval=lo, maxval=hi)

    inp = {}
    inp['x_prompt'] = nrm((BATCH, SEQ, D), 1.0)
    inp['x_sample'] = nrm((DEC_BATCH, DEC_SEQ, D), 1.0)
    inp['state_wkv'] = nrm((N_A, DEC_BATCH, H, Dh, Dh), 0.5)
    inp['state_shift'] = nrm((N_A, DEC_BATCH, D), 1.0)
    inp['cache_k'] = nrm((N_B, pool, PAGE_SIZE, H, Dh), 1.0)
    inp['cache_v'] = nrm((N_B, pool, PAGE_SIZE, H, Dh), 1.0)
    inp['state_conv'] = nrm((DEPTH, DEC_BATCH, CONV_W - 1, F2), 1.0)
    perm = jax.random.permutation(next(keys), pool)
    inp['page_table'] = perm[:DEC_BATCH * n_pages].reshape(DEC_BATCH, n_pages).astype(jnp.int32)
    inp['p_prompt'] = nrm((DEPTH, BATCH, SEQ, PLE_DIM), 1.0)
    inp['p_sample'] = nrm((DEPTH, DEC_BATCH, DEC_SEQ, PLE_DIM), 1.0)
    inp['rw_mix'] = unif((N_A, N_MIX, D), 0.0, 1.0)
    inp['rw_rkv'] = nrm((N_A, 3, D, D), D ** -0.5)
    inp['rw_w0'] = unif((N_A, D), -6.0, 1.0)
    inp['rw_w1'] = nrm((N_A, D, DECAY_LORA), D ** -0.5)
    inp['rw_w2'] = nrm((N_A, DECAY_LORA, D), 0.5 * DECAY_LORA ** -0.5)
    inp['rw_a0'] = nrm((N_A, D), 0.1)
    inp['rw_a1'] = nrm((N_A, D, AAA_LORA), D ** -0.5)
    inp['rw_a2'] = nrm((N_A, AAA_LORA, D), 0.5 * AAA_LORA ** -0.5)
    inp['rw_g1'] = nrm((N_A, D, GATE_LORA), D ** -0.5)
    inp['rw_g2'] = nrm((N_A, GATE_LORA, D), GATE_LORA ** -0.5)
    inp['rw_kk'] = 0.85 + nrm((N_A, D), 0.05)
    inp['rw_ka'] = 1.0 + nrm((N_A, D), 0.05)
    inp['rw_rk'] = nrm((N_A, H, Dh), 0.1)
    inp['rw_lnw'] = 1.0 + nrm((N_A, D), 0.05)
    inp['rw_lnb'] = nrm((N_A, D), 0.02)
    inp['rw_wo'] = nrm((N_A, D, D), D ** -0.5)
    inp['mb_wqkv'] = nrm((N_B, D, 3 * D), D ** -0.5)
    inp['mb_wo'] = nrm((N_B, D, D), D ** -0.5)
    inp['norm_mix'] = 1.0 + nrm((DEPTH, D), 0.05)
    inp['norm_ffn'] = 1.0 + nrm((DEPTH, D), 0.05)
    inp['ff_wup'] = nrm((DEPTH, D, F2), D ** -0.5)
    inp['ff_conv_w'] = nrm((DEPTH, CONV_W, F2), 0.5)
    inp['ff_conv_b'] = nrm((DEPTH, F2), 0.02)
    inp['ff_wdown'] = nrm((DEPTH, D_FF, D), D_FF ** -0.5)
    inp['ple_wp'] = nrm((DEPTH, PLE_DIM, D), PLE_DIM ** -0.5)
    inp['ple_wg'] = nrm((DEPTH, D, D), D ** -0.5)
    inp['norm_out'] = 1.0 + nrm((D,), 0.05)
    return inp


def reference(x_prompt, x_sample, state_wkv, state_shift, cache_k, cache_v, state_conv, page_table,
              p_prompt, p_sample, rw_mix, rw_rkv, rw_w0, rw_w1, rw_w2, rw_a0, rw_a1, rw_a2, rw_g1, rw_g2,
              rw_kk, rw_ka, rw_rk, rw_lnw, rw_lnb, rw_wo, mb_wqkv, mb_wo, norm_mix, norm_ffn,
              ff_wup, ff_conv_w, ff_conv_b, ff_wdown, ple_wp, ple_wg, norm_out):
    bp, tp, D = x_prompt.shape
    bs, ts, _ = x_sample.shape
    dt = x_prompt.dtype
    pos_p = jnp.arange(tp, dtype=jnp.int32)
    pos_s = PAST_LEN + jnp.arange(ts, dtype=jnp.int32)
    xp, xs = x_prompt, x_sample
    wkv_p, shift_p, k_p, v_p, conv_p = [], [], [], [], []
    wkv_s, shift_s, k_s, v_s, conv_s = [], [], [], [], []
    for i in range(DEPTH):
        j = i // N_MIXERS
        hp = rms_norm(xp, norm_mix[i])
        hs = rms_norm(xs, norm_mix[i])
        if i % N_MIXERS == 0:
            w = (rw_mix[j], rw_rkv[j], rw_w0[j], rw_w1[j], rw_w2[j], rw_a0[j], rw_a1[j], rw_a2[j],
                 rw_g1[j], rw_g2[j], rw_kk[j], rw_ka[j], rw_rk[j], rw_lnw[j], rw_lnb[j], rw_wo[j])
            op, sp, lp = rwkv7_mix(hp, jnp.zeros((bp, D), dt),
                                   jnp.zeros((bp, N_HEADS, HEAD_DIM, HEAD_DIM), dt), *w)
            os_, ss, ls = rwkv7_mix(hs, state_shift[j], state_wkv[j], *w)
            wkv_p.append(sp); shift_p.append(lp); wkv_s.append(ss); shift_s.append(ls)
        else:
            op, kp_, vp_ = moba_mix(hp, pos_p, None, None, mb_wqkv[j], mb_wo[j], Q_CHUNK)
            k_past = cache_k[j][page_table].reshape(bs, -1, N_HEADS, HEAD_DIM)
            v_past = cache_v[j][page_table].reshape(bs, -1, N_HEADS, HEAD_DIM)
            os_, ks_, vs_ = moba_mix(hs, pos_s, k_past, v_past, mb_wqkv[j], mb_wo[j], 1)
            k_p.append(kp_); v_p.append(vp_); k_s.append(ks_); v_s.append(vs_)
        xp = xp + op
        xs = xs + os_
        fp, cp = conv_ffn(rms_norm(xp, norm_ffn[i]), jnp.zeros((bp, CONV_W - 1, 2 * D_FF), dt),
                          ff_wup[i], ff_conv_w[i], ff_conv_b[i], ff_wdown[i])
        fs, cs = conv_ffn(rms_norm(xs, norm_ffn[i]), state_conv[i],
                          ff_wup[i], ff_conv_w[i], ff_conv_b[i], ff_wdown[i])
        conv_p.append(cp); conv_s.append(cs)
        xp = xp + fp
        xs = xs + fs
        xp = xp + jax.nn.sigmoid(xp @ ple_wg[i]) * (p_prompt[i] @ ple_wp[i])
        xs = xs + jax.nn.sigmoid(xs @ ple_wg[i]) * (p_sample[i] @ ple_wp[i])
    y_prompt = rms_norm(xp, norm_out)
    y_sample = rms_norm(xs, norm_out)
    return (y_prompt, y_sample,
            jnp.stack(wkv_p), jnp.stack(shift_p), jnp.stack(k_p), jnp.stack(v_p), jnp.stack(conv_p),
            jnp.stack(wkv_s), jnp.stack(shift_s), jnp.stack(k_s), jnp.stack(v_s), jnp.stack(conv_s))
```

```python
import functools

import jax
import jax.numpy as jnp
from jax import lax
from jax.experimental import pallas as pl
from jax.experimental.pallas import tpu as pltpu

F32 = jnp.float32
BF16 = jnp.bfloat16

HEAD_DIM = 64
GN_EPS = 64e-5
RMS_EPS = 1e-6
MOBA_BLOCK = 256
MOBA_TOPK = 3
ROPE_THETA = 500000.0
ROT_DIM = HEAD_DIM // 4
NEG_BIG = -1e30
LANES = 128
SUBLANES = 8
VMEM_LIMIT_BYTES = 56 * 1024 * 1024


def _cparams(n_axes):
    return pltpu.CompilerParams(dimension_semantics=("arbitrary",) * n_axes,
                                vmem_limit_bytes=VMEM_LIMIT_BYTES)


def _const_spec(shape):
    nd = len(shape)
    return pl.BlockSpec(shape, lambda *_: (0,) * nd)


def _dot(a, b):
    return jnp.dot(a.astype(BF16), b.astype(BF16), preferred_element_type=F32)


def _dot_nt(a, b):
    return lax.dot_general(a.astype(BF16), b.astype(BF16), (((1,), (1,)), ((), ())),
                           preferred_element_type=F32)


def _dot_tn(a, b):
    return lax.dot_general(a.astype(BF16), b.astype(BF16), (((0,), (0,)), ((), ())),
                           preferred_element_type=F32)


def _split_dot(x, m):
    hi = x.astype(BF16)
    lo = (x - hi.astype(F32)).astype(BF16)
    return (jnp.dot(hi, m, preferred_element_type=F32) + jnp.dot(lo, m, preferred_element_type=F32))


def _sigmoid(z):
    return 1.0 / (1.0 + jnp.exp(-z))


def _rms(x, g):
    return x * lax.rsqrt(jnp.mean(x * x, axis=-1, keepdims=True) + RMS_EPS) * g


def _seg_sum(x, segr, segb):
    return _split_dot(_split_dot(x, segr), segb)


def _prev_rows_carry(u, k, carry):
    rolled = pltpu.roll(u, k, axis=0)
    row = lax.broadcasted_iota(jnp.int32, (u.shape[0], 1), 0)
    out = rolled
    for j in range(k):
        out = jnp.where(row == j, carry[SUBLANES - k + j:SUBLANES - k + j + 1, :], out)
    return out


def _prev_rows_init(u, k, seq_len, init):
    rolled = pltpu.roll(u, k, axis=0)
    row = lax.broadcasted_iota(jnp.int32, (u.shape[0], 1), 0)
    return jnp.where(row % seq_len < k, init, rolled)


def _rwkv_proj_kernel(carry_mode, seq_len, tiles_per_seq, *refs):
    if carry_mode:
        (x_ref, gmix_ref, mix_ref, wrkv_ref, w0_ref, w1_ref, w2_ref, a0_ref, a1_ref, a2_ref,
         g1_ref, g2_ref, kk_ref, ka_ref, segr_ref, segb_ref,
         r_o, lw_o, k_o, v_o, a_o, b_o, g_o, xn_o, carry) = refs
        init_ref = None
    else:
        (x_ref, init_ref, gmix_ref, mix_ref, wrkv_ref, w0_ref, w1_ref, w2_ref, a0_ref, a1_ref,
         a2_ref, g1_ref, g2_ref, kk_ref, ka_ref, segr_ref, segb_ref,
         r_o, lw_o, k_o, v_o, a_o, b_o, g_o, xn_o) = refs
        carry = None
    xn = _rms(x_ref[...], gmix_ref[...])
    tm = xn.shape[0]
    if carry_mode:
        @pl.when(pl.program_id(0) % tiles_per_seq == 0)
        def _():
            carry[...] = jnp.zeros_like(carry)
        xprev = _prev_rows_carry(xn, 1, carry)
        carry[...] = xn[tm - SUBLANES:tm, :]
        xn_o[0] = xn[tm - 1:tm, :]
    else:
        xprev = _prev_rows_init(xn, 1, seq_len, init_ref[...])
        xn_o[...] = xn
    dx = xprev - xn

    def xm(i):
        return (xn + dx * mix_ref[i:i + 1, :]).astype(BF16)

    r = jnp.dot(xm(0), wrkv_ref[0], preferred_element_type=F32)
    k = jnp.dot(xm(1), wrkv_ref[1], preferred_element_type=F32)
    v = jnp.dot(xm(2), wrkv_ref[2], preferred_element_type=F32)
    wl = w0_ref[...] + _dot(jnp.tanh(jnp.dot(xm(3), w1_ref[...], preferred_element_type=F32)), w2_ref[...])
    sp = jnp.maximum(-wl, 0.0) + jnp.log(1.0 + jnp.exp(-jnp.abs(wl)))
    lw = -jnp.exp(-sp - 0.5)
    a = _sigmoid(a0_ref[...] + _dot(jnp.dot(xm(4), a1_ref[...], preferred_element_type=F32), a2_ref[...]))
    g = _dot(_sigmoid(jnp.dot(xm(5), g1_ref[...], preferred_element_type=F32)), g2_ref[...])
    kk = k * kk_ref[...]
    n2 = _seg_sum(kk * kk, segr_ref[...], segb_ref[...])
    kk = kk / jnp.maximum(jnp.sqrt(n2), 1e-12)
    r_o[...] = r
    lw_o[...] = lw
    k_o[...] = k * (1.0 + (a - 1.0) * ka_ref[...])
    v_o[...] = v
    a_o[...] = -kk
    b_o[...] = kk * a
    g_o[...] = g


def _rwkv_proj(x, seq_len, init, w):
    n, d = x.shape
    carry_mode = init is None
    tm = min(256, n)
    assert n % tm == 0
    if carry_mode:
        assert seq_len % tm == 0
    else:
        assert tm % seq_len == 0
    tiles_per_seq = max(seq_len // tm, 1)
    row = pl.BlockSpec((tm, d), lambda i: (i, 0))
    consts = [w['gmix'], w['mix'], w['wrkv'], w['w0'], w['w1'], w['w2'], w['a0'], w['a1'], w['a2'],
              w['g1'], w['g2'], w['kk'], w['ka'], w['segr'], w['segb']]
    in_specs = [row] + ([] if carry_mode else [row]) + [_const_spec(c.shape) for c in consts]
    args = [x] + ([] if carry_mode else [init]) + consts
    big = jax.ShapeDtypeStruct((n, d), F32)
    if carry_mode:
        nseq = n // seq_len
        xn_shape = jax.ShapeDtypeStruct((nseq, 1, d), F32)
        xn_spec = pl.BlockSpec((1, 1, d), lambda i: (i // tiles_per_seq, 0, 0))
        scratch = [pltpu.VMEM((SUBLANES, d), F32)]
    else:
        xn_shape = big
        xn_spec = row
        scratch = []
    return pl.pallas_call(
        functools.partial(_rwkv_proj_kernel, carry_mode, seq_len, tiles_per_seq),
        out_shape=[big] * 7 + [xn_shape],
        grid=(n // tm,),
        in_specs=in_specs,
        out_specs=[row] * 7 + [xn_spec],
        scratch_shapes=scratch,
        compiler_params=_cparams(1),
        name="rwkv_proj",
    )(*args)


def _wkv_masks(c):
    n = 2 * c
    row = lax.broadcasted_iota(jnp.int32, (n, n), 0)
    col = lax.broadcasted_iota(jnp.int32, (n, n), 1)
    rq = row >= c
    cq = col >= c
    tl = jnp.logical_and(jnp.logical_not(rq), jnp.logical_not(cq))
    br = jnp.logical_and(rq, cq)
    tr = jnp.logical_and(jnp.logical_not(rq), cq)
    bl = jnp.logical_and(rq, jnp.logical_not(cq))
    strict = (col % c) < (row % c)
    incl = (col % c) <= (row % c)
    one = jnp.ones((n, n), F32)
    zero = jnp.zeros((n, n), F32)

    def f(m):
        return jnp.where(m, one, zero)

    return dict(
        ab0=f(jnp.logical_and(strict, tl)), ab1=f(jnp.logical_and(strict, br)),
        ak0=f(jnp.logical_and(strict, tr)), ak1=f(jnp.logical_and(strict, bl)),
        rb0=f(jnp.logical_and(incl, bl)), rb1=f(jnp.logical_and(incl, tr)),
        rk0=f(jnp.logical_and(incl, br)), rk1=f(jnp.logical_and(incl, tl)),
        eye=f(row == col),
    )


def _wkv_chunk(c, n_rounds, r, lw, k, v, a, b, state, tri, msk, m0, m1, bdmask):
    g_inc = jnp.dot(tri, lw, preferred_element_type=F32, precision=lax.Precision.HIGHEST)
    e_inc = jnp.exp(g_inc)
    e_exc = jnp.exp(g_inc - lw)
    e_neg = jnp.exp(-g_inc)
    at = a * e_exc
    rt = r * e_inc
    bt = b * e_neg
    kt = k * e_neg
    lr = jnp.concatenate([at, rt], axis=0)
    rl = jnp.concatenate([rt, at], axis=0)
    bk = jnp.concatenate([bt, kt], axis=0)
    kb = jnp.concatenate([kt, bt], axis=0)
    sc0 = _dot_nt(lr * m0, bk)
    sc1 = _dot_nt(rl * m1, kb)
    ab = sc0 * msk['ab0'] + sc1 * msk['ab1']
    ak = sc0 * msk['ak0'] + sc1 * msk['ak1']
    rb = sc0 * msk['rb0'] + sc1 * msk['rb1']
    rk = sc0 * msk['rk0'] + sc1 * msk['rk1']
    tinv = msk['eye'] + ab
    pw = ab
    for _ in range(n_rounds):
        pw = _dot(pw, pw)
        tinv = tinv + _dot(tinv, pw)
    xs = _dot_nt(lr, state)
    vst = jnp.concatenate([v * m1, v * m0], axis=0)
    x0 = xs[:c]
    zst = jnp.concatenate([x0 * m0, x0 * m1], axis=0) + _dot(ak, vst)
    w = _dot(tinv, zst)
    u = w[:c] + w[c:]
    ys = _dot(jnp.concatenate([rb, rk], axis=1), jnp.concatenate([w, vst], axis=0))
    y = xs[c:] + ys[:c] + ys[c:]
    ds = _dot_tn(jnp.concatenate([u, v], axis=0), bk)
    new_state = (state + ds * bdmask) * e_inc[c - 1:c, :]
    return y, new_state


def _wkv_kernel(has_init, c, n_chunks, bb, npair, *refs):
    if has_init:
        (r_ref, lw_ref, k_ref, v_ref, a_ref, b_ref, g_ref, rk_ref, lnw_ref, lnb_ref, seg_ref,
         s_in, o_ref, s_out, state, ybuf) = refs
    else:
        (r_ref, lw_ref, k_ref, v_ref, a_ref, b_ref, g_ref, rk_ref, lnw_ref, lnb_ref, seg_ref,
         o_ref, s_out, state, ybuf) = refs
        s_in = None
    tstep = pl.program_id(2)
    lane = lax.broadcasted_iota(jnp.int32, (1, LANES), 1)
    m0 = jnp.where(lane < HEAD_DIM, 1.0, 0.0).astype(F32)
    m1 = 1.0 - m0
    rowi = lax.broadcasted_iota(jnp.int32, (LANES, LANES), 0)
    coli = lax.broadcasted_iota(jnp.int32, (LANES, LANES), 1)
    bdmask = jnp.where((rowi >= HEAD_DIM) == (coli >= HEAD_DIM), 1.0, 0.0).astype(F32)
    half = LANES // 2

    @pl.when(tstep == 0)
    def _():
        for ib in range(bb):
            for ip in range(npair):
                if has_init:
                    ss = s_in[ib, ip]
                    state[ib * npair + ip] = jnp.concatenate([ss * m0, ss * m1], axis=0)
                else:
                    state[ib * npair + ip] = jnp.zeros((LANES, LANES), F32)

    ti = lax.broadcasted_iota(jnp.int32, (c, c), 0)
    tj = lax.broadcasted_iota(jnp.int32, (c, c), 1)
    tri = jnp.where(tj <= ti, 1.0, 0.0).astype(F32)
    msk = _wkv_masks(c)
    n_rounds = max(c.bit_length() - 2, 0)

    def chunk_body(ci, carry):
        t0 = pl.multiple_of(ci * c, c)
        for ib in range(bb):
            for ip in range(npair):
                ls = slice(ip * LANES, (ip + 1) * LANES)
                rows = pl.ds(t0, c)
                y, ns = _wkv_chunk(
                    c, n_rounds,
                    r_ref[ib, rows, ls], lw_ref[ib, rows, ls], k_ref[ib, rows, ls],
                    v_ref[ib, rows, ls], a_ref[ib, rows, ls], b_ref[ib, rows, ls],
                    state[ib * npair + ip], tri, msk, m0, m1, bdmask)
                state[ib * npair + ip] = ns
                ybuf[ib, rows, ls] = y
        return carry

    lax.fori_loop(0, n_chunks, chunk_body, 0)

    seg = seg_ref[...]
    for ib in range(bb):
        for ip in range(npair):
            ls = slice(ip * LANES, (ip + 1) * LANES)
            y = ybuf[ib, :, ls]
            mu = _split_dot(y, seg) * (1.0 / HEAD_DIM)
            yc = y - mu
            var = _split_dot(yc * yc, seg) * (1.0 / HEAD_DIM)
            yn = yc * lax.rsqrt(var + GN_EPS) * lnw_ref[:, ls] + lnb_ref[:, ls]
            rr = r_ref[ib, :, ls]
            bonus = _split_dot(rr * k_ref[ib, :, ls] * rk_ref[:, ls], seg) * v_ref[ib, :, ls]
            o_ref[ib, :, ls] = ((yn + bonus) * g_ref[ib, :, ls]).astype(o_ref.dtype)

    @pl.when(tstep == pl.num_programs(2) - 1)
    def _():
        for ib in range(bb):
            for ip in range(npair):
                st = state[ib * npair + ip]
                s_out[ib, ip] = st[:half] + st[half:]


def _wkv(r, lw, k, v, a, b, g, w, s_init, chunk, tblk, bb, npair):
    bsz, t, d = r.shape
    np_total = d // LANES
    assert bsz % bb == 0 and np_total % npair == 0 and t % tblk == 0 and tblk % chunk == 0
    has_init = s_init is not None
    blk = pl.BlockSpec((bb, tblk, npair * LANES), lambda ib, ip, it: (ib, it, ip))
    par = pl.BlockSpec((1, npair * LANES), lambda ib, ip, it: (0, ip))
    st_spec = pl.BlockSpec((bb, npair, HEAD_DIM, LANES), lambda ib, ip, it: (ib, ip, 0, 0))
    in_specs = [blk] * 7 + [par] * 3 + [_const_spec(w['seg2'].shape)] + ([st_spec] if has_init else [])
    args = [r, lw, k, v, a, b, g, w['rk'], w['lnw'], w['lnb'], w['seg2']] + ([s_init] if has_init else [])
    return pl.pallas_call(
        functools.partial(_wkv_kernel, has_init, chunk, tblk // chunk, bb, npair),
        out_shape=[jax.ShapeDtypeStruct((bsz, t, d), BF16),
                   jax.ShapeDtypeStruct((bsz, np_total, HEAD_DIM, LANES), F32)],
        grid=(bsz // bb, np_total // npair, t // tblk),
        in_specs=in_specs,
        out_specs=[blk, st_spec],
        scratch_shapes=[pltpu.VMEM((bb * npair, LANES, LANES), F32),
                        pltpu.VMEM((bb, tblk, npair * LANES), F32)],
        compiler_params=_cparams(3),
        name="wkv",
    )(*args)


def _ffn_kernel(carry_mode, seq_len, tiles_per_seq, n_f, final_norm, *refs):
    refs = list(refs)
    x_ref, o_ref, p_ref = refs[:3]
    pos = 3
    if not carry_mode:
        ig1, ig2, iv1, iv2 = refs[pos:pos + 4]
        pos += 4
    (wo_ref, gffn_ref, wug_ref, wuv_ref, cwg_ref, cwv_ref, cbg_ref, cbv_ref, wd_ref,
     wg_ref, wp_ref, gout_ref) = refs[pos:pos + 12]
    pos += 12
    y_o, ug_o, uv_o = refs[pos:pos + 3]
    pos += 3
    x1_s, hn_s, acc_s = refs[pos:pos + 3]
    pos += 3
    if carry_mode:
        cg_s, cv_s = refs[pos:pos + 2]
    i = pl.program_id(0)
    f = pl.program_id(1)

    @pl.when(f == 0)
    def _():
        x1 = x_ref[...] + jnp.dot(o_ref[...], wo_ref[...], preferred_element_type=F32)
        x1_s[...] = x1
        hn_s[...] = _rms(x1, gffn_ref[...]).astype(BF16)
        acc_s[...] = jnp.zeros_like(acc_s)

    hn = hn_s[...]
    ug = jnp.dot(hn, wug_ref[...], preferred_element_type=F32)
    uv = jnp.dot(hn, wuv_ref[...], preferred_element_type=F32)
    tm = ug.shape[0]

    if carry_mode:
        @pl.when(i % tiles_per_seq == 0)
        def _():
            cg_s[f] = jnp.zeros(cg_s.shape[1:], F32)
            cv_s[f] = jnp.zeros(cv_s.shape[1:], F32)
        cg = cg_s[f]
        cv = cv_s[f]
        ug1, ug2 = _prev_rows_carry(ug, 1, cg), _prev_rows_carry(ug, 2, cg)
        uv1, uv2 = _prev_rows_carry(uv, 1, cv), _prev_rows_carry(uv, 2, cv)
        cg_s[f] = ug[tm - SUBLANES:tm, :]
        cv_s[f] = uv[tm - SUBLANES:tm, :]
        ug_o[0] = ug[tm - SUBLANES:tm, :]
        uv_o[0] = uv[tm - SUBLANES:tm, :]
    else:
        ug1 = _prev_rows_init(ug, 1, seq_len, ig1[...])
        ug2 = _prev_rows_init(ug, 2, seq_len, ig2[...])
        uv1 = _prev_rows_init(uv, 1, seq_len, iv1[...])
        uv2 = _prev_rows_init(uv, 2, seq_len, iv2[...])
        ug_o[...] = ug
        uv_o[...] = uv
    gate = cbg_ref[...] + cwg_ref[0:1, :] * ug2 + cwg_ref[1:2, :] * ug1 + cwg_ref[2:3, :] * ug
    val = cbv_ref[...] + cwv_ref[0:1, :] * uv2 + cwv_ref[1:2, :] * uv1 + cwv_ref[2:3, :] * uv
    act = gate * _sigmoid(gate) * val
    acc_s[...] += jnp.dot(act.astype(BF16), wd_ref[...], preferred_element_type=F32)

    @pl.when(f == n_f - 1)
    def _():
        x2 = x1_s[...] + acc_s[...]
        x3 = x2 + _sigmoid(_dot(x2, wg_ref[...])) * _dot(p_ref[...], wp_ref[...])
        if final_norm:
            x3 = _rms(x3, gout_ref[...])
        y_o[...] = x3


def _ffn(x, o, p, seq_len, conv_init, w, final_norm, tm, fc):
    n, d = x.shape
    dff = w['wd'].shape[0]
    pdim = p.shape[1]
    carry_mode = conv_init is None
    tm = min(tm, n)
    assert n % tm == 0 and dff % fc == 0
    if carry_mode:
        assert seq_len % tm == 0
    else:
        assert tm % seq_len == 0
    tiles_per_seq = max(seq_len // tm, 1)
    n_f = dff // fc
    rowd = pl.BlockSpec((tm, d), lambda i, f: (i, 0))
    rowp = pl.BlockSpec((tm, pdim), lambda i, f: (i, 0))
    rowf = pl.BlockSpec((tm, fc), lambda i, f: (i, f))
    in_specs = [rowd, rowd, rowp]
    args = [x, o, p]
    if not carry_mode:
        in_specs += [rowf] * 4
        args += list(conv_init)
    fcol = pl.BlockSpec((d, fc), lambda i, f: (0, f))
    fcol_v = pl.BlockSpec((d, fc), lambda i, f: (0, n_f + f))
    c3 = pl.BlockSpec((3, fc), lambda i, f: (0, f))
    c3_v = pl.BlockSpec((3, fc), lambda i, f: (0, n_f + f))
    c1 = pl.BlockSpec((1, fc), lambda i, f: (0, f))
    c1_v = pl.BlockSpec((1, fc), lambda i, f: (0, n_f + f))
    in_specs += [_const_spec(w['wo'].shape), _const_spec(w['gffn'].shape), fcol, fcol_v, c3, c3_v, c1, c1_v,
                 pl.BlockSpec((fc, d), lambda i, f: (f, 0)),
                 _const_spec(w['wg'].shape), _const_spec(w['wp'].shape), _const_spec(w['gout'].shape)]
    args += [w['wo'], w['gffn'], w['wup'], w['wup'], w['cw'], w['cw'], w['cb'], w['cb'], w['wd'],
             w['wg'], w['wp'], w['gout']]
    if carry_mode:
        u_shape = jax.ShapeDtypeStruct((n // tm, SUBLANES, dff), F32)
        u_spec = pl.BlockSpec((1, SUBLANES, fc), lambda i, f: (i, 0, f))
        scratch_c = [pltpu.VMEM((n_f, SUBLANES, fc), F32)] * 2
    else:
        u_shape = jax.ShapeDtypeStruct((n, dff), F32)
        u_spec = rowf
        scratch_c = []
    return pl.pallas_call(
        functools.partial(_ffn_kernel, carry_mode, seq_len, tiles_per_seq, n_f, final_norm),
        out_shape=[jax.ShapeDtypeStruct((n, d), F32), u_shape, u_shape],
        grid=(n // tm, n_f),
        in_specs=in_specs,
        out_specs=[rowd, u_spec, u_spec],
        scratch_shapes=[pltpu.VMEM((tm, d), F32), pltpu.VMEM((tm, d), BF16), pltpu.VMEM((tm, d), F32)] + scratch_c,
        compiler_params=_cparams(2),
        name="ffn",
    )(*args)


def _qkv_kernel(n_blk, with_kv16, *refs):
    if with_kv16:
        (x_ref, g_ref, w_ref, cos_ref, s1_ref, s2_ref, q_o, k_o, v_o, k16_o, vt16_o, ksum_o) = refs
    else:
        (x_ref, g_ref, w_ref, cos_ref, s1_ref, s2_ref, q_o, k_o, v_o) = refs
    hn = _rms(x_ref[...], g_ref[...]).astype(BF16)
    d = hn.shape[1]
    reps = d // LANES
    cos = jnp.concatenate([cos_ref[...]] * reps, axis=1)
    s1 = jnp.concatenate([s1_ref[...]] * reps, axis=1)
    s2 = jnp.concatenate([s2_ref[...]] * reps, axis=1)
    half = ROT_DIM // 2

    def rope(z):
        return z * cos + pltpu.roll(z, d - half, axis=1) * s1 + pltpu.roll(z, half, axis=1) * s2

    q = rope(jnp.dot(hn, w_ref[:, 0:d], preferred_element_type=F32)) * (HEAD_DIM ** -0.5)
    k = rope(jnp.dot(hn, w_ref[:, d:2 * d], preferred_element_type=F32))
    v = jnp.dot(hn, w_ref[:, 2 * d:3 * d], preferred_element_type=F32)
    q_o[...] = q
    k_o[...] = k
    v_o[...] = v
    if with_kv16:
        k16_o[...] = k.astype(BF16)
        vt16_o[0] = v.T.astype(BF16)
        for j in range(n_blk):
            ksum_o[j] = jnp.sum(k[j * MOBA_BLOCK:(j + 1) * MOBA_BLOCK, :], axis=0, keepdims=True)


def _qkv(x, seq_len, gmix, wqkv, cos, s1, s2, with_kv16, tm):
    n, d = x.shape
    tm = min(tm, n)
    assert n % tm == 0
    row = pl.BlockSpec((tm, d), lambda i: (i, 0))
    rowt = pl.BlockSpec((tm, LANES), lambda i: (i, 0))
    big = jax.ShapeDtypeStruct((n, d), F32)
    out_shape = [big, big, big]
    out_specs = [row, row, row]
    n_blk = 0
    if with_kv16:
        assert tm % MOBA_BLOCK == 0 and seq_len % tm == 0
        n_blk = tm // MOBA_BLOCK
        tps = seq_len // tm
        out_shape += [jax.ShapeDtypeStruct((n, d), BF16), jax.ShapeDtypeStruct((n // seq_len, d, seq_len), BF16),
                      jax.ShapeDtypeStruct((n // MOBA_BLOCK, 1, d), F32)]
        out_specs += [row, pl.BlockSpec((1, d, tm), lambda i: (i // tps, 0, i % tps)),
                      pl.BlockSpec((n_blk, 1, d), lambda i: (i, 0, 0))]
    return pl.pallas_call(
        functools.partial(_qkv_kernel, n_blk, with_kv16),
        out_shape=out_shape,
        grid=(n // tm,),
        in_specs=[row, _const_spec(gmix.shape), _const_spec(wqkv.shape), rowt, rowt, rowt],
        out_specs=out_specs,
        compiler_params=_cparams(1),
        name="qkv",
    )(x, gmix, wqkv, cos, s1, s2)


def _moba_kernel(n_blocks, q_ref, k_ref, vt_ref, km_ref, o_ref, sel_s):
    qi = pl.program_id(2)
    blk = MOBA_BLOCK
    q = q_ref[0]
    lane = lax.broadcasted_iota(jnp.int32, (1, LANES), 1)
    km = km_ref[0]
    jrow = lax.broadcasted_iota(jnp.int32, (n_blocks, 1), 0)
    jrow_f = jrow.astype(F32)
    past = jrow < qi
    neg_inf = jnp.float32(-jnp.inf)
    qh16 = []
    for h in range(2):
        hm = (lane >= HEAD_DIM) if h else (lane < HEAD_DIM)
        qh = jnp.where(hm, q, 0.0)
        qh16.append(qh.astype(BF16))
        gate = lax.dot_general(km, qh, (((1,), (1,)), ((), ())), preferred_element_type=F32,
                               precision=lax.Precision.HIGHEST)
        cur = jnp.where(past, gate, neg_inf)
        sel = jnp.zeros(cur.shape, F32)
        for _ in range(MOBA_TOPK):
            mx = jnp.max(cur, axis=0, keepdims=True)
            first = jnp.min(jnp.where(cur == mx, jrow_f, float(n_blocks)), axis=0, keepdims=True)
            onehot = jrow_f == first
            pick = jnp.logical_and(onehot, mx > neg_inf)
            sel = jnp.where(pick, 1.0, sel)
            cur = jnp.where(onehot, neg_inf, cur)
        for j in range(n_blocks):
            sel_s[h * n_blocks + j] = jnp.broadcast_to(sel[j:j + 1, :], (SUBLANES, blk))

    def partial_attn(j, h, diag):
        k0 = pl.multiple_of(j * blk, blk)
        kj = k_ref[0, pl.ds(k0, blk), :]
        vtj = vt_ref[0, h * HEAD_DIM:(h + 1) * HEAD_DIM, pl.ds(k0, blk)]
        st = lax.dot_general(kj, qh16[h], (((1,), (1,)), ((), ())), preferred_element_type=F32)
        if diag:
            kr = lax.broadcasted_iota(jnp.int32, (blk, blk), 0)
            qc = lax.broadcasted_iota(jnp.int32, (blk, blk), 1)
            st = jnp.where(kr <= qc, st, NEG_BIG)
        mj = jnp.max(st, axis=0, keepdims=True)
        p = jnp.exp(st - mj)
        lj = jnp.sum(p, axis=0, keepdims=True)
        oj = jnp.dot(vtj, p.astype(BF16), preferred_element_type=F32)
        return mj, lj, oj

    def merge(state, part, selj):
        m, l, acc = state
        mj, lj, oj = part
        m_new = jnp.where(selj, jnp.maximum(m, mj), m)
        alpha = jnp.exp(m - m_new)
        beta = jnp.where(selj, jnp.exp(mj - m_new), 0.0)
        return m_new, l * alpha + lj * beta, acc * alpha + oj * beta

    def body(j, carry):
        out = []
        for h in range(2):
            selj = sel_s[h * n_blocks + j][0:1, :] > 0.5
            out.append(merge(carry[h], partial_attn(j, h, False), selj))
        return tuple(out)

    init = tuple((jnp.full((1, blk), NEG_BIG, F32), jnp.zeros((1, blk), F32),
                  jnp.zeros((HEAD_DIM, blk), F32)) for _ in range(2))
    carry = lax.fori_loop(0, qi, body, init)
    outs = []
    for h in range(2):
        m, l, acc = merge(carry[h], partial_attn(qi, h, True), jnp.full((1, blk), True))
        outs.append(acc / l)
    o_ref[0] = jnp.concatenate(outs, axis=0).T.astype(o_ref.dtype)


def _moba_prompt(q, k16, vt16, kmean):
    bsz, t, d = q.shape
    n_blocks = t // MOBA_BLOCK
    return pl.pallas_call(
        functools.partial(_moba_kernel, n_blocks),
        out_shape=jax.ShapeDtypeStruct((bsz, t, d), BF16),
        grid=(bsz, d // LANES, n_blocks),
        in_specs=[pl.BlockSpec((1, MOBA_BLOCK, LANES), lambda b, p, i: (b, i, p)),
                  pl.BlockSpec((1, t, LANES), lambda b, p, i: (b, 0, p)),
                  pl.BlockSpec((1, LANES, t), lambda b, p, i: (b, p, 0)),
                  pl.BlockSpec((1, n_blocks, LANES), lambda b, p, i: (b, 0, p))],
        out_specs=pl.BlockSpec((1, MOBA_BLOCK, LANES), lambda b, p, i: (b, i, p)),
        scratch_shapes=[pltpu.VMEM((2 * n_blocks, SUBLANES, MOBA_BLOCK), F32)],
        compiler_params=_cparams(3),
        name="moba_prompt",
    )(q, k16, vt16, kmean)


def _moba_decode_kernel(n_tok, n_heads, pages_per_step, n_steps, pt_ref, q_ref, kn_ref, vn_ref, *refs):
    kp = refs[:pages_per_step]
    vp = refs[pages_per_step:2 * pages_per_step]
    o_ref = refs[2 * pages_per_step]
    m_s, l_s, acc_s, ks_s = refs[2 * pages_per_step + 1:]
    step = pl.program_id(1)
    d = q_ref.shape[2]
    page = kp[0].shape[2]
    pages_per_blk = MOBA_BLOCK // page
    blk_per_step = pages_per_step // pages_per_blk
    n_past = n_steps * blk_per_step
    nrow = n_tok * n_heads
    q = q_ref[0]
    lane_head = lax.broadcasted_iota(jnp.int32, (n_heads, d), 1) // HEAD_DIM
    row_head = lax.broadcasted_iota(jnp.int32, (n_heads, d), 0)
    hmask = lane_head == row_head
    qbd = jnp.concatenate(
        [jnp.where(hmask, jnp.broadcast_to(q[t:t + 1, :], (n_heads, d)), 0.0) for t in range(n_tok)], axis=0)
    qbd16 = qbd.astype(BF16)

    for jb in range(blk_per_step):
        ks = [kp[jb * pages_per_blk + i][0, 0] for i in range(pages_per_blk)]
        vs = [vp[jb * pages_per_blk + i][0, 0] for i in range(pages_per_blk)]
        s = jnp.concatenate([_dot_nt(qbd16, kpg) for kpg in ks], axis=1)
        mj = jnp.max(s, axis=1, keepdims=True)
        p = jnp.exp(s - mj)
        lj = jnp.sum(p, axis=1, keepdims=True)
        oj = sum(_dot(p[:, i * page:(i + 1) * page], vs[i]) for i in range(pages_per_blk))
        ksum = sum(jnp.sum(kpg, axis=0, keepdims=True) for kpg in ks)
        slot = step * blk_per_step + jb
        m_s[slot] = jnp.broadcast_to(mj, (nrow, LANES))
        l_s[slot] = jnp.broadcast_to(lj, (nrow, LANES))
        acc_s[slot] = oj
        ks_s[slot] = jnp.broadcast_to(ksum, (SUBLANES, d))

    @pl.when(step == n_steps - 1)
    def _():
        inv_blk = 1.0 / MOBA_BLOCK
        gates = [jnp.sum(qbd * (ks_s[j][0:1, :] * inv_blk), axis=1, keepdims=True) for j in range(n_past)]
        kn = kn_ref[0]
        vn = vn_ref[0]
        trow = lax.broadcasted_iota(jnp.int32, (nrow, 1), 0) // n_heads
        s_own = [jnp.where(trow >= t, jnp.sum(qbd * kn[t:t + 1, :], axis=1, keepdims=True), NEG_BIG)
                 for t in range(n_tok)]
        m = functools.reduce(jnp.maximum, s_own)
        p_own = [jnp.exp(s - m) for s in s_own]
        l = sum(p_own)
        acc = sum(p_own[t] * vn[t:t + 1, :] for t in range(n_tok))
        for j in range(n_past):
            rank = sum(jnp.where(jnp.logical_or(gates[i] > gates[j],
                                                jnp.logical_and(gates[i] == gates[j], i < j)), 1.0, 0.0)
                       for i in range(n_past) if i != j)
            selj = rank < float(MOBA_TOPK)
            mj = m_s[j][:, 0:1]
            m_new = jnp.where(selj, jnp.maximum(m, mj), m)
            alpha = jnp.exp(m - m_new)
            beta = jnp.where(selj, jnp.exp(mj - m_new), 0.0)
            l = l * alpha + l_s[j][:, 0:1] * beta
            acc = acc * alpha + acc_s[j] * beta
            m = m_new
        lane_head_r = lax.broadcasted_iota(jnp.int32, (nrow, d), 1) // HEAD_DIM
        row_head_r = lax.broadcasted_iota(jnp.int32, (nrow, d), 0) % n_heads
        out = jnp.where(lane_head_r == row_head_r, acc / l, 0.0)
        o_ref[0] = jnp.concatenate(
            [jnp.sum(out[t * n_heads:(t + 1) * n_heads], axis=0, keepdims=True) for t in range(n_tok)],
            axis=0).astype(o_ref.dtype)


def _moba_decode(q, kn, vn, cache_k, cache_v, page_table, pages_per_step):
    bsz, n_tok, d = q.shape
    n_heads = d // HEAD_DIM
    pool, page, _ = cache_k.shape
    n_pages = page_table.shape[1]
    assert n_pages % pages_per_step == 0 and MOBA_BLOCK % page == 0
    assert (n_pages * page) % MOBA_BLOCK == 0 and n_tok <= MOBA_BLOCK
    assert pages_per_step % (MOBA_BLOCK // page) == 0
    n_steps = n_pages // pages_per_step
    n_past = n_pages * page // MOBA_BLOCK
    nrow = n_tok * n_heads
    ck = cache_k.reshape(pool, 1, page, d)
    cv = cache_v.reshape(pool, 1, page, d)
    tok = pl.BlockSpec((1, n_tok, d), lambda b, s, pt: (b, 0, 0))

    def page_spec(i):
        return pl.BlockSpec((1, 1, page, d), lambda b, s, pt: (pt[b, s * pages_per_step + i], 0, 0, 0))

    grid_spec = pltpu.PrefetchScalarGridSpec(
        num_scalar_prefetch=1,
        grid=(bsz, n_steps),
        in_specs=[tok, tok, tok] + [page_spec(i) for i in range(pages_per_step)] * 2,
        out_specs=tok,
        scratch_shapes=[pltpu.VMEM((n_past, nrow, LANES), F32), pltpu.VMEM((n_past, nrow, LANES), F32),
                        pltpu.VMEM((n_past, nrow, d), F32), pltpu.VMEM((n_past, SUBLANES, d), F32)],
    )
    return pl.pallas_call(
        functools.partial(_moba_decode_kernel, n_tok, n_heads, pages_per_step, n_steps),
        out_shape=jax.ShapeDtypeStruct((bsz, n_tok, d), BF16),
        grid_spec=grid_spec,
        compiler_params=_cparams(2),
        name="moba_decode",
    )(page_table, q, kn, vn, *([ck] * pages_per_step), *([cv] * pages_per_step))


def _pad_cols(w, n):
    return jnp.pad(w, ((0, 0), (0, n - w.shape[1])))


def _pad_rows(w, n):
    return jnp.pad(w, ((0, n - w.shape[0]), (0, 0)))


def _rope_tables(pos):
    half = ROT_DIM // 2
    inv = ROPE_THETA ** (-2.0 * jnp.arange(half, dtype=F32) / ROT_DIM)
    ang = pos.astype(F32)[:, None] * inv[None, :]
    cos = jnp.cos(ang)
    sin = jnp.sin(ang)
    t = pos.shape[0]
    ones = jnp.ones((t, HEAD_DIM - ROT_DIM), F32)
    zeros_r = jnp.zeros((t, HEAD_DIM - ROT_DIM), F32)
    zeros_h = jnp.zeros((t, half), F32)
    c = jnp.concatenate([cos, cos, ones], axis=1)
    s1 = jnp.concatenate([-sin, zeros_h, zeros_r], axis=1)
    s2 = jnp.concatenate([zeros_h, sin, zeros_r], axis=1)
    rep = LANES // HEAD_DIM
    return jnp.tile(c, (1, rep)), jnp.tile(s1, (1, rep)), jnp.tile(s2, (1, rep))


def _pack_state(s):
    b, h, dv, dk = s.shape
    return s.reshape(b, h // 2, 2, dv, dk).transpose(0, 1, 3, 2, 4).reshape(b, h // 2, dv, 2 * dk)


def _unpack_state(s):
    b, hp, dv, dk2 = s.shape
    return s.reshape(b, hp, dv, 2, dk2 // 2).transpose(0, 1, 3, 2, 4).reshape(b, hp * 2, dv, dk2 // 2)


def kernel(x_prompt, x_sample, state_wkv, state_shift, cache_k, cache_v, state_conv, page_table, p_prompt, p_sample, rw_mix, rw_rkv, rw_w0, rw_w1, rw_w2, rw_a0, rw_a1, rw_a2, rw_g1, rw_g2, rw_kk, rw_ka, rw_rk, rw_lnw, rw_lnb, rw_wo, mb_wqkv, mb_wo, norm_mix, norm_ffn, ff_wup, ff_conv_w, ff_conv_b, ff_wdown, ple_wp, ple_wg, norm_out):
    bp, tp, d = x_prompt.shape
    bs, ts, _ = x_sample.shape
    depth = norm_mix.shape[0]
    n_heads = d // HEAD_DIM
    dff = ff_wdown.shape[1]
    past_len = page_table.shape[1] * cache_k.shape[2]
    assert d % LANES == 0 and depth == 2

    lane_head = jnp.arange(d) // HEAD_DIM
    segr = (lane_head[:, None] == jnp.arange(LANES)[None, :]).astype(BF16)
    segb = segr.T
    seg2 = ((jnp.arange(LANES) // HEAD_DIM)[:, None] == (jnp.arange(LANES) // HEAD_DIM)[None, :]).astype(BF16)

    def row(v):
        return v.reshape(1, -1).astype(F32)

    rw = dict(
        gmix=row(norm_mix[0]), mix=rw_mix[0], wrkv=rw_rkv[0].astype(BF16), w0=row(rw_w0[0]),
        w1=_pad_cols(rw_w1[0], LANES).astype(BF16), w2=_pad_rows(rw_w2[0], LANES).astype(BF16),
        a0=row(rw_a0[0]), a1=_pad_cols(rw_a1[0], LANES).astype(BF16), a2=_pad_rows(rw_a2[0], LANES).astype(BF16),
        g1=_pad_cols(rw_g1[0], 2 * LANES).astype(BF16), g2=_pad_rows(rw_g2[0], 2 * LANES).astype(BF16),
        kk=row(rw_kk[0]), ka=row(rw_ka[0]), segr=segr, segb=segb,
        rk=row(rw_rk[0]), lnw=row(rw_lnw[0]), lnb=row(rw_lnb[0]), seg2=seg2)

    def ffw(i, wo):
        return dict(wo=wo.astype(BF16), gffn=row(norm_ffn[i]), wup=ff_wup[i].astype(BF16), cw=ff_conv_w[i],
                    cb=row(ff_conv_b[i]), wd=ff_wdown[i].astype(BF16), wg=ple_wg[i].astype(BF16),
                    wp=ple_wp[i].astype(BF16), gout=row(norm_out))

    fw0 = ffw(0, rw_wo[0])
    fw1 = ffw(1, mb_wo[0])
    wqkv = mb_wqkv[0].astype(BF16)
    gmix1 = row(norm_mix[1])

    xp = x_prompt.reshape(bp * tp, d)
    xs = x_sample.reshape(bs * ts, d)
    pp = p_prompt.reshape(depth, bp * tp, -1)
    ps = p_sample.reshape(depth, bs * ts, -1)

    r, lw, k, v, a, b, g, shift_p = _rwkv_proj(xp, tp, None, rw)
    sh = lambda z: z.reshape(bp, tp, d)
    o_p, wkv_p = _wkv(sh(r), sh(lw), sh(k), sh(v), sh(a), sh(b), sh(g), rw, None,
                      chunk=64, tblk=min(512, tp), bb=bp, npair=4)
    def conv_state_p(ug, uv):
        last = lambda z: z.reshape(bp, -1, SUBLANES, dff)[:, -1, SUBLANES - 2:]
        return jnp.concatenate([last(ug), last(uv)], axis=-1)

    xp, ugp, uvp = _ffn(xp, o_p.reshape(bp * tp, d), pp[0], tp, None, fw0, False, 512, 256)
    conv_p0 = conv_state_p(ugp, uvp)
    init_shift = jnp.repeat(state_shift[0], ts, axis=0)
    r, lw, k, v, a, b, g, xn_s = _rwkv_proj(xs, ts, init_shift, rw)
    shift_s = xn_s.reshape(bs, ts, d)[:, -1]
    tpad = -(-ts // SUBLANES) * SUBLANES
    shs = lambda z: jnp.pad(z.reshape(bs, ts, d), ((0, 0), (0, tpad - ts), (0, 0)))
    o_s, wkv_s = _wkv(shs(r), shs(lw), shs(k), shs(v), shs(a), shs(b), shs(g), rw, _pack_state(state_wkv[0]),
                      chunk=tpad, tblk=tpad, bb=4, npair=4)
    o_s = o_s[:, :ts].reshape(bs * ts, d)

    def conv_inits(sc):
        i1 = jnp.broadcast_to(sc[:, 1:2], (bs, ts, 2 * dff)).reshape(bs * ts, 2 * dff)
        i2 = jnp.concatenate([sc, jnp.zeros((bs, ts - 2, 2 * dff), F32)], axis=1).reshape(bs * ts, 2 * dff)
        return (i1[:, :dff], i2[:, :dff], i1[:, dff:], i2[:, dff:])

    xs, ugs, uvs = _ffn(xs, o_s, ps[0], ts, conv_inits(state_conv[0]), fw0, False, 512, 256)
    conv_s0 = jnp.concatenate([ugs, uvs], axis=-1).reshape(bs, ts, 2 * dff)[:, ts - 2:]

    cos, s1, s2 = _rope_tables(jnp.arange(tp, dtype=jnp.int32))
    cos_p, s1_p, s2_p = (jnp.tile(z, (bp, 1)) for z in (cos, s1, s2))
    q, k, v, k16, vt16, ksum = _qkv(xp, tp, gmix1, wqkv, cos_p, s1_p, s2_p, True, 256)
    k_p = k.reshape(bp, tp, n_heads, HEAD_DIM)
    v_p = v.reshape(bp, tp, n_heads, HEAD_DIM)
    kmean = ksum.reshape(bp, tp // MOBA_BLOCK, d) * (1.0 / MOBA_BLOCK)
    o_p = _moba_prompt(q.reshape(bp, tp, d), k16.reshape(bp, tp, d), vt16, kmean)
    xp, ugp, uvp = _ffn(xp, o_p.reshape(bp * tp, d), pp[1], tp, None, fw1, True, 512, 256)
    conv_p1 = conv_state_p(ugp, uvp)

    cos, s1, s2 = _rope_tables(past_len + jnp.arange(ts, dtype=jnp.int32))
    cos_s, s1_s, s2_s = (jnp.tile(z, (bs, 1)) for z in (cos, s1, s2))
    q, k, v = _qkv(xs, ts, gmix1, wqkv, cos_s, s1_s, s2_s, False, 512)
    k_s = k.reshape(bs, ts, n_heads, HEAD_DIM)
    v_s = v.reshape(bs, ts, n_heads, HEAD_DIM)
    pool, page = cache_k.shape[1], cache_k.shape[2]
    o_s = _moba_decode(q.reshape(bs, ts, d), k.reshape(bs, ts, d), v.reshape(bs, ts, d),
                       cache_k[0].reshape(pool, page, d), cache_v[0].reshape(pool, page, d), page_table, 4)
    xs, ugs, uvs = _ffn(xs, o_s.reshape(bs * ts, d), ps[1], ts, conv_inits(state_conv[1]), fw1, True, 512, 256)
    conv_s1 = jnp.concatenate([ugs, uvs], axis=-1).reshape(bs, ts, 2 * dff)[:, ts - 2:]

    return (xp.reshape(bp, tp, d), xs.reshape(bs, ts, d),
            _unpack_state(wkv_p)[None], shift_p.reshape(1, bp, d), k_p[None], v_p[None],
            jnp.stack([conv_p0, conv_p1]),
            _unpack_state(wkv_s)[None], shift_s[None], k_s[None], v_s[None],
            jnp.stack([conv_s0, conv_s1]))
```

```python
import functools

import jax
import jax.numpy as jnp
from jax import lax
from jax.experimental import pallas as pl
from jax.experimental.pallas import tpu as pltpu

F32 = jnp.float32
BF16 = jnp.bfloat16

HEAD_DIM = 64
GN_EPS = 64e-5
RMS_EPS = 1e-6
MOBA_BLOCK = 256
MOBA_TOPK = 3
ROPE_THETA = 500000.0
ROT_DIM = HEAD_DIM // 4
NEG_BIG = -1e30
LANES = 128
SUBLANES = 8
VMEM_LIMIT_BYTES = 56 * 1024 * 1024


def _cparams(n_axes):
    return pltpu.CompilerParams(dimension_semantics=("arbitrary",) * n_axes,
                                vmem_limit_bytes=VMEM_LIMIT_BYTES)


def _const_spec(shape):
    nd = len(shape)
    return pl.BlockSpec(shape, lambda *_: (0,) * nd)


def _dot(a, b):
    return jnp.dot(a.astype(BF16), b.astype(BF16), preferred_element_type=F32)


def _dot_nt(a, b):
    return lax.dot_general(a.astype(BF16), b.astype(BF16), (((1,), (1,)), ((), ())),
                           preferred_element_type=F32)


def _dot_tn(a, b):
    return lax.dot_general(a.astype(BF16), b.astype(BF16), (((0,), (0,)), ((), ())),
                           preferred_element_type=F32)


def _split_dot(x, m):
    hi = x.astype(BF16)
    lo = (x - hi.astype(F32)).astype(BF16)
    return (jnp.dot(hi, m, preferred_element_type=F32) + jnp.dot(lo, m, preferred_element_type=F32))


def _sigmoid(z):
    return 1.0 / (1.0 + jnp.exp(-z))


def _rms(x, g):
    return x * lax.rsqrt(jnp.mean(x * x, axis=-1, keepdims=True) + RMS_EPS) * g


def _seg_sum(x, segr, segb):
    return _split_dot(_split_dot(x, segr), segb)


def _prev_rows_carry(u, k, carry):
    rolled = pltpu.roll(u, k, axis=0)
    row = lax.broadcasted_iota(jnp.int32, (u.shape[0], 1), 0)
    out = rolled
    for j in range(k):
        out = jnp.where(row == j, carry[SUBLANES - k + j:SUBLANES - k + j + 1, :], out)
    return out


def _rwkv_proj_kernel(block_mode, tiles_per_seq, *refs):
    if block_mode:
        (x_ref, init_ref, gmix_ref, mix_ref, wrkv_ref, w0_ref, w1_ref, w2_ref, a0_ref, a1_ref,
         a2_ref, g1_ref, g2_ref, kk_ref, ka_ref, segr_ref, segb_ref,
         r_o, lw_o, k_o, v_o, a_o, b_o, g_o, xn_o, carry) = refs
    else:
        (x_ref, gmix_ref, mix_ref, wrkv_ref, w0_ref, w1_ref, w2_ref, a0_ref, a1_ref, a2_ref,
         g1_ref, g2_ref, kk_ref, ka_ref, segr_ref, segb_ref,
         r_o, lw_o, k_o, v_o, a_o, b_o, g_o, xn_o, carry) = refs
    xn = _rms(x_ref[...], gmix_ref[...])
    tm = xn.shape[0]
    if block_mode:
        @pl.when(pl.program_id(0) == 0)
        def _():
            carry[...] = init_ref[...]
        xprev = carry[...]
        carry[...] = xn
        xn_o[...] = xn
    else:
        @pl.when(pl.program_id(0) % tiles_per_seq == 0)
        def _():
            carry[...] = jnp.zeros_like(carry)
        xprev = _prev_rows_carry(xn, 1, carry)
        carry[...] = xn[tm - SUBLANES:tm, :]
        xn_o[0] = xn[tm - 1:tm, :]
    dx = xprev - xn

    def xm(i):
        return (xn + dx * mix_ref[i:i + 1, :]).astype(BF16)

    r = jnp.dot(xm(0), wrkv_ref[0], preferred_element_type=F32)
    k = jnp.dot(xm(1), wrkv_ref[1], preferred_element_type=F32)
    v = jnp.dot(xm(2), wrkv_ref[2], preferred_element_type=F32)
    wl = w0_ref[...] + _dot(jnp.tanh(jnp.dot(xm(3), w1_ref[...], preferred_element_type=F32)), w2_ref[...])
    sp = jnp.maximum(-wl, 0.0) + jnp.log(1.0 + jnp.exp(-jnp.abs(wl)))
    lw = -jnp.exp(-sp - 0.5)
    a = _sigmoid(a0_ref[...] + _dot(jnp.dot(xm(4), a1_ref[...], preferred_element_type=F32), a2_ref[...]))
    g = _dot(_sigmoid(jnp.dot(xm(5), g1_ref[...], preferred_element_type=F32)), g2_ref[...])
    kk = k * kk_ref[...]
    n2 = _seg_sum(kk * kk, segr_ref[...], segb_ref[...])
    kk = kk / jnp.maximum(jnp.sqrt(n2), 1e-12)
    outs = (r, lw, k * (1.0 + (a - 1.0) * ka_ref[...]), v, -kk, kk * a, g)
    for o_ref, val in zip((r_o, lw_o, k_o, v_o, a_o, b_o, g_o), outs):
        if block_mode:
            o_ref[0] = val.T
        else:
            o_ref[...] = val


def _rwkv_proj(x, seq_len, init, w):
    n, d = x.shape
    block_mode = init is not None
    consts = [w['gmix'], w['mix'], w['wrkv'], w['w0'], w['w1'], w['w2'], w['a0'], w['a1'], w['a2'],
              w['g1'], w['g2'], w['kk'], w['ka'], w['segr'], w['segb']]
    if block_mode:
        tm = n // seq_len
        tiles_per_seq = 1
        row = pl.BlockSpec((tm, d), lambda i: (i, 0))
        in_specs = [row, _const_spec(init.shape)] + [_const_spec(c.shape) for c in consts]
        args = [x, init] + consts
        out_shape = [jax.ShapeDtypeStruct((seq_len, d, tm), F32)] * 7 + [jax.ShapeDtypeStruct((tm, d), F32)]
        out_specs = [pl.BlockSpec((1, d, tm), lambda i: (i, 0, 0))] * 7 + [_const_spec((tm, d))]
        scratch = [pltpu.VMEM((tm, d), F32)]
    else:
        tm = min(256, seq_len)
        assert seq_len % tm == 0 and n % seq_len == 0
        tiles_per_seq = seq_len // tm
        row = pl.BlockSpec((tm, d), lambda i: (i, 0))
        in_specs = [row] + [_const_spec(c.shape) for c in consts]
        args = [x] + consts
        out_shape = [jax.ShapeDtypeStruct((n, d), F32)] * 7 + [jax.ShapeDtypeStruct((n // seq_len, 1, d), F32)]
        out_specs = [row] * 7 + [pl.BlockSpec((1, 1, d), lambda i: (i // tiles_per_seq, 0, 0))]
        scratch = [pltpu.VMEM((SUBLANES, d), F32)]
    return pl.pallas_call(
        functools.partial(_rwkv_proj_kernel, block_mode, tiles_per_seq),
        out_shape=out_shape,
        grid=(n // tm,),
        in_specs=in_specs,
        out_specs=out_specs,
        scratch_shapes=scratch,
        compiler_params=_cparams(1),
        name="rwkv_proj",
    )(*args)


def _wkv_masks(c):
    n = 2 * c
    row = lax.broadcasted_iota(jnp.int32, (n, n), 0)
    col = lax.broadcasted_iota(jnp.int32, (n, n), 1)
    rq = row >= c
    cq = col >= c
    tl = jnp.logical_and(jnp.logical_not(rq), jnp.logical_not(cq))
    br = jnp.logical_and(rq, cq)
    tr = jnp.logical_and(jnp.logical_not(rq), cq)
    bl = jnp.logical_and(rq, jnp.logical_not(cq))
    strict = (col % c) < (row % c)
    incl = (col % c) <= (row % c)
    one = jnp.ones((n, n), F32)
    zero = jnp.zeros((n, n), F32)

    def f(m):
        return jnp.where(m, one, zero)

    return dict(
        ab0=f(jnp.logical_and(strict, tl)), ab1=f(jnp.logical_and(strict, br)),
        ak0=f(jnp.logical_and(strict, tr)), ak1=f(jnp.logical_and(strict, bl)),
        rb0=f(jnp.logical_and(incl, bl)), rb1=f(jnp.logical_and(incl, tr)),
        rk0=f(jnp.logical_and(incl, br)), rk1=f(jnp.logical_and(incl, tl)),
        eye=f(row == col),
    )


def _wkv_chunk(c, n_rounds, r, lw, k, v, a, b, state, tri, msk, m0, m1, bdmask):
    g_inc = jnp.dot(tri, lw, preferred_element_type=F32, precision=lax.Precision.HIGHEST)
    e_inc = jnp.exp(g_inc)
    e_exc = jnp.exp(g_inc - lw)
    e_neg = jnp.exp(-g_inc)
    at = a * e_exc
    rt = r * e_inc
    bt = b * e_neg
    kt = k * e_neg
    lr = jnp.concatenate([at, rt], axis=0)
    rl = jnp.concatenate([rt, at], axis=0)
    bk = jnp.concatenate([bt, kt], axis=0)
    kb = jnp.concatenate([kt, bt], axis=0)
    sc0 = _dot_nt(lr * m0, bk)
    sc1 = _dot_nt(rl * m1, kb)
    ab = sc0 * msk['ab0'] + sc1 * msk['ab1']
    ak = sc0 * msk['ak0'] + sc1 * msk['ak1']
    rb = sc0 * msk['rb0'] + sc1 * msk['rb1']
    rk = sc0 * msk['rk0'] + sc1 * msk['rk1']
    tinv = msk['eye'] + ab
    pw = ab
    for _ in range(n_rounds):
        pw = _dot(pw, pw)
        tinv = tinv + _dot(tinv, pw)
    xs = _dot_nt(lr, state)
    vst = jnp.concatenate([v * m1, v * m0], axis=0)
    x0 = xs[:c]
    zst = jnp.concatenate([x0 * m0, x0 * m1], axis=0) + _dot(ak, vst)
    w = _dot(tinv, zst)
    u = w[:c] + w[c:]
    ys = _dot(jnp.concatenate([rb, rk], axis=1), jnp.concatenate([w, vst], axis=0))
    y = xs[c:] + ys[:c] + ys[c:]
    ds = _dot_tn(jnp.concatenate([u, v], axis=0), bk)
    new_state = (state + ds * bdmask) * e_inc[c - 1:c, :]
    return y, new_state


def _wkv_kernel(c, n_chunks, bb, npair, r_ref, lw_ref, k_ref, v_ref, a_ref, b_ref, g_ref, rk_ref, lnw_ref,
                lnb_ref, seg_ref, o_ref, s_out, state, ybuf):
    tstep = pl.program_id(2)
    lane = lax.broadcasted_iota(jnp.int32, (1, LANES), 1)
    m0 = jnp.where(lane < HEAD_DIM, 1.0, 0.0).astype(F32)
    m1 = 1.0 - m0
    rowi = lax.broadcasted_iota(jnp.int32, (LANES, LANES), 0)
    coli = lax.broadcasted_iota(jnp.int32, (LANES, LANES), 1)
    bdmask = jnp.where((rowi >= HEAD_DIM) == (coli >= HEAD_DIM), 1.0, 0.0).astype(F32)
    half = LANES // 2

    @pl.when(tstep == 0)
    def _():
        state[...] = jnp.zeros_like(state)

    ti = lax.broadcasted_iota(jnp.int32, (c, c), 0)
    tj = lax.broadcasted_iota(jnp.int32, (c, c), 1)
    tri = jnp.where(tj <= ti, 1.0, 0.0).astype(F32)
    msk = _wkv_masks(c)
    n_rounds = max(c.bit_length() - 2, 0)

    def chunk_body(ci, carry):
        t0 = pl.multiple_of(ci * c, c)
        for ib in range(bb):
            for ip in range(npair):
                ls = slice(ip * LANES, (ip + 1) * LANES)
                rows = pl.ds(t0, c)
                y, ns = _wkv_chunk(
                    c, n_rounds,
                    r_ref[ib, rows, ls], lw_ref[ib, rows, ls], k_ref[ib, rows, ls],
                    v_ref[ib, rows, ls], a_ref[ib, rows, ls], b_ref[ib, rows, ls],
                    state[ib * npair + ip], tri, msk, m0, m1, bdmask)
                state[ib * npair + ip] = ns
                ybuf[ib, rows, ls] = y
        return carry

    lax.fori_loop(0, n_chunks, chunk_body, 0)

    seg = seg_ref[...]
    for ib in range(bb):
        for ip in range(npair):
            ls = slice(ip * LANES, (ip + 1) * LANES)
            y = ybuf[ib, :, ls]
            mu = _split_dot(y, seg) * (1.0 / HEAD_DIM)
            yc = y - mu
            var = _split_dot(yc * yc, seg) * (1.0 / HEAD_DIM)
            yn = yc * lax.rsqrt(var + GN_EPS) * lnw_ref[:, ls] + lnb_ref[:, ls]
            rr = r_ref[ib, :, ls]
            bonus = _split_dot(rr * k_ref[ib, :, ls] * rk_ref[:, ls], seg) * v_ref[ib, :, ls]
            o_ref[ib, :, ls] = ((yn + bonus) * g_ref[ib, :, ls]).astype(o_ref.dtype)

    @pl.when(tstep == pl.num_programs(2) - 1)
    def _():
        for ib in range(bb):
            for ip in range(npair):
                st = state[ib * npair + ip]
                s_out[ib, ip] = st[:half] + st[half:]


def _wkv(r, lw, k, v, a, b, g, w, chunk, tblk, bb, npair):
    bsz, t, d = r.shape
    np_total = d // LANES
    assert bsz % bb == 0 and np_total % npair == 0 and t % tblk == 0 and tblk % chunk == 0
    blk = pl.BlockSpec((bb, tblk, npair * LANES), lambda ib, ip, it: (ib, it, ip))
    par = pl.BlockSpec((1, npair * LANES), lambda ib, ip, it: (0, ip))
    st_spec = pl.BlockSpec((bb, npair, HEAD_DIM, LANES), lambda ib, ip, it: (ib, ip, 0, 0))
    return pl.pallas_call(
        functools.partial(_wkv_kernel, chunk, tblk // chunk, bb, npair),
        out_shape=[jax.ShapeDtypeStruct((bsz, t, d), BF16),
                   jax.ShapeDtypeStruct((bsz, np_total, HEAD_DIM, LANES), F32)],
        grid=(bsz // bb, np_total // npair, t // tblk),
        in_specs=[blk] * 7 + [par] * 3 + [_const_spec(w['seg2'].shape)],
        out_specs=[blk, st_spec],
        scratch_shapes=[pltpu.VMEM((bb * npair, LANES, LANES), F32),
                        pltpu.VMEM((bb, tblk, npair * LANES), F32)],
        compiler_params=_cparams(3),
        name="wkv",
    )(r, lw, k, v, a, b, g, w['rk'], w['lnw'], w['lnb'], w['seg2'])


def _wkv_decode_kernel(n_tok, hb, r_ref, lw_ref, k_ref, v_ref, a_ref, b_ref, g_ref, rk_ref, lnw_ref, lnb_ref,
                       s_in, o_ref, s_out, w_s, y_s):
    bsz = r_ref.shape[2]
    w_s[...] = jnp.exp(lw_ref[...])
    for h in range(hb):
        rows = slice(h * HEAD_DIM, (h + 1) * HEAD_DIM)

        def body(i, carry, h=h, rows=rows):
            s = s_in[h, i]
            for t in range(n_tok):
                sa = jnp.sum(s * a_ref[t, rows, :], axis=0, keepdims=True)
                vi = v_ref[t, pl.ds(h * HEAD_DIM + i, 1), :]
                s = s * w_s[t, rows, :] + sa * b_ref[t, rows, :] + vi * k_ref[t, rows, :]
                y_s[t, pl.ds(h * HEAD_DIM + i, 1), :] = jnp.sum(s * r_ref[t, rows, :], axis=0, keepdims=True)
            s_out[h, i] = s
            return carry

        lax.fori_loop(0, HEAD_DIM, body, 0)

    for t in range(n_tok):
        outs = []
        for h in range(hb):
            rows = slice(h * HEAD_DIM, (h + 1) * HEAD_DIM)
            y = y_s[t, rows, :]
            mu = jnp.mean(y, axis=0, keepdims=True)
            yc = y - mu
            var = jnp.mean(yc * yc, axis=0, keepdims=True)
            yn = yc * lax.rsqrt(var + GN_EPS) * lnw_ref[rows, :] + lnb_ref[rows, :]
            bonus = jnp.sum(r_ref[t, rows, :] * k_ref[t, rows, :] * rk_ref[rows, :], axis=0,
                            keepdims=True) * v_ref[t, rows, :]
            outs.append((yn + bonus) * g_ref[t, rows, :])
        o_ref[t * bsz:(t + 1) * bsz, :] = jnp.concatenate(outs, axis=0).T.astype(o_ref.dtype)


def _wkv_decode(r, lw, k, v, a, b, g, rk, lnw, lnb, state, hb):
    n_tok, d, bsz = r.shape
    n_heads = d // HEAD_DIM
    assert n_heads % hb == 0
    blk = pl.BlockSpec((n_tok, hb * HEAD_DIM, bsz), lambda i: (0, i, 0))
    par = pl.BlockSpec((hb * HEAD_DIM, bsz), lambda i: (i, 0))
    st = pl.BlockSpec((hb, HEAD_DIM, HEAD_DIM, bsz), lambda i: (i, 0, 0, 0))
    return pl.pallas_call(
        functools.partial(_wkv_decode_kernel, n_tok, hb),
        out_shape=[jax.ShapeDtypeStruct((n_tok * bsz, d), BF16), jax.ShapeDtypeStruct(state.shape, F32)],
        grid=(n_heads // hb,),
        in_specs=[blk] * 7 + [par] * 3 + [st],
        out_specs=[pl.BlockSpec((n_tok * bsz, hb * HEAD_DIM), lambda i: (0, i)), st],
        scratch_shapes=[pltpu.VMEM((n_tok, hb * HEAD_DIM, bsz), F32)] * 2,
        compiler_params=_cparams(1),
        name="wkv_decode",
    )(r, lw, k, v, a, b, g, rk, lnw, lnb, state)


def _ffn_kernel(block_mode, tiles_per_seq, n_f, final_norm, *refs):
    refs = list(refs)
    x_ref, o_ref, p_ref = refs[:3]
    pos = 3
    if block_mode:
        sg0, sv0, sg1, sv1 = refs[pos:pos + 4]
        pos += 4
    (wo_ref, gffn_ref, wug_ref, wuv_ref, cwg_ref, cwv_ref, cbg_ref, cbv_ref, wd_ref,
     wg_ref, wp_ref, gout_ref) = refs[pos:pos + 12]
    pos += 12
    y_o, ug_o, uv_o = refs[pos:pos + 3]
    pos += 3
    x1_s, hn_s, acc_s = refs[pos:pos + 3]
    pos += 3
    i = pl.program_id(0)
    f = pl.program_id(1)

    @pl.when(f == 0)
    def _():
        x1 = x_ref[...] + jnp.dot(o_ref[...], wo_ref[...], preferred_element_type=F32)
        x1_s[...] = x1
        hn_s[...] = _rms(x1, gffn_ref[...]).astype(BF16)
        acc_s[...] = jnp.zeros_like(acc_s)

    hn = hn_s[...]
    ug = jnp.dot(hn, wug_ref[...], preferred_element_type=F32)
    uv = jnp.dot(hn, wuv_ref[...], preferred_element_type=F32)
    tm = ug.shape[0]

    if block_mode:
        g1_s, g2_s, v1_s, v2_s = refs[pos:pos + 4]

        @pl.when(i == 0)
        def _():
            g2_s[f] = sg0[...]
            g1_s[f] = sg1[...]
            v2_s[f] = sv0[...]
            v1_s[f] = sv1[...]
        ug1, ug2, uv1, uv2 = g1_s[f], g2_s[f], v1_s[f], v2_s[f]
        g2_s[f] = ug1
        g1_s[f] = ug
        v2_s[f] = uv1
        v1_s[f] = uv
        ug_o[0] = ug
        uv_o[0] = uv
    else:
        cg_s, cv_s = refs[pos:pos + 2]

        @pl.when(i % tiles_per_seq == 0)
        def _():
            cg_s[f] = jnp.zeros(cg_s.shape[1:], F32)
            cv_s[f] = jnp.zeros(cv_s.shape[1:], F32)
        cg = cg_s[f]
        cv = cv_s[f]
        ug1, ug2 = _prev_rows_carry(ug, 1, cg), _prev_rows_carry(ug, 2, cg)
        uv1, uv2 = _prev_rows_carry(uv, 1, cv), _prev_rows_carry(uv, 2, cv)
        cg_s[f] = ug[tm - SUBLANES:tm, :]
        cv_s[f] = uv[tm - SUBLANES:tm, :]
        ug_o[0] = ug[tm - SUBLANES:tm, :]
        uv_o[0] = uv[tm - SUBLANES:tm, :]
    gate = cbg_ref[...] + cwg_ref[0:1, :] * ug2 + cwg_ref[1:2, :] * ug1 + cwg_ref[2:3, :] * ug
    val = cbv_ref[...] + cwv_ref[0:1, :] * uv2 + cwv_ref[1:2, :] * uv1 + cwv_ref[2:3, :] * uv
    act = gate * _sigmoid(gate) * val
    acc_s[...] += jnp.dot(act.astype(BF16), wd_ref[...], preferred_element_type=F32)

    @pl.when(f == n_f - 1)
    def _():
        x2 = x1_s[...] + acc_s[...]
        x3 = x2 + _sigmoid(_dot(x2, wg_ref[...])) * _dot(p_ref[...], wp_ref[...])
        if final_norm:
            x3 = _rms(x3, gout_ref[...])
        y_o[...] = x3


def _ffn(x, o, p, seq_len, conv_state, w, final_norm, tm, fc):
    n, d = x.shape
    dff = w['wd'].shape[0]
    pdim = p.shape[1]
    block_mode = conv_state is not None
    assert dff % fc == 0
    n_f = dff // fc
    if block_mode:
        tm = n // seq_len
        tiles_per_seq = 1
    else:
        tm = min(tm, seq_len)
        assert seq_len % tm == 0 and n % seq_len == 0
        tiles_per_seq = seq_len // tm
    rowd = pl.BlockSpec((tm, d), lambda i, f: (i, 0))
    rowp = pl.BlockSpec((tm, pdim), lambda i, f: (i, 0))
    in_specs = [rowd, rowd, rowp]
    args = [x, o, p]
    if block_mode:
        in_specs += [pl.BlockSpec((tm, fc), lambda i, f, q=q: (0, q * n_f + f)) for q in range(4)]
        args += [conv_state] * 4
    fcol = pl.BlockSpec((d, fc), lambda i, f: (0, f))
    fcol_v = pl.BlockSpec((d, fc), lambda i, f: (0, n_f + f))
    c3 = pl.BlockSpec((3, fc), lambda i, f: (0, f))
    c3_v = pl.BlockSpec((3, fc), lambda i, f: (0, n_f + f))
    c1 = pl.BlockSpec((1, fc), lambda i, f: (0, f))
    c1_v = pl.BlockSpec((1, fc), lambda i, f: (0, n_f + f))
    in_specs += [_const_spec(w['wo'].shape), _const_spec(w['gffn'].shape), fcol, fcol_v, c3, c3_v, c1, c1_v,
                 pl.BlockSpec((fc, d), lambda i, f: (f, 0)),
                 _const_spec(w['wg'].shape), _const_spec(w['wp'].shape), _const_spec(w['gout'].shape)]
    args += [w['wo'], w['gffn'], w['wup'], w['wup'], w['cw'], w['cw'], w['cb'], w['cb'], w['wd'],
             w['wg'], w['wp'], w['gout']]
    if block_mode:
        u_shape = jax.ShapeDtypeStruct((seq_len, tm, dff), F32)
        u_spec = pl.BlockSpec((1, tm, fc), lambda i, f: (i, 0, f))
        scratch_c = [pltpu.VMEM((n_f, tm, fc), F32)] * 4
    else:
        u_shape = jax.ShapeDtypeStruct((n // tm, SUBLANES, dff), F32)
        u_spec = pl.BlockSpec((1, SUBLANES, fc), lambda i, f: (i, 0, f))
        scratch_c = [pltpu.VMEM((n_f, SUBLANES, fc), F32)] * 2
    return pl.pallas_call(
        functools.partial(_ffn_kernel, block_mode, tiles_per_seq, n_f, final_norm),
        out_shape=[jax.ShapeDtypeStruct((n, d), F32), u_shape, u_shape],
        grid=(n // tm, n_f),
        in_specs=in_specs,
        out_specs=[rowd, u_spec, u_spec],
        scratch_shapes=[pltpu.VMEM((tm, d), F32), pltpu.VMEM((tm, d), BF16), pltpu.VMEM((tm, d), F32)] + scratch_c,
        compiler_params=_cparams(2),
        name="ffn",
    )(*args)


def _qkv_kernel(n_blk, prompt_mode, *refs):
    if prompt_mode:
        (x_ref, g_ref, w_ref, cos_ref, s1_ref, s2_ref, q_o, kt_o, vt_o, k16_o, vt16_o, ksum_o) = refs
    else:
        (x_ref, g_ref, w_ref, cos_ref, s1_ref, s2_ref, q_o, kt_o, vt_o, k_o, v_o) = refs
    hn = _rms(x_ref[...], g_ref[...]).astype(BF16)
    d = hn.shape[1]
    reps = d // LANES
    cos = jnp.concatenate([cos_ref[...]] * reps, axis=1)
    s1 = jnp.concatenate([s1_ref[...]] * reps, axis=1)
    s2 = jnp.concatenate([s2_ref[...]] * reps, axis=1)
    half = ROT_DIM // 2

    def rope(z):
        return z * cos + pltpu.roll(z, d - half, axis=1) * s1 + pltpu.roll(z, half, axis=1) * s2

    q = rope(jnp.dot(hn, w_ref[:, 0:d], preferred_element_type=F32)) * (HEAD_DIM ** -0.5)
    k = rope(jnp.dot(hn, w_ref[:, d:2 * d], preferred_element_type=F32))
    v = jnp.dot(hn, w_ref[:, 2 * d:3 * d], preferred_element_type=F32)
    vt = v.T
    q_o[...] = q
    kt_o[0] = k.T
    vt_o[0] = vt
    if prompt_mode:
        k16_o[...] = k.astype(BF16)
        vt16_o[0] = vt.astype(BF16)
        for j in range(n_blk):
            ksum_o[j] = jnp.sum(k[j * MOBA_BLOCK:(j + 1) * MOBA_BLOCK, :], axis=0, keepdims=True)
    else:
        k_o[...] = k
        v_o[...] = v


def _qkv(x, group, gmix, wqkv, cos, s1, s2, prompt_mode, tm):
    n, d = x.shape
    tm = min(tm, group)
    assert group % tm == 0 and n % group == 0
    tpg = group // tm
    row = pl.BlockSpec((tm, d), lambda i: (i, 0))
    rowt = pl.BlockSpec((tm, LANES), lambda i: (i, 0))
    tr = pl.BlockSpec((1, d, tm), lambda i: (i // tpg, 0, i % tpg))
    big = jax.ShapeDtypeStruct((n, d), F32)
    bigt = jax.ShapeDtypeStruct((n // group, d, group), F32)
    out_shape = [big, bigt, bigt]
    out_specs = [row, tr, tr]
    n_blk = 0
    if prompt_mode:
        assert tm % MOBA_BLOCK == 0
        n_blk = tm // MOBA_BLOCK
        out_shape += [jax.ShapeDtypeStruct((n, d), BF16), jax.ShapeDtypeStruct((n // group, d, group), BF16),
                      jax.ShapeDtypeStruct((n // MOBA_BLOCK, 1, d), F32)]
        out_specs += [row, tr, pl.BlockSpec((n_blk, 1, d), lambda i: (i, 0, 0))]
    else:
        out_shape += [big, big]
        out_specs += [row, row]
    return pl.pallas_call(
        functools.partial(_qkv_kernel, n_blk, prompt_mode),
        out_shape=out_shape,
        grid=(n // tm,),
        in_specs=[row, _const_spec(gmix.shape), _const_spec(wqkv.shape), rowt, rowt, rowt],
        out_specs=out_specs,
        compiler_params=_cparams(1),
        name="qkv",
    )(x, gmix, wqkv, cos, s1, s2)


def _moba_kernel(n_blocks, q_ref, k_ref, vt_ref, km_ref, o_ref, sel_s):
    qi = pl.program_id(2)
    blk = MOBA_BLOCK
    q = q_ref[0]
    lane = lax.broadcasted_iota(jnp.int32, (1, LANES), 1)
    km = km_ref[0]
    jrow = lax.broadcasted_iota(jnp.int32, (n_blocks, 1), 0)
    jrow_f = jrow.astype(F32)
    past = jrow < qi
    neg_inf = jnp.float32(-jnp.inf)
    qh16 = []
    for h in range(2):
        hm = (lane >= HEAD_DIM) if h else (lane < HEAD_DIM)
        qh = jnp.where(hm, q, 0.0)
        qh16.append(qh.astype(BF16))
        gate = lax.dot_general(km, qh, (((1,), (1,)), ((), ())), preferred_element_type=F32,
                               precision=lax.Precision.HIGHEST)
        cur = jnp.where(past, gate, neg_inf)
        sel = jnp.zeros(cur.shape, F32)
        for _ in range(MOBA_TOPK):
            mx = jnp.max(cur, axis=0, keepdims=True)
            first = jnp.min(jnp.where(cur == mx, jrow_f, float(n_blocks)), axis=0, keepdims=True)
            onehot = jrow_f == first
            pick = jnp.logical_and(onehot, mx > neg_inf)
            sel = jnp.where(pick, 1.0, sel)
            cur = jnp.where(onehot, neg_inf, cur)
        for j in range(n_blocks):
            sel_s[h * n_blocks + j] = jnp.broadcast_to(sel[j:j + 1, :], (SUBLANES, blk))

    def partial_attn(j, h, diag):
        k0 = pl.multiple_of(j * blk, blk)
        kj = k_ref[0, pl.ds(k0, blk), :]
        vtj = vt_ref[0, h * HEAD_DIM:(h + 1) * HEAD_DIM, pl.ds(k0, blk)]
        st = lax.dot_general(kj, qh16[h], (((1,), (1,)), ((), ())), preferred_element_type=F32)
        if diag:
            kr = lax.broadcasted_iota(jnp.int32, (blk, blk), 0)
            qc = lax.broadcasted_iota(jnp.int32, (blk, blk), 1)
            st = jnp.where(kr <= qc, st, NEG_BIG)
        mj = jnp.max(st, axis=0, keepdims=True)
        p = jnp.exp(st - mj)
        lj = jnp.sum(p, axis=0, keepdims=True)
        oj = jnp.dot(vtj, p.astype(BF16), preferred_element_type=F32)
        return mj, lj, oj

    def merge(state, part, selj):
        m, l, acc = state
        mj, lj, oj = part
        m_new = jnp.where(selj, jnp.maximum(m, mj), m)
        alpha = jnp.exp(m - m_new)
        beta = jnp.where(selj, jnp.exp(mj - m_new), 0.0)
        return m_new, l * alpha + lj * beta, acc * alpha + oj * beta

    def body(j, carry):
        out = []
        for h in range(2):
            selj = sel_s[h * n_blocks + j][0:1, :] > 0.5
            out.append(merge(carry[h], partial_attn(j, h, False), selj))
        return tuple(out)

    init = tuple((jnp.full((1, blk), NEG_BIG, F32), jnp.zeros((1, blk), F32),
                  jnp.zeros((HEAD_DIM, blk), F32)) for _ in range(2))
    carry = lax.fori_loop(0, qi, body, init)
    outs = []
    for h in range(2):
        m, l, acc = merge(carry[h], partial_attn(qi, h, True), jnp.full((1, blk), True))
        outs.append(acc / l)
    o_ref[0] = jnp.concatenate(outs, axis=0).T.astype(o_ref.dtype)


def _moba_prompt(q, k16, vt16, kmean):
    bsz, t, d = q.shape
    n_blocks = t // MOBA_BLOCK
    return pl.pallas_call(
        functools.partial(_moba_kernel, n_blocks),
        out_shape=jax.ShapeDtypeStruct((bsz, t, d), BF16),
        grid=(bsz, d // LANES, n_blocks),
        in_specs=[pl.BlockSpec((1, MOBA_BLOCK, LANES), lambda b, p, i: (b, i, p)),
                  pl.BlockSpec((1, t, LANES), lambda b, p, i: (b, 0, p)),
                  pl.BlockSpec((1, LANES, t), lambda b, p, i: (b, p, 0)),
                  pl.BlockSpec((1, n_blocks, LANES), lambda b, p, i: (b, 0, p))],
        out_specs=pl.BlockSpec((1, MOBA_BLOCK, LANES), lambda b, p, i: (b, i, p)),
        scratch_shapes=[pltpu.VMEM((2 * n_blocks, SUBLANES, MOBA_BLOCK), F32)],
        compiler_params=_cparams(3),
        name="moba_prompt",
    )(q, k16, vt16, kmean)


def _moba_decode_kernel(n_tok, n_heads, pages_per_step, n_steps, pt_ref, q_ref, kn_ref, vn_ref, *refs):
    kp = refs[:pages_per_step]
    vp = refs[pages_per_step:2 * pages_per_step]
    o_ref = refs[2 * pages_per_step]
    m_s, l_s, g_s, acc_s = refs[2 * pages_per_step + 1:]
    step = pl.program_id(1)
    d = q_ref.shape[2]
    page = kp[0].shape[2]
    pages_per_blk = MOBA_BLOCK // page
    blk_per_step = pages_per_step // pages_per_blk
    n_past = n_steps * blk_per_step
    nrow = n_tok * n_heads
    q = q_ref[0]
    lane_head = lax.broadcasted_iota(jnp.int32, (n_heads, d), 1) // HEAD_DIM
    row_head = lax.broadcasted_iota(jnp.int32, (n_heads, d), 0)
    hmask = lane_head == row_head
    qbd = jnp.concatenate(
        [jnp.where(hmask, jnp.broadcast_to(q[t:t + 1, :], (n_heads, d)), 0.0) for t in range(n_tok)], axis=0)
    qbd16 = qbd.astype(BF16)

    for jb in range(blk_per_step):
        kts = [kp[jb * pages_per_blk + i][0] for i in range(pages_per_blk)]
        vts = [vp[jb * pages_per_blk + i][0] for i in range(pages_per_blk)]
        s = jnp.concatenate([_dot(qbd16, kt) for kt in kts], axis=1)
        gate = jnp.sum(s, axis=1, keepdims=True) * (1.0 / MOBA_BLOCK)
        mj = jnp.max(s, axis=1, keepdims=True)
        p = jnp.exp(s - mj)
        lj = jnp.sum(p, axis=1, keepdims=True)
        oj = sum(_dot_nt(p[:, i * page:(i + 1) * page], vts[i]) for i in range(pages_per_blk))
        slot = step * blk_per_step + jb
        m_s[slot] = jnp.broadcast_to(mj, (nrow, LANES))
        l_s[slot] = jnp.broadcast_to(lj, (nrow, LANES))
        g_s[slot] = jnp.broadcast_to(gate, (nrow, LANES))
        acc_s[slot] = oj

    @pl.when(step == n_steps - 1)
    def _():
        gates = [g_s[j][:, 0:1] for j in range(n_past)]
        kn = kn_ref[0]
        vn = vn_ref[0]
        trow = lax.broadcasted_iota(jnp.int32, (nrow, 1), 0) // n_heads
        s_own = [jnp.where(trow >= t, jnp.sum(qbd * kn[t:t + 1, :], axis=1, keepdims=True), NEG_BIG)
                 for t in range(n_tok)]
        m = functools.reduce(jnp.maximum, s_own)
        p_own = [jnp.exp(s - m) for s in s_own]
        l = sum(p_own)
        acc = sum(p_own[t] * vn[t:t + 1, :] for t in range(n_tok))
        for j in range(n_past):
            rank = sum(jnp.where(jnp.logical_or(gates[i] > gates[j],
                                                jnp.logical_and(gates[i] == gates[j], i < j)), 1.0, 0.0)
                       for i in range(n_past) if i != j)
            selj = rank < float(MOBA_TOPK)
            mj = m_s[j][:, 0:1]
            m_new = jnp.where(selj, jnp.maximum(m, mj), m)
            alpha = jnp.exp(m - m_new)
            beta = jnp.where(selj, jnp.exp(mj - m_new), 0.0)
            l = l * alpha + l_s[j][:, 0:1] * beta
            acc = acc * alpha + acc_s[j] * beta
            m = m_new
        lane_head_r = lax.broadcasted_iota(jnp.int32, (nrow, d), 1) // HEAD_DIM
        row_head_r = lax.broadcasted_iota(jnp.int32, (nrow, d), 0) % n_heads
        out = jnp.where(lane_head_r == row_head_r, acc / l, 0.0)
        o_ref[0] = jnp.concatenate(
            [jnp.sum(out[t * n_heads:(t + 1) * n_heads], axis=0, keepdims=True) for t in range(n_tok)],
            axis=0).astype(o_ref.dtype)


def _moba_decode(q, kn, vn, cache_kt, cache_vt, page_table, pages_per_step):
    bsz, n_tok, d = q.shape
    n_heads = d // HEAD_DIM
    pool, _, page = cache_kt.shape
    n_pages = page_table.shape[1]
    assert n_pages % pages_per_step == 0 and MOBA_BLOCK % page == 0
    assert (n_pages * page) % MOBA_BLOCK == 0 and n_tok <= MOBA_BLOCK
    assert pages_per_step % (MOBA_BLOCK // page) == 0
    n_steps = n_pages // pages_per_step
    n_past = n_pages * page // MOBA_BLOCK
    nrow = n_tok * n_heads
    tok = pl.BlockSpec((1, n_tok, d), lambda b, s, pt: (b, 0, 0))

    def page_spec(i):
        return pl.BlockSpec((1, d, page), lambda b, s, pt: (pt[b, s * pages_per_step + i], 0, 0))

    grid_spec = pltpu.PrefetchScalarGridSpec(
        num_scalar_prefetch=1,
        grid=(bsz, n_steps),
        in_specs=[tok, tok, tok] + [page_spec(i) for i in range(pages_per_step)] * 2,
        out_specs=tok,
        scratch_shapes=[pltpu.VMEM((n_past, nrow, LANES), F32)] * 3 + [pltpu.VMEM((n_past, nrow, d), F32)],
    )
    return pl.pallas_call(
        functools.partial(_moba_decode_kernel, n_tok, n_heads, pages_per_step, n_steps),
        out_shape=jax.ShapeDtypeStruct((bsz, n_tok, d), BF16),
        grid_spec=grid_spec,
        compiler_params=_cparams(2),
        name="moba_decode",
    )(page_table, q, kn, vn, *([cache_kt] * pages_per_step), *([cache_vt] * pages_per_step))


def _pad_cols(w, n):
    return jnp.pad(w, ((0, 0), (0, n - w.shape[1])))


def _pad_rows(w, n):
    return jnp.pad(w, ((0, n - w.shape[0]), (0, 0)))


def _rope_tables(pos):
    half = ROT_DIM // 2
    inv = ROPE_THETA ** (-2.0 * jnp.arange(half, dtype=F32) / ROT_DIM)
    ang = pos.astype(F32)[:, None] * inv[None, :]
    cos = jnp.cos(ang)
    sin = jnp.sin(ang)
    t = pos.shape[0]
    ones = jnp.ones((t, HEAD_DIM - ROT_DIM), F32)
    zeros_r = jnp.zeros((t, HEAD_DIM - ROT_DIM), F32)
    zeros_h = jnp.zeros((t, half), F32)
    c = jnp.concatenate([cos, cos, ones], axis=1)
    s1 = jnp.concatenate([-sin, zeros_h, zeros_r], axis=1)
    s2 = jnp.concatenate([zeros_h, sin, zeros_r], axis=1)
    rep = LANES // HEAD_DIM
    return jnp.tile(c, (1, rep)), jnp.tile(s1, (1, rep)), jnp.tile(s2, (1, rep))


def _unpack_state(s):
    b, hp, dv, dk2 = s.shape
    return s.reshape(b, hp, dv, 2, dk2 // 2).transpose(0, 1, 3, 2, 4).reshape(b, hp * 2, dv, dk2 // 2)


def kernel(x_prompt, x_sample, state_wkv, state_shift, cache_k, cache_v, state_conv, page_table, p_prompt, p_sample, rw_mix, rw_rkv, rw_w0, rw_w1, rw_w2, rw_a0, rw_a1, rw_a2, rw_g1, rw_g2, rw_kk, rw_ka, rw_rk, rw_lnw, rw_lnb, rw_wo, mb_wqkv, mb_wo, norm_mix, norm_ffn, ff_wup, ff_conv_w, ff_conv_b, ff_wdown, ple_wp, ple_wg, norm_out):
    bp, tp, d = x_prompt.shape
    bs, ts, _ = x_sample.shape
    depth = norm_mix.shape[0]
    n_heads = d // HEAD_DIM
    dff = ff_wdown.shape[1]
    pool, page = cache_k.shape[1], cache_k.shape[2]
    past_len = page_table.shape[1] * page
    assert d % LANES == 0 and depth == 2 and ts >= 2

    lane_head = jnp.arange(d) // HEAD_DIM
    segr = (lane_head[:, None] == jnp.arange(LANES)[None, :]).astype(BF16)
    segb = segr.T
    seg2 = ((jnp.arange(LANES) // HEAD_DIM)[:, None] == (jnp.arange(LANES) // HEAD_DIM)[None, :]).astype(BF16)

    def row(v):
        return v.reshape(1, -1).astype(F32)

    def col(v):
        return jnp.broadcast_to(v.reshape(-1, 1).astype(F32), (v.size, bs))

    rw = dict(
        gmix=row(norm_mix[0]), mix=rw_mix[0], wrkv=rw_rkv[0].astype(BF16), w0=row(rw_w0[0]),
        w1=_pad_cols(rw_w1[0], LANES).astype(BF16), w2=_pad_rows(rw_w2[0], LANES).astype(BF16),
        a0=row(rw_a0[0]), a1=_pad_cols(rw_a1[0], LANES).astype(BF16), a2=_pad_rows(rw_a2[0], LANES).astype(BF16),
        g1=_pad_cols(rw_g1[0], 2 * LANES).astype(BF16), g2=_pad_rows(rw_g2[0], 2 * LANES).astype(BF16),
        kk=row(rw_kk[0]), ka=row(rw_ka[0]), segr=segr, segb=segb,
        rk=row(rw_rk[0]), lnw=row(rw_lnw[0]), lnb=row(rw_lnb[0]), seg2=seg2)

    def ffw(i, wo):
        return dict(wo=wo.astype(BF16), gffn=row(norm_ffn[i]), wup=ff_wup[i].astype(BF16), cw=ff_conv_w[i],
                    cb=row(ff_conv_b[i]), wd=ff_wdown[i].astype(BF16), wg=ple_wg[i].astype(BF16),
                    wp=ple_wp[i].astype(BF16), gout=row(norm_out))

    fw0 = ffw(0, rw_wo[0])
    fw1 = ffw(1, mb_wo[0])
    wqkv = mb_wqkv[0].astype(BF16)
    gmix1 = row(norm_mix[1])

    xp = x_prompt.reshape(bp * tp, d)
    pp = p_prompt.reshape(depth, bp * tp, -1)
    xs = x_sample.transpose(1, 0, 2).reshape(ts * bs, d)
    ps = p_sample.transpose(0, 2, 1, 3).reshape(depth, ts * bs, -1)

    def conv_state_p(ug, uv):
        last = lambda z: z.reshape(bp, -1, SUBLANES, dff)[:, -1, SUBLANES - 2:]
        return jnp.concatenate([last(ug), last(uv)], axis=-1)

    def conv_state_s(ug, uv):
        return jnp.concatenate([ug[ts - 2:], uv[ts - 2:]], axis=-1).transpose(1, 0, 2)

    r, lw, k, v, a, b, g, shift_p = _rwkv_proj(xp, tp, None, rw)
    sh = lambda z: z.reshape(bp, tp, d)
    o_p, wkv_p = _wkv(sh(r), sh(lw), sh(k), sh(v), sh(a), sh(b), sh(g), rw,
                      chunk=64, tblk=min(512, tp), bb=bp, npair=4)
    xp, ugp, uvp = _ffn(xp, o_p.reshape(bp * tp, d), pp[0], tp, None, fw0, False, 512, 256)
    conv_p0 = conv_state_p(ugp, uvp)

    r, lw, k, v, a, b, g, shift_s = _rwkv_proj(xs, ts, state_shift[0], rw)
    st_in = jnp.transpose(state_wkv[0], (1, 2, 3, 0))
    o_s, st_out = _wkv_decode(r, lw, k, v, a, b, g, col(rw_rk[0]), col(rw_lnw[0]), col(rw_lnb[0]), st_in, 2)
    wkv_s = jnp.transpose(st_out, (3, 0, 1, 2))
    xs, ugs, uvs = _ffn(xs, o_s, ps[0], ts, state_conv[0].reshape(bs, 4 * dff), fw0, False, 512, 256)
    conv_s0 = conv_state_s(ugs, uvs)

    cos, s1, s2 = _rope_tables(jnp.arange(tp, dtype=jnp.int32))
    cos_p, s1_p, s2_p = (jnp.tile(z, (bp, 1)) for z in (cos, s1, s2))
    q, kt, vt, k16, vt16, ksum = _qkv(xp, tp, gmix1, wqkv, cos_p, s1_p, s2_p, True, 256)
    k_p = kt.reshape(bp, n_heads, HEAD_DIM, tp).transpose(0, 3, 1, 2)
    v_p = vt.reshape(bp, n_heads, HEAD_DIM, tp).transpose(0, 3, 1, 2)
    kmean = ksum.reshape(bp, tp // MOBA_BLOCK, d) * (1.0 / MOBA_BLOCK)
    o_p = _moba_prompt(q.reshape(bp, tp, d), k16.reshape(bp, tp, d), vt16, kmean)
    xp, ugp, uvp = _ffn(xp, o_p.reshape(bp * tp, d), pp[1], tp, None, fw1, True, 512, 256)
    conv_p1 = conv_state_p(ugp, uvp)

    cos, s1, s2 = _rope_tables(past_len + jnp.arange(ts, dtype=jnp.int32))
    cos_s, s1_s, s2_s = (jnp.repeat(z, bs, axis=0) for z in (cos, s1, s2))
    q, kt, vt, k, v = _qkv(xs, bs, gmix1, wqkv, cos_s, s1_s, s2_s, False, 128)
    k_s = kt.reshape(ts, n_heads, HEAD_DIM, bs).transpose(3, 0, 1, 2)
    v_s = vt.reshape(ts, n_heads, HEAD_DIM, bs).transpose(3, 0, 1, 2)
    by_seq = lambda z: z.reshape(ts, bs, d).transpose(1, 0, 2)
    cache_kt = jnp.transpose(cache_k[0], (0, 2, 3, 1)).reshape(pool, d, page)
    cache_vt = jnp.transpose(cache_v[0], (0, 2, 3, 1)).reshape(pool, d, page)
    o_s = _moba_decode(by_seq(q), by_seq(k), by_seq(v), cache_kt, cache_vt, page_table, 4)
    o_s = o_s.transpose(1, 0, 2).reshape(ts * bs, d)
    xs, ugs, uvs = _ffn(xs, o_s, ps[1], ts, state_conv[1].reshape(bs, 4 * dff), fw1, True, 512, 256)
    conv_s1 = conv_state_s(ugs, uvs)

    return (xp.reshape(bp, tp, d), by_seq(xs),
            _unpack_state(wkv_p)[None], shift_p.reshape(1, bp, d), k_p[None], v_p[None],
            jnp.stack([conv_p0, conv_p1]),
            wkv_s[None], shift_s[None], k_s[None], v_s[None],
            jnp.stack([conv_s0, conv_s1]))
```

```python
import functools

import jax
import jax.numpy as jnp
from jax import lax
from jax.experimental import pallas as pl
from jax.experimental.pallas import tpu as pltpu

F32 = jnp.float32
BF16 = jnp.bfloat16

HEAD_DIM = 64
GN_EPS = 64e-5
RMS_EPS = 1e-6
MOBA_BLOCK = 256
MOBA_TOPK = 3
ROPE_THETA = 500000.0
ROT_DIM = HEAD_DIM // 4
NEG_BIG = -1e30
LOG2E = 1.4426950408889634
LANES = 128
SUBLANES = 8
VMEM_LIMIT_BYTES = 56 * 1024 * 1024


def _cparams(n_axes):
    return pltpu.CompilerParams(dimension_semantics=("arbitrary",) * n_axes,
                                vmem_limit_bytes=VMEM_LIMIT_BYTES)


def _const_spec(shape):
    nd = len(shape)
    return pl.BlockSpec(shape, lambda *_: (0,) * nd)


def _dot(a, b):
    return jnp.dot(a.astype(BF16), b.astype(BF16), preferred_element_type=F32)


def _dot_nt(a, b):
    return lax.dot_general(a.astype(BF16), b.astype(BF16), (((1,), (1,)), ((), ())),
                           preferred_element_type=F32)


def _dot_tn(a, b):
    return lax.dot_general(a.astype(BF16), b.astype(BF16), (((0,), (0,)), ((), ())),
                           preferred_element_type=F32)


def _split_dot(x, m):
    hi = x.astype(BF16)
    lo = (x - hi.astype(F32)).astype(BF16)
    return (jnp.dot(hi, m, preferred_element_type=F32) + jnp.dot(lo, m, preferred_element_type=F32))


def _sigmoid(z):
    return 1.0 / (1.0 + jnp.exp(-z))


def _rms(x, g):
    return x * lax.rsqrt(jnp.mean(x * x, axis=-1, keepdims=True) + RMS_EPS) * g


def _seg_sum(x, segr, segb):
    return _split_dot(_split_dot(x, segr), segb)


def _prev_rows_carry(u, k, carry):
    rolled = pltpu.roll(u, k, axis=0)
    row = lax.broadcasted_iota(jnp.int32, (u.shape[0], 1), 0)
    out = rolled
    for j in range(k):
        out = jnp.where(row == j, carry[SUBLANES - k + j:SUBLANES - k + j + 1, :], out)
    return out


def _rwkv_proj_kernel(block_mode, tiles_per_seq, *refs):
    if block_mode:
        (x_ref, init_ref, gmix_ref, mix_ref, wrkv_ref, w0_ref, w1_ref, w2_ref, a0_ref, a1_ref,
         a2_ref, g1_ref, g2_ref, kk_ref, ka_ref, segr_ref, segb_ref,
         r_o, lw_o, k_o, v_o, a_o, b_o, g_o, xn_o, carry) = refs
    else:
        (x_ref, gmix_ref, mix_ref, wrkv_ref, w0_ref, w1_ref, w2_ref, a0_ref, a1_ref, a2_ref,
         g1_ref, g2_ref, kk_ref, ka_ref, segr_ref, segb_ref,
         r_o, lw_o, k_o, v_o, a_o, b_o, g_o, xn_o, carry) = refs
    xn = _rms(x_ref[...], gmix_ref[...])
    tm = xn.shape[0]
    if block_mode:
        @pl.when(pl.program_id(0) == 0)
        def _():
            carry[...] = init_ref[...]
        xprev = carry[...]
        carry[...] = xn
        xn_o[...] = xn
    else:
        @pl.when(pl.program_id(0) % tiles_per_seq == 0)
        def _():
            carry[...] = jnp.zeros_like(carry)
        xprev = _prev_rows_carry(xn, 1, carry)
        carry[...] = xn[tm - SUBLANES:tm, :]
        xn_o[0] = xn[tm - 1:tm, :]
    dx = xprev - xn

    def xm(i):
        return (xn + dx * mix_ref[i:i + 1, :]).astype(BF16)

    r = jnp.dot(xm(0), wrkv_ref[0], preferred_element_type=F32)
    k = jnp.dot(xm(1), wrkv_ref[1], preferred_element_type=F32)
    v = jnp.dot(xm(2), wrkv_ref[2], preferred_element_type=F32)
    wl = w0_ref[...] + _dot(jnp.tanh(jnp.dot(xm(3), w1_ref[...], preferred_element_type=F32)), w2_ref[...])
    sp = jnp.maximum(-wl, 0.0) + jnp.log(1.0 + jnp.exp(-jnp.abs(wl)))
    lw = -jnp.exp(-sp - 0.5)
    a = _sigmoid(a0_ref[...] + _dot(jnp.dot(xm(4), a1_ref[...], preferred_element_type=F32), a2_ref[...]))
    g = _dot(_sigmoid(jnp.dot(xm(5), g1_ref[...], preferred_element_type=F32)), g2_ref[...])
    kk = k * kk_ref[...]
    n2 = _seg_sum(kk * kk, segr_ref[...], segb_ref[...])
    kk = kk / jnp.maximum(jnp.sqrt(n2), 1e-12)
    outs = (r, lw, k * (1.0 + (a - 1.0) * ka_ref[...]), v, -kk, kk * a, g)
    for o_ref, val in zip((r_o, lw_o, k_o, v_o, a_o, b_o, g_o), outs):
        if block_mode:
            o_ref[0] = val.T
        else:
            o_ref[...] = val


def _rwkv_proj(x, seq_len, init, w):
    n, d = x.shape
    block_mode = init is not None
    consts = [w['gmix'], w['mix'], w['wrkv'], w['w0'], w['w1'], w['w2'], w['a0'], w['a1'], w['a2'],
              w['g1'], w['g2'], w['kk'], w['ka'], w['segr'], w['segb']]
    if block_mode:
        tm = n // seq_len
        tiles_per_seq = 1
        row = pl.BlockSpec((tm, d), lambda i: (i, 0))
        in_specs = [row, _const_spec(init.shape)] + [_const_spec(c.shape) for c in consts]
        args = [x, init] + consts
        out_shape = [jax.ShapeDtypeStruct((seq_len, d, tm), F32)] * 7 + [jax.ShapeDtypeStruct((tm, d), F32)]
        out_specs = [pl.BlockSpec((1, d, tm), lambda i: (i, 0, 0))] * 7 + [_const_spec((tm, d))]
        scratch = [pltpu.VMEM((tm, d), F32)]
    else:
        tm = min(256, seq_len)
        assert seq_len % tm == 0 and n % seq_len == 0
        tiles_per_seq = seq_len // tm
        row = pl.BlockSpec((tm, d), lambda i: (i, 0))
        in_specs = [row] + [_const_spec(c.shape) for c in consts]
        args = [x] + consts
        out_shape = [jax.ShapeDtypeStruct((n, d), F32)] * 7 + [jax.ShapeDtypeStruct((n // seq_len, 1, d), F32)]
        out_specs = [row] * 7 + [pl.BlockSpec((1, 1, d), lambda i: (i // tiles_per_seq, 0, 0))]
        scratch = [pltpu.VMEM((SUBLANES, d), F32)]
    return pl.pallas_call(
        functools.partial(_rwkv_proj_kernel, block_mode, tiles_per_seq),
        out_shape=out_shape,
        grid=(n // tm,),
        in_specs=in_specs,
        out_specs=out_specs,
        scratch_shapes=scratch,
        compiler_params=_cparams(1),
        name="rwkv_proj",
    )(*args)


def _wkv_masks(c):
    n = 2 * c
    row = lax.broadcasted_iota(jnp.int32, (n, n), 0)
    col = lax.broadcasted_iota(jnp.int32, (n, n), 1)
    rq = row >= c
    cq = col >= c
    tl = jnp.logical_and(jnp.logical_not(rq), jnp.logical_not(cq))
    br = jnp.logical_and(rq, cq)
    tr = jnp.logical_and(jnp.logical_not(rq), cq)
    bl = jnp.logical_and(rq, jnp.logical_not(cq))
    strict = (col % c) < (row % c)
    incl = (col % c) <= (row % c)
    one = jnp.ones((n, n), F32)
    zero = jnp.zeros((n, n), F32)

    def f(m):
        return jnp.where(m, one, zero)

    return dict(
        ab0=f(jnp.logical_and(strict, tl)), ab1=f(jnp.logical_and(strict, br)),
        ak0=f(jnp.logical_and(strict, tr)), ak1=f(jnp.logical_and(strict, bl)),
        rb0=f(jnp.logical_and(incl, bl)), rb1=f(jnp.logical_and(incl, tr)),
        rk0=f(jnp.logical_and(incl, br)), rk1=f(jnp.logical_and(incl, tl)),
        eye=f(row == col),
    )


def _wkv_chunk(c, n_rounds, units, tri, msk, m0, m1, bdmask):
    nu = range(len(units))
    pre = []
    for r, lw, k, v, a, b, state in units:
        g_inc = jnp.dot(tri, lw, preferred_element_type=F32, precision=lax.Precision.HIGHEST)
        e_inc = jnp.exp(g_inc)
        e_neg = jnp.exp(-g_inc)
        at = a * jnp.exp(g_inc - lw)
        rt = r * e_inc
        bt = b * e_neg
        kt = k * e_neg
        pre.append(dict(
            lr=jnp.concatenate([at, rt], axis=0), rl=jnp.concatenate([rt, at], axis=0),
            bk=jnp.concatenate([bt, kt], axis=0), kb=jnp.concatenate([kt, bt], axis=0),
            vst=jnp.concatenate([v * m1, v * m0], axis=0), v=v, state=state, gc=e_inc[c - 1:c, :]))
    sc0 = [_dot_nt(p['lr'] * m0, p['bk']) for p in pre]
    sc1 = [_dot_nt(p['rl'] * m1, p['kb']) for p in pre]
    xs = [_dot_nt(p['lr'], p['state']) for p in pre]
    ab = [sc0[i] * msk['ab0'] + sc1[i] * msk['ab1'] for i in nu]
    ak = [sc0[i] * msk['ak0'] + sc1[i] * msk['ak1'] for i in nu]
    rbk = [jnp.concatenate([sc0[i] * msk['rb0'] + sc1[i] * msk['rb1'],
                            sc0[i] * msk['rk0'] + sc1[i] * msk['rk1']], axis=1) for i in nu]
    akv = [_dot(ak[i], pre[i]['vst']) for i in nu]
    tinv = [msk['eye'] + ab[i] for i in nu]
    pw = ab
    for _ in range(n_rounds):
        pw = [_dot(pw[i], pw[i]) for i in nu]
        tinv = [tinv[i] + _dot(tinv[i], pw[i]) for i in nu]
    zst = [jnp.concatenate([xs[i][:c] * m0, xs[i][:c] * m1], axis=0) + akv[i] for i in nu]
    w = [_dot(tinv[i], zst[i]) for i in nu]
    ys = [_dot(rbk[i], jnp.concatenate([w[i], pre[i]['vst']], axis=0)) for i in nu]
    ds = [_dot_tn(jnp.concatenate([w[i][:c] + w[i][c:], pre[i]['v']], axis=0), pre[i]['bk']) for i in nu]
    outs = []
    for i in nu:
        y = xs[i][c:] + ys[i][:c] + ys[i][c:]
        outs.append((y, (pre[i]['state'] + ds[i] * bdmask) * pre[i]['gc']))
    return outs


def _wkv_kernel(c, n_chunks, bb, npair, r_ref, lw_ref, k_ref, v_ref, a_ref, b_ref, g_ref, rk_ref, lnw_ref,
                lnb_ref, seg_ref, o_ref, s_out, state, ybuf):
    tstep = pl.program_id(2)
    lane = lax.broadcasted_iota(jnp.int32, (1, LANES), 1)
    m0 = jnp.where(lane < HEAD_DIM, 1.0, 0.0).astype(F32)
    m1 = 1.0 - m0
    rowi = lax.broadcasted_iota(jnp.int32, (LANES, LANES), 0)
    coli = lax.broadcasted_iota(jnp.int32, (LANES, LANES), 1)
    bdmask = jnp.where((rowi >= HEAD_DIM) == (coli >= HEAD_DIM), 1.0, 0.0).astype(F32)
    half = LANES // 2

    @pl.when(tstep == 0)
    def _():
        state[...] = jnp.zeros_like(state)

    ti = lax.broadcasted_iota(jnp.int32, (c, c), 0)
    tj = lax.broadcasted_iota(jnp.int32, (c, c), 1)
    tri = jnp.where(tj <= ti, 1.0, 0.0).astype(F32)
    msk = _wkv_masks(c)
    n_rounds = max(c.bit_length() - 2, 0)

    def chunk_body(ci, carry):
        t0 = pl.multiple_of(ci * c, c)
        rows = pl.ds(t0, c)
        units = [(ib, ip) for ib in range(bb) for ip in range(npair)]
        ins = []
        for ib, ip in units:
            ls = slice(ip * LANES, (ip + 1) * LANES)
            ins.append((r_ref[ib, rows, ls], lw_ref[ib, rows, ls], k_ref[ib, rows, ls],
                        v_ref[ib, rows, ls], a_ref[ib, rows, ls], b_ref[ib, rows, ls],
                        state[ib * npair + ip]))
        outs = _wkv_chunk(c, n_rounds, ins, tri, msk, m0, m1, bdmask)
        for (ib, ip), (y, ns) in zip(units, outs):
            state[ib * npair + ip] = ns
            ybuf[ib, rows, slice(ip * LANES, (ip + 1) * LANES)] = y
        return carry

    lax.fori_loop(0, n_chunks, chunk_body, 0)

    seg = seg_ref[...]
    for ib in range(bb):
        for ip in range(npair):
            ls = slice(ip * LANES, (ip + 1) * LANES)
            y = ybuf[ib, :, ls]
            mu = _split_dot(y, seg) * (1.0 / HEAD_DIM)
            yc = y - mu
            var = _split_dot(yc * yc, seg) * (1.0 / HEAD_DIM)
            yn = yc * lax.rsqrt(var + GN_EPS) * lnw_ref[:, ls] + lnb_ref[:, ls]
            rr = r_ref[ib, :, ls]
            bonus = _split_dot(rr * k_ref[ib, :, ls] * rk_ref[:, ls], seg) * v_ref[ib, :, ls]
            o_ref[ib, :, ls] = ((yn + bonus) * g_ref[ib, :, ls]).astype(o_ref.dtype)

    @pl.when(tstep == pl.num_programs(2) - 1)
    def _():
        for ib in range(bb):
            for ip in range(npair):
                st = state[ib * npair + ip]
                s_out[ib, ip] = st[:half] + st[half:]


def _wkv(r, lw, k, v, a, b, g, w, chunk, tblk, bb, npair):
    bsz, t, d = r.shape
    np_total = d // LANES
    assert bsz % bb == 0 and np_total % npair == 0 and t % tblk == 0 and tblk % chunk == 0
    blk = pl.BlockSpec((bb, tblk, npair * LANES), lambda ib, ip, it: (ib, it, ip))
    par = pl.BlockSpec((1, npair * LANES), lambda ib, ip, it: (0, ip))
    st_spec = pl.BlockSpec((bb, npair, HEAD_DIM, LANES), lambda ib, ip, it: (ib, ip, 0, 0))
    return pl.pallas_call(
        functools.partial(_wkv_kernel, chunk, tblk // chunk, bb, npair),
        out_shape=[jax.ShapeDtypeStruct((bsz, t, d), BF16),
                   jax.ShapeDtypeStruct((bsz, np_total, HEAD_DIM, LANES), F32)],
        grid=(bsz // bb, np_total // npair, t // tblk),
        in_specs=[blk] * 7 + [par] * 3 + [_const_spec(w['seg2'].shape)],
        out_specs=[blk, st_spec],
        scratch_shapes=[pltpu.VMEM((bb * npair, LANES, LANES), F32),
                        pltpu.VMEM((bb, tblk, npair * LANES), F32)],
        compiler_params=_cparams(3),
        name="wkv",
    )(r, lw, k, v, a, b, g, w['rk'], w['lnw'], w['lnb'], w['seg2'])


def _wkv_decode_kernel(n_tok, hb, r_ref, lw_ref, k_ref, v_ref, a_ref, b_ref, g_ref, rk_ref, lnw_ref, lnb_ref,
                       s_in, o_ref, s_out, w_s, y_s):
    bsz = r_ref.shape[2]
    w_s[...] = jnp.exp(lw_ref[...])
    for h in range(hb):
        rows = slice(h * HEAD_DIM, (h + 1) * HEAD_DIM)

        def body(i, carry, h=h, rows=rows):
            s = s_in[h, i]
            for t in range(n_tok):
                sa = jnp.sum(s * a_ref[t, rows, :], axis=0, keepdims=True)
                vi = v_ref[t, pl.ds(h * HEAD_DIM + i, 1), :]
                s = s * w_s[t, rows, :] + sa * b_ref[t, rows, :] + vi * k_ref[t, rows, :]
                y_s[t, pl.ds(h * HEAD_DIM + i, 1), :] = jnp.sum(s * r_ref[t, rows, :], axis=0, keepdims=True)
            s_out[h, i] = s
            return carry

        lax.fori_loop(0, HEAD_DIM, body, 0)

    for t in range(n_tok):
        outs = []
        for h in range(hb):
            rows = slice(h * HEAD_DIM, (h + 1) * HEAD_DIM)
            y = y_s[t, rows, :]
            mu = jnp.mean(y, axis=0, keepdims=True)
            yc = y - mu
            var = jnp.mean(yc * yc, axis=0, keepdims=True)
            yn = yc * lax.rsqrt(var + GN_EPS) * lnw_ref[rows, :] + lnb_ref[rows, :]
            bonus = jnp.sum(r_ref[t, rows, :] * k_ref[t, rows, :] * rk_ref[rows, :], axis=0,
                            keepdims=True) * v_ref[t, rows, :]
            outs.append((yn + bonus) * g_ref[t, rows, :])
        o_ref[t * bsz:(t + 1) * bsz, :] = jnp.concatenate(outs, axis=0).T.astype(o_ref.dtype)


def _wkv_decode(r, lw, k, v, a, b, g, rk, lnw, lnb, state, hb):
    n_tok, d, bsz = r.shape
    n_heads = d // HEAD_DIM
    assert n_heads % hb == 0
    blk = pl.BlockSpec((n_tok, hb * HEAD_DIM, bsz), lambda i: (0, i, 0))
    par = pl.BlockSpec((hb * HEAD_DIM, bsz), lambda i: (i, 0))
    st = pl.BlockSpec((hb, HEAD_DIM, HEAD_DIM, bsz), lambda i: (i, 0, 0, 0))
    return pl.pallas_call(
        functools.partial(_wkv_decode_kernel, n_tok, hb),
        out_shape=[jax.ShapeDtypeStruct((n_tok * bsz, d), BF16), jax.ShapeDtypeStruct(state.shape, F32)],
        grid=(n_heads // hb,),
        in_specs=[blk] * 7 + [par] * 3 + [st],
        out_specs=[pl.BlockSpec((n_tok * bsz, hb * HEAD_DIM), lambda i: (0, i)), st],
        scratch_shapes=[pltpu.VMEM((n_tok, hb * HEAD_DIM, bsz), F32)] * 2,
        compiler_params=_cparams(1),
        name="wkv_decode",
    )(r, lw, k, v, a, b, g, rk, lnw, lnb, state)


def _ffn_kernel(block_mode, tiles_per_seq, n_f, final_norm, *refs):
    refs = list(refs)
    x_ref, o_ref, p_ref = refs[:3]
    pos = 3
    if block_mode:
        sg0, sv0, sg1, sv1 = refs[pos:pos + 4]
        pos += 4
    (wo_ref, gffn_ref, wug_ref, wuv_ref, cwg_ref, cwv_ref, cbg_ref, cbv_ref, wd_ref,
     wg_ref, wp_ref, gout_ref) = refs[pos:pos + 12]
    pos += 12
    y_o, ug_o, uv_o = refs[pos:pos + 3]
    pos += 3
    x1_s, hn_s, acc_s = refs[pos:pos + 3]
    pos += 3
    i = pl.program_id(0)
    f = pl.program_id(1)

    @pl.when(f == 0)
    def _():
        x1 = x_ref[...] + jnp.dot(o_ref[...], wo_ref[...], preferred_element_type=F32)
        x1_s[...] = x1
        hn_s[...] = _rms(x1, gffn_ref[...]).astype(BF16)
        acc_s[...] = jnp.zeros_like(acc_s)

    hn = hn_s[...]
    ug = jnp.dot(hn, wug_ref[...], preferred_element_type=F32)
    uv = jnp.dot(hn, wuv_ref[...], preferred_element_type=F32)
    tm = ug.shape[0]

    if block_mode:
        g1_s, g2_s, v1_s, v2_s = refs[pos:pos + 4]

        @pl.when(i == 0)
        def _():
            g2_s[f] = sg0[...]
            g1_s[f] = sg1[...]
            v2_s[f] = sv0[...]
            v1_s[f] = sv1[...]
        ug1, ug2, uv1, uv2 = g1_s[f], g2_s[f], v1_s[f], v2_s[f]
        g2_s[f] = ug1
        g1_s[f] = ug
        v2_s[f] = uv1
        v1_s[f] = uv
        ug_o[0] = ug
        uv_o[0] = uv
    else:
        cg_s, cv_s = refs[pos:pos + 2]

        @pl.when(i % tiles_per_seq == 0)
        def _():
            cg_s[f] = jnp.zeros(cg_s.shape[1:], F32)
            cv_s[f] = jnp.zeros(cv_s.shape[1:], F32)
        cg = cg_s[f]
        cv = cv_s[f]
        ug1, ug2 = _prev_rows_carry(ug, 1, cg), _prev_rows_carry(ug, 2, cg)
        uv1, uv2 = _prev_rows_carry(uv, 1, cv), _prev_rows_carry(uv, 2, cv)
        cg_s[f] = ug[tm - SUBLANES:tm, :]
        cv_s[f] = uv[tm - SUBLANES:tm, :]
        ug_o[0] = ug[tm - SUBLANES:tm, :]
        uv_o[0] = uv[tm - SUBLANES:tm, :]
    gate = cbg_ref[...] + cwg_ref[0:1, :] * ug2 + cwg_ref[1:2, :] * ug1 + cwg_ref[2:3, :] * ug
    val = cbv_ref[...] + cwv_ref[0:1, :] * uv2 + cwv_ref[1:2, :] * uv1 + cwv_ref[2:3, :] * uv
    act = gate * _sigmoid(gate) * val
    acc_s[...] += jnp.dot(act.astype(BF16), wd_ref[...], preferred_element_type=F32)

    @pl.when(f == n_f - 1)
    def _():
        x2 = x1_s[...] + acc_s[...]
        x3 = x2 + _sigmoid(_dot(x2, wg_ref[...])) * _dot(p_ref[...], wp_ref[...])
        if final_norm:
            x3 = _rms(x3, gout_ref[...])
        y_o[...] = x3


def _ffn(x, o, p, seq_len, conv_state, w, final_norm, tm, fc):
    n, d = x.shape
    dff = w['wd'].shape[0]
    pdim = p.shape[1]
    block_mode = conv_state is not None
    assert dff % fc == 0
    n_f = dff // fc
    if block_mode:
        tm = n // seq_len
        tiles_per_seq = 1
    else:
        tm = min(tm, seq_len)
        assert seq_len % tm == 0 and n % seq_len == 0
        tiles_per_seq = seq_len // tm
    rowd = pl.BlockSpec((tm, d), lambda i, f: (i, 0))
    rowp = pl.BlockSpec((tm, pdim), lambda i, f: (i, 0))
    in_specs = [rowd, rowd, rowp]
    args = [x, o, p]
    if block_mode:
        in_specs += [pl.BlockSpec((tm, fc), lambda i, f, q=q: (0, q * n_f + f)) for q in range(4)]
        args += [conv_state] * 4
    fcol = pl.BlockSpec((d, fc), lambda i, f: (0, f))
    fcol_v = pl.BlockSpec((d, fc), lambda i, f: (0, n_f + f))
    c3 = pl.BlockSpec((3, fc), lambda i, f: (0, f))
    c3_v = pl.BlockSpec((3, fc), lambda i, f: (0, n_f + f))
    c1 = pl.BlockSpec((1, fc), lambda i, f: (0, f))
    c1_v = pl.BlockSpec((1, fc), lambda i, f: (0, n_f + f))
    in_specs += [_const_spec(w['wo'].shape), _const_spec(w['gffn'].shape), fcol, fcol_v, c3, c3_v, c1, c1_v,
                 pl.BlockSpec((fc, d), lambda i, f: (f, 0)),
                 _const_spec(w['wg'].shape), _const_spec(w['wp'].shape), _const_spec(w['gout'].shape)]
    args += [w['wo'], w['gffn'], w['wup'], w['wup'], w['cw'], w['cw'], w['cb'], w['cb'], w['wd'],
             w['wg'], w['wp'], w['gout']]
    if block_mode:
        u_shape = jax.ShapeDtypeStruct((seq_len, tm, dff), F32)
        u_spec = pl.BlockSpec((1, tm, fc), lambda i, f: (i, 0, f))
        scratch_c = [pltpu.VMEM((n_f, tm, fc), F32)] * 4
    else:
        u_shape = jax.ShapeDtypeStruct((n // tm, SUBLANES, dff), F32)
        u_spec = pl.BlockSpec((1, SUBLANES, fc), lambda i, f: (i, 0, f))
        scratch_c = [pltpu.VMEM((n_f, SUBLANES, fc), F32)] * 2
    return pl.pallas_call(
        functools.partial(_ffn_kernel, block_mode, tiles_per_seq, n_f, final_norm),
        out_shape=[jax.ShapeDtypeStruct((n, d), F32), u_shape, u_shape],
        grid=(n // tm, n_f),
        in_specs=in_specs,
        out_specs=[rowd, u_spec, u_spec],
        scratch_shapes=[pltpu.VMEM((tm, d), F32), pltpu.VMEM((tm, d), BF16), pltpu.VMEM((tm, d), F32)] + scratch_c,
        compiler_params=_cparams(2),
        name="ffn",
    )(*args)


def _qkv_kernel(prompt_mode, *refs):
    if prompt_mode:
        (x_ref, g_ref, w_ref, cos_ref, s1_ref, s2_ref, q_o, kt_o, vt_o, k16_o, vt16_o) = refs
    else:
        (x_ref, g_ref, w_ref, cos_ref, s1_ref, s2_ref, q_o, kt_o, vt_o, k_o, v_o) = refs
    hn = _rms(x_ref[...], g_ref[...]).astype(BF16)
    d = hn.shape[1]
    reps = d // LANES
    cos = jnp.concatenate([cos_ref[...]] * reps, axis=1)
    s1 = jnp.concatenate([s1_ref[...]] * reps, axis=1)
    s2 = jnp.concatenate([s2_ref[...]] * reps, axis=1)
    half = ROT_DIM // 2

    def rope(z):
        return z * cos + pltpu.roll(z, d - half, axis=1) * s1 + pltpu.roll(z, half, axis=1) * s2

    q = rope(jnp.dot(hn, w_ref[:, 0:d], preferred_element_type=F32)) * (HEAD_DIM ** -0.5)
    k = rope(jnp.dot(hn, w_ref[:, d:2 * d], preferred_element_type=F32))
    v = jnp.dot(hn, w_ref[:, 2 * d:3 * d], preferred_element_type=F32)
    vt = v.T
    kt_o[0] = k.T
    vt_o[0] = vt
    if prompt_mode:
        q_o[...] = (q * LOG2E).astype(BF16)
        k16_o[...] = k.astype(BF16)
        vt16_o[0] = vt.astype(BF16)
    else:
        q_o[...] = q
        k_o[...] = k
        v_o[...] = v


def _qkv(x, group, gmix, wqkv, cos, s1, s2, prompt_mode, tm):
    n, d = x.shape
    tm = min(tm, group)
    assert group % tm == 0 and n % group == 0
    tpg = group // tm
    row = pl.BlockSpec((tm, d), lambda i: (i, 0))
    rowt = pl.BlockSpec((tm, LANES), lambda i: (i, 0))
    tr = pl.BlockSpec((1, d, tm), lambda i: (i // tpg, 0, i % tpg))
    big = jax.ShapeDtypeStruct((n, d), F32)
    big16 = jax.ShapeDtypeStruct((n, d), BF16)
    bigt = jax.ShapeDtypeStruct((n // group, d, group), F32)
    if prompt_mode:
        out_shape = [big16, bigt, bigt, big16, jax.ShapeDtypeStruct((n // group, d, group), BF16)]
        out_specs = [row, tr, tr, row, tr]
    else:
        out_shape = [big, bigt, bigt, big, big]
        out_specs = [row, tr, tr, row, row]
    return pl.pallas_call(
        functools.partial(_qkv_kernel, prompt_mode),
        out_shape=out_shape,
        grid=(n // tm,),
        in_specs=[row, _const_spec(gmix.shape), _const_spec(wqkv.shape), rowt, rowt, rowt],
        out_specs=out_specs,
        compiler_params=_cparams(1),
        name="qkv",
    )(x, gmix, wqkv, cos, s1, s2)


def _moba_kernel(n_blocks, grp, q_ref, k_ref, vt_ref, o_ref, o_s, m_s, l_s, g_s, w_s):
    qi = pl.program_id(2)
    blk = MOBA_BLOCK
    q = q_ref[0]
    lane = lax.broadcasted_iota(jnp.int32, (1, LANES), 1)
    zero16 = jnp.zeros_like(q)
    qh = [jnp.where(lane < HEAD_DIM, q, zero16), jnp.where(lane >= HEAD_DIM, q, zero16)]
    ones_rows = jnp.where(lax.broadcasted_iota(jnp.int32, (2 * SUBLANES, blk), 0) == 0, 1.0, 0.0).astype(BF16)
    neg_inf = jnp.float32(-jnp.inf)
    for h in range(2):
        m_s[h] = jnp.full((n_blocks, blk), NEG_BIG, F32)
        l_s[h] = jnp.zeros((n_blocks, blk), F32)
        g_s[h] = jnp.full((n_blocks, blk), neg_inf, F32)

    def score_blocks(j0, nb, diag):
        k0 = pl.multiple_of(j0 * blk, blk)
        kg = k_ref[0, pl.ds(k0, nb * blk), :]
        st = [lax.dot_general(kg, qh[h], (((1,), (1,)), ((), ())), preferred_element_type=F32)
              for h in range(2)]
        parts = []
        for h in range(2):
            for jb in range(nb):
                s = st[h][jb * blk:(jb + 1) * blk]
                if diag:
                    kr = lax.broadcasted_iota(jnp.int32, (blk, blk), 0)
                    qc = lax.broadcasted_iota(jnp.int32, (blk, blk), 1)
                    s = jnp.where(kr <= qc, s, NEG_BIG)
                gate = jnp.sum(s, axis=0, keepdims=True) * (1.0 / blk)
                mj = jnp.max(s, axis=0, keepdims=True)
                p = jnp.exp2(s - mj).astype(BF16)
                parts.append((h, jb, gate, mj, p))
        pvs = []
        for h, jb, gate, mj, p in parts:
            kb0 = pl.multiple_of((j0 + jb) * blk, blk)
            vt_ext = jnp.concatenate([vt_ref[0, h * HEAD_DIM:(h + 1) * HEAD_DIM, pl.ds(kb0, blk)], ones_rows],
                                     axis=0)
            pvs.append(jnp.dot(vt_ext, p, preferred_element_type=F32))
        for (h, jb, gate, mj, p), pv in zip(parts, pvs):
            j = j0 + jb
            o_s[h * n_blocks + j] = pv[:HEAD_DIM]
            l_s[h, pl.ds(j, 1), :] = pv[HEAD_DIM:HEAD_DIM + 1]
            m_s[h, pl.ds(j, 1), :] = mj
            g_s[h, pl.ds(j, 1), :] = gate

    def group_body(gi, carry):
        score_blocks(gi * grp, grp, False)
        return carry

    lax.fori_loop(0, (qi + grp - 1) // grp, group_body, 0)
    score_blocks(qi, 1, True)

    jrow = lax.broadcasted_iota(jnp.int32, (n_blocks, 1), 0)
    jrow_f = jrow.astype(F32)
    outs = []
    for h in range(2):
        cur = jnp.where(jrow < qi, g_s[h], neg_inf)
        sel = jrow == qi
        for _ in range(MOBA_TOPK):
            mx = jnp.max(cur, axis=0, keepdims=True)
            first = jnp.min(jnp.where(cur == mx, jrow_f, float(n_blocks)), axis=0, keepdims=True)
            onehot = jrow_f == first
            sel = jnp.logical_or(sel, jnp.logical_and(onehot, mx > neg_inf))
            cur = jnp.where(onehot, neg_inf, cur)
        m = m_s[h]
        mx = jnp.max(jnp.where(sel, m, NEG_BIG), axis=0, keepdims=True)
        w = jnp.where(sel, jnp.exp2(m - mx), 0.0)
        w_s[h] = w
        lsum = jnp.sum(w * l_s[h], axis=0, keepdims=True)

        def merge(j, acc, h=h):
            return acc + w_s[h, pl.ds(j, 1), :] * o_s[h * n_blocks + j]

        acc = lax.fori_loop(0, qi + 1, merge, jnp.zeros((HEAD_DIM, blk), F32))
        outs.append(acc / lsum)
    o_ref[0] = jnp.concatenate(outs, axis=0).T.astype(o_ref.dtype)


def _moba_prompt(q16, k16, vt16, grp):
    bsz, t, d = q16.shape
    n_blocks = t // MOBA_BLOCK
    assert n_blocks % grp == 0
    stat = pltpu.VMEM((2, n_blocks, MOBA_BLOCK), F32)
    return pl.pallas_call(
        functools.partial(_moba_kernel, n_blocks, grp),
        out_shape=jax.ShapeDtypeStruct((bsz, t, d), BF16),
        grid=(bsz, d // LANES, n_blocks),
        in_specs=[pl.BlockSpec((1, MOBA_BLOCK, LANES), lambda b, p, i: (b, i, p)),
                  pl.BlockSpec((1, t, LANES), lambda b, p, i: (b, 0, p)),
                  pl.BlockSpec((1, LANES, t), lambda b, p, i: (b, p, 0))],
        out_specs=pl.BlockSpec((1, MOBA_BLOCK, LANES), lambda b, p, i: (b, i, p)),
        scratch_shapes=[pltpu.VMEM((2 * n_blocks, HEAD_DIM, MOBA_BLOCK), F32), stat, stat, stat, stat],
        compiler_params=_cparams(3),
        name="moba_prompt",
    )(q16, k16, vt16)


def _moba_decode_kernel(n_tok, n_heads, pages_per_step, n_steps, pt_ref, q_ref, kn_ref, vn_ref, *refs):
    kp = refs[:pages_per_step]
    vp = refs[pages_per_step:2 * pages_per_step]
    o_ref = refs[2 * pages_per_step]
    m_s, l_s, g_s, acc_s = refs[2 * pages_per_step + 1:]
    step = pl.program_id(1)
    d = q_ref.shape[2]
    page = kp[0].shape[2]
    pages_per_blk = MOBA_BLOCK // page
    blk_per_step = pages_per_step // pages_per_blk
    n_past = n_steps * blk_per_step
    nrow = n_tok * n_heads
    q = q_ref[0]
    lane_head = lax.broadcasted_iota(jnp.int32, (n_heads, d), 1) // HEAD_DIM
    row_head = lax.broadcasted_iota(jnp.int32, (n_heads, d), 0)
    hmask = lane_head == row_head
    qbd = jnp.concatenate(
        [jnp.where(hmask, jnp.broadcast_to(q[t:t + 1, :], (n_heads, d)), 0.0) for t in range(n_tok)], axis=0)
    qbd16 = qbd.astype(BF16)

    for jb in range(blk_per_step):
        kts = [kp[jb * pages_per_blk + i][0] for i in range(pages_per_blk)]
        vts = [vp[jb * pages_per_blk + i][0] for i in range(pages_per_blk)]
        s = jnp.concatenate([_dot(qbd16, kt) for kt in kts], axis=1)
        gate = jnp.sum(s, axis=1, keepdims=True) * (1.0 / MOBA_BLOCK)
        mj = jnp.max(s, axis=1, keepdims=True)
        p = jnp.exp(s - mj)
        lj = jnp.sum(p, axis=1, keepdims=True)
        oj = sum(_dot_nt(p[:, i * page:(i + 1) * page], vts[i]) for i in range(pages_per_blk))
        slot = step * blk_per_step + jb
        m_s[slot] = jnp.broadcast_to(mj, (nrow, LANES))
        l_s[slot] = jnp.broadcast_to(lj, (nrow, LANES))
        g_s[slot] = jnp.broadcast_to(gate, (nrow, LANES))
        acc_s[slot] = oj

    @pl.when(step == n_steps - 1)
    def _():
        gates = [g_s[j][:, 0:1] for j in range(n_past)]
        kn = kn_ref[0]
        vn = vn_ref[0]
        trow = lax.broadcasted_iota(jnp.int32, (nrow, 1), 0) // n_heads
        s_own = [jnp.where(trow >= t, jnp.sum(qbd * kn[t:t + 1, :], axis=1, keepdims=True), NEG_BIG)
                 for t in range(n_tok)]
        m = functools.reduce(jnp.maximum, s_own)
        p_own = [jnp.exp(s - m) for s in s_own]
        l = sum(p_own)
        acc = sum(p_own[t] * vn[t:t + 1, :] for t in range(n_tok))
        for j in range(n_past):
            rank = sum(jnp.where(jnp.logical_or(gates[i] > gates[j],
                                                jnp.logical_and(gates[i] == gates[j], i < j)), 1.0, 0.0)
                       for i in range(n_past) if i != j)
            selj = rank < float(MOBA_TOPK)
            mj = m_s[j][:, 0:1]
            m_new = jnp.where(selj, jnp.maximum(m, mj), m)
            alpha = jnp.exp(m - m_new)
            beta = jnp.where(selj, jnp.exp(mj - m_new), 0.0)
            l = l * alpha + l_s[j][:, 0:1] * beta
            acc = acc * alpha + acc_s[j] * beta
            m = m_new
        lane_head_r = lax.broadcasted_iota(jnp.int32, (nrow, d), 1) // HEAD_DIM
        row_head_r = lax.broadcasted_iota(jnp.int32, (nrow, d), 0) % n_heads
        out = jnp.where(lane_head_r == row_head_r, acc / l, 0.0)
        o_ref[0] = jnp.concatenate(
            [jnp.sum(out[t * n_heads:(t + 1) * n_heads], axis=0, keepdims=True) for t in range(n_tok)],
            axis=0).astype(o_ref.dtype)


def _moba_decode(q, kn, vn, cache_kt, cache_vt, page_table, pages_per_step):
    bsz, n_tok, d = q.shape
    n_heads = d // HEAD_DIM
    pool, _, page = cache_kt.shape
    n_pages = page_table.shape[1]
    assert n_pages % pages_per_step == 0 and MOBA_BLOCK % page == 0
    assert (n_pages * page) % MOBA_BLOCK == 0 and n_tok <= MOBA_BLOCK
    assert pages_per_step % (MOBA_BLOCK // page) == 0
    n_steps = n_pages // pages_per_step
    n_past = n_pages * page // MOBA_BLOCK
    nrow = n_tok * n_heads
    tok = pl.BlockSpec((1, n_tok, d), lambda b, s, pt: (b, 0, 0))

    def page_spec(i):
        return pl.BlockSpec((1, d, page), lambda b, s, pt: (pt[b, s * pages_per_step + i], 0, 0))

    grid_spec = pltpu.PrefetchScalarGridSpec(
        num_scalar_prefetch=1,
        grid=(bsz, n_steps),
        in_specs=[tok, tok, tok] + [page_spec(i) for i in range(pages_per_step)] * 2,
        out_specs=tok,
        scratch_shapes=[pltpu.VMEM((n_past, nrow, LANES), F32)] * 3 + [pltpu.VMEM((n_past, nrow, d), F32)],
    )
    return pl.pallas_call(
        functools.partial(_moba_decode_kernel, n_tok, n_heads, pages_per_step, n_steps),
        out_shape=jax.ShapeDtypeStruct((bsz, n_tok, d), BF16),
        grid_spec=grid_spec,
        compiler_params=_cparams(2),
        name="moba_decode",
    )(page_table, q, kn, vn, *([cache_kt] * pages_per_step), *([cache_vt] * pages_per_step))


def _pad_cols(w, n):
    return jnp.pad(w, ((0, 0), (0, n - w.shape[1])))


def _pad_rows(w, n):
    return jnp.pad(w, ((0, n - w.shape[0]), (0, 0)))


def _rope_tables(pos):
    half = ROT_DIM // 2
    inv = ROPE_THETA ** (-2.0 * jnp.arange(half, dtype=F32) / ROT_DIM)
    ang = pos.astype(F32)[:, None] * inv[None, :]
    cos = jnp.cos(ang)
    sin = jnp.sin(ang)
    t = pos.shape[0]
    ones = jnp.ones((t, HEAD_DIM - ROT_DIM), F32)
    zeros_r = jnp.zeros((t, HEAD_DIM - ROT_DIM), F32)
    zeros_h = jnp.zeros((t, half), F32)
    c = jnp.concatenate([cos, cos, ones], axis=1)
    s1 = jnp.concatenate([-sin, zeros_h, zeros_r], axis=1)
    s2 = jnp.concatenate([zeros_h, sin, zeros_r], axis=1)
    rep = LANES // HEAD_DIM
    return jnp.tile(c, (1, rep)), jnp.tile(s1, (1, rep)), jnp.tile(s2, (1, rep))


def _unpack_state(s):
    b, hp, dv, dk2 = s.shape
    return s.reshape(b, hp, dv, 2, dk2 // 2).transpose(0, 1, 3, 2, 4).reshape(b, hp * 2, dv, dk2 // 2)


def kernel(x_prompt, x_sample, state_wkv, state_shift, cache_k, cache_v, state_conv, page_table, p_prompt, p_sample, rw_mix, rw_rkv, rw_w0, rw_w1, rw_w2, rw_a0, rw_a1, rw_a2, rw_g1, rw_g2, rw_kk, rw_ka, rw_rk, rw_lnw, rw_lnb, rw_wo, mb_wqkv, mb_wo, norm_mix, norm_ffn, ff_wup, ff_conv_w, ff_conv_b, ff_wdown, ple_wp, ple_wg, norm_out):
    bp, tp, d = x_prompt.shape
    bs, ts, _ = x_sample.shape
    depth = norm_mix.shape[0]
    n_heads = d // HEAD_DIM
    dff = ff_wdown.shape[1]
    pool, page = cache_k.shape[1], cache_k.shape[2]
    past_len = page_table.shape[1] * page
    assert d % LANES == 0 and depth == 2 and ts >= 2

    lane_head = jnp.arange(d) // HEAD_DIM
    segr = (lane_head[:, None] == jnp.arange(LANES)[None, :]).astype(BF16)
    segb = segr.T
    seg2 = ((jnp.arange(LANES) // HEAD_DIM)[:, None] == (jnp.arange(LANES) // HEAD_DIM)[None, :]).astype(BF16)

    def row(v):
        return v.reshape(1, -1).astype(F32)

    def col(v):
        return jnp.broadcast_to(v.reshape(-1, 1).astype(F32), (v.size, bs))

    rw = dict(
        gmix=row(norm_mix[0]), mix=rw_mix[0], wrkv=rw_rkv[0].astype(BF16), w0=row(rw_w0[0]),
        w1=_pad_cols(rw_w1[0], LANES).astype(BF16), w2=_pad_rows(rw_w2[0], LANES).astype(BF16),
        a0=row(rw_a0[0]), a1=_pad_cols(rw_a1[0], LANES).astype(BF16), a2=_pad_rows(rw_a2[0], LANES).astype(BF16),
        g1=_pad_cols(rw_g1[0], 2 * LANES).astype(BF16), g2=_pad_rows(rw_g2[0], 2 * LANES).astype(BF16),
        kk=row(rw_kk[0]), ka=row(rw_ka[0]), segr=segr, segb=segb,
        rk=row(rw_rk[0]), lnw=row(rw_lnw[0]), lnb=row(rw_lnb[0]), seg2=seg2)

    def ffw(i, wo):
        return dict(wo=wo.astype(BF16), gffn=row(norm_ffn[i]), wup=ff_wup[i].astype(BF16), cw=ff_conv_w[i],
                    cb=row(ff_conv_b[i]), wd=ff_wdown[i].astype(BF16), wg=ple_wg[i].astype(BF16),
                    wp=ple_wp[i].astype(BF16), gout=row(norm_out))

    fw0 = ffw(0, rw_wo[0])
    fw1 = ffw(1, mb_wo[0])
    wqkv = mb_wqkv[0].astype(BF16)
    gmix1 = row(norm_mix[1])

    xp = x_prompt.reshape(bp * tp, d)
    pp = p_prompt.reshape(depth, bp * tp, -1)
    xs = x_sample.transpose(1, 0, 2).reshape(ts * bs, d)
    ps = p_sample.transpose(0, 2, 1, 3).reshape(depth, ts * bs, -1)

    def conv_state_p(ug, uv):
        last = lambda z: z.reshape(bp, -1, SUBLANES, dff)[:, -1, SUBLANES - 2:]
        return jnp.concatenate([last(ug), last(uv)], axis=-1)

    def conv_state_s(ug, uv):
        return jnp.concatenate([ug[ts - 2:], uv[ts - 2:]], axis=-1).transpose(1, 0, 2)

    r, lw, k, v, a, b, g, shift_p = _rwkv_proj(xp, tp, None, rw)
    sh = lambda z: z.reshape(bp, tp, d)
    o_p, wkv_p = _wkv(sh(r), sh(lw), sh(k), sh(v), sh(a), sh(b), sh(g), rw,
                      chunk=64, tblk=min(512, tp), bb=bp, npair=4)
    xp, ugp, uvp = _ffn(xp, o_p.reshape(bp * tp, d), pp[0], tp, None, fw0, False, 512, 256)
    conv_p0 = conv_state_p(ugp, uvp)

    r, lw, k, v, a, b, g, shift_s = _rwkv_proj(xs, ts, state_shift[0], rw)
    st_in = jnp.transpose(state_wkv[0], (1, 2, 3, 0))
    o_s, st_out = _wkv_decode(r, lw, k, v, a, b, g, col(rw_rk[0]), col(rw_lnw[0]), col(rw_lnb[0]), st_in, 2)
    wkv_s = jnp.transpose(st_out, (3, 0, 1, 2))
    xs, ugs, uvs = _ffn(xs, o_s, ps[0], ts, state_conv[0].reshape(bs, 4 * dff), fw0, False, 512, 256)
    conv_s0 = conv_state_s(ugs, uvs)

    cos, s1, s2 = _rope_tables(jnp.arange(tp, dtype=jnp.int32))
    cos_p, s1_p, s2_p = (jnp.tile(z, (bp, 1)) for z in (cos, s1, s2))
    q16, kt, vt, k16, vt16 = _qkv(xp, tp, gmix1, wqkv, cos_p, s1_p, s2_p, True, 256)
    k_p = kt.reshape(bp, n_heads, HEAD_DIM, tp).transpose(0, 3, 1, 2)
    v_p = vt.reshape(bp, n_heads, HEAD_DIM, tp).transpose(0, 3, 1, 2)
    o_p = _moba_prompt(q16.reshape(bp, tp, d), k16.reshape(bp, tp, d), vt16, 4)
    xp, ugp, uvp = _ffn(xp, o_p.reshape(bp * tp, d), pp[1], tp, None, fw1, True, 512, 256)
    conv_p1 = conv_state_p(ugp, uvp)

    cos, s1, s2 = _rope_tables(past_len + jnp.arange(ts, dtype=jnp.int32))
    cos_s, s1_s, s2_s = (jnp.repeat(z, bs, axis=0) for z in (cos, s1, s2))
    q, kt, vt, k, v = _qkv(xs, bs, gmix1, wqkv, cos_s, s1_s, s2_s, False, 128)
    k_s = kt.reshape(ts, n_heads, HEAD_DIM, bs).transpose(3, 0, 1, 2)
    v_s = vt.reshape(ts, n_heads, HEAD_DIM, bs).transpose(3, 0, 1, 2)
    by_seq = lambda z: z.reshape(ts, bs, d).transpose(1, 0, 2)
    cache_kt = jnp.transpose(cache_k[0], (0, 2, 3, 1)).reshape(pool, d, page)
    cache_vt = jnp.transpose(cache_v[0], (0, 2, 3, 1)).reshape(pool, d, page)
    o_s = _moba_decode(by_seq(q), by_seq(k), by_seq(v), cache_kt, cache_vt, page_table, 4)
    o_s = o_s.transpose(1, 0, 2).reshape(ts * bs, d)
    xs, ugs, uvs = _ffn(xs, o_s, ps[1], ts, state_conv[1].reshape(bs, 4 * dff), fw1, True, 512, 256)
    conv_s1 = conv_state_s(ugs, uvs)

    return (xp.reshape(bp, tp, d), by_seq(xs),
            _unpack_state(wkv_p)[None], shift_p.reshape(1, bp, d), k_p[None], v_p[None],
            jnp.stack([conv_p0, conv_p1]),
            wkv_s[None], shift_s[None], k_s[None], v_s[None],
            jnp.stack([conv_s0, conv_s1]))
```

```python
import functools

import jax
import jax.numpy as jnp
from jax import lax
from jax.experimental import pallas as pl
from jax.experimental.pallas import tpu as pltpu

F32 = jnp.float32
BF16 = jnp.bfloat16

HEAD_DIM = 64
GN_EPS = 64e-5
RMS_EPS = 1e-6
MOBA_BLOCK = 256
MOBA_TOPK = 3
ROPE_THETA = 500000.0
ROT_DIM = HEAD_DIM // 4
NEG_BIG = -1e30
LOG2E = 1.4426950408889634
LANES = 128
SUBLANES = 8
VMEM_LIMIT_BYTES = 56 * 1024 * 1024


def _cparams(n_axes):
    return pltpu.CompilerParams(dimension_semantics=("arbitrary",) * n_axes,
                                vmem_limit_bytes=VMEM_LIMIT_BYTES)


def _const_spec(shape):
    nd = len(shape)
    return pl.BlockSpec(shape, lambda *_: (0,) * nd)


def _dot(a, b):
    return jnp.dot(a.astype(BF16), b.astype(BF16), preferred_element_type=F32)


def _dot_nt(a, b):
    return lax.dot_general(a.astype(BF16), b.astype(BF16), (((1,), (1,)), ((), ())),
                           preferred_element_type=F32)


def _dot_tn(a, b):
    return lax.dot_general(a.astype(BF16), b.astype(BF16), (((0,), (0,)), ((), ())),
                           preferred_element_type=F32)


def _split_dot(x, m):
    hi = x.astype(BF16)
    lo = (x - hi.astype(F32)).astype(BF16)
    return (jnp.dot(hi, m, preferred_element_type=F32) + jnp.dot(lo, m, preferred_element_type=F32))


def _sigmoid(z):
    return 1.0 / (1.0 + jnp.exp(-z))


def _rms(x, g):
    return x * lax.rsqrt(jnp.mean(x * x, axis=-1, keepdims=True) + RMS_EPS) * g


def _seg_sum(x, segr, segb):
    return _split_dot(_split_dot(x, segr), segb)


def _prev_rows_carry(u, k, carry):
    rolled = pltpu.roll(u, k, axis=0)
    row = lax.broadcasted_iota(jnp.int32, (u.shape[0], 1), 0)
    out = rolled
    for j in range(k):
        out = jnp.where(row == j, carry[SUBLANES - k + j:SUBLANES - k + j + 1, :], out)
    return out


def _rwkv_proj_kernel(block_mode, tiles_per_seq, *refs):
    if block_mode:
        (x_ref, init_ref, gmix_ref, mix_ref, wrkv_ref, w0_ref, w1_ref, w2_ref, a0_ref, a1_ref,
         a2_ref, g1_ref, g2_ref, kk_ref, ka_ref, segr_ref, segb_ref,
         r_o, lw_o, k_o, v_o, a_o, b_o, g_o, xn_o, carry) = refs
    else:
        (x_ref, gmix_ref, mix_ref, wrkv_ref, w0_ref, w1_ref, w2_ref, a0_ref, a1_ref, a2_ref,
         g1_ref, g2_ref, kk_ref, ka_ref, segr_ref, segb_ref,
         r_o, lw_o, k_o, v_o, a_o, b_o, g_o, xn_o, carry) = refs
    xn = _rms(x_ref[...], gmix_ref[...])
    tm = xn.shape[0]
    if block_mode:
        @pl.when(pl.program_id(0) == 0)
        def _():
            carry[...] = init_ref[...]
        xprev = carry[...]
        carry[...] = xn
        xn_o[...] = xn
    else:
        @pl.when(pl.program_id(0) % tiles_per_seq == 0)
        def _():
            carry[...] = jnp.zeros_like(carry)
        xprev = _prev_rows_carry(xn, 1, carry)
        carry[...] = xn[tm - SUBLANES:tm, :]
        xn_o[0] = xn[tm - 1:tm, :]
    dx = xprev - xn

    def xm(i):
        return (xn + dx * mix_ref[i:i + 1, :]).astype(BF16)

    r = jnp.dot(xm(0), wrkv_ref[0], preferred_element_type=F32)
    k = jnp.dot(xm(1), wrkv_ref[1], preferred_element_type=F32)
    v = jnp.dot(xm(2), wrkv_ref[2], preferred_element_type=F32)
    wl = w0_ref[...] + _dot(jnp.tanh(jnp.dot(xm(3), w1_ref[...], preferred_element_type=F32)), w2_ref[...])
    sp = jnp.maximum(-wl, 0.0) + jnp.log(1.0 + jnp.exp(-jnp.abs(wl)))
    lw = -jnp.exp(-sp - 0.5)
    a = _sigmoid(a0_ref[...] + _dot(jnp.dot(xm(4), a1_ref[...], preferred_element_type=F32), a2_ref[...]))
    g = _dot(_sigmoid(jnp.dot(xm(5), g1_ref[...], preferred_element_type=F32)), g2_ref[...])
    kk = k * kk_ref[...]
    n2 = _seg_sum(kk * kk, segr_ref[...], segb_ref[...])
    kk = kk / jnp.maximum(jnp.sqrt(n2), 1e-12)
    outs = (r, lw, k * (1.0 + (a - 1.0) * ka_ref[...]), v, -kk, kk * a, g)
    for o_ref, val in zip((r_o, lw_o, k_o, v_o, a_o, b_o, g_o), outs):
        if block_mode:
            o_ref[0] = val.T
        else:
            o_ref[...] = val


def _rwkv_proj(x, seq_len, init, w):
    n, d = x.shape
    block_mode = init is not None
    consts = [w['gmix'], w['mix'], w['wrkv'], w['w0'], w['w1'], w['w2'], w['a0'], w['a1'], w['a2'],
              w['g1'], w['g2'], w['kk'], w['ka'], w['segr'], w['segb']]
    if block_mode:
        tm = n // seq_len
        tiles_per_seq = 1
        row = pl.BlockSpec((tm, d), lambda i: (i, 0))
        in_specs = [row, _const_spec(init.shape)] + [_const_spec(c.shape) for c in consts]
        args = [x, init] + consts
        out_shape = [jax.ShapeDtypeStruct((seq_len, d, tm), F32)] * 7 + [jax.ShapeDtypeStruct((tm, d), F32)]
        out_specs = [pl.BlockSpec((1, d, tm), lambda i: (i, 0, 0))] * 7 + [_const_spec((tm, d))]
        scratch = [pltpu.VMEM((tm, d), F32)]
    else:
        tm = min(256, seq_len)
        assert seq_len % tm == 0 and n % seq_len == 0
        tiles_per_seq = seq_len // tm
        row = pl.BlockSpec((tm, d), lambda i: (i, 0))
        in_specs = [row] + [_const_spec(c.shape) for c in consts]
        args = [x] + consts
        out_shape = [jax.ShapeDtypeStruct((n, d), F32)] * 7 + [jax.ShapeDtypeStruct((n // seq_len, 1, d), F32)]
        out_specs = [row] * 7 + [pl.BlockSpec((1, 1, d), lambda i: (i // tiles_per_seq, 0, 0))]
        scratch = [pltpu.VMEM((SUBLANES, d), F32)]
    return pl.pallas_call(
        functools.partial(_rwkv_proj_kernel, block_mode, tiles_per_seq),
        out_shape=out_shape,
        grid=(n // tm,),
        in_specs=in_specs,
        out_specs=out_specs,
        scratch_shapes=scratch,
        compiler_params=_cparams(1),
        name="rwkv_proj",
    )(*args)


def _wkv_masks(c):
    n = 2 * c
    row = lax.broadcasted_iota(jnp.int32, (n, n), 0)
    col = lax.broadcasted_iota(jnp.int32, (n, n), 1)
    rq = row >= c
    cq = col >= c
    tl = jnp.logical_and(jnp.logical_not(rq), jnp.logical_not(cq))
    br = jnp.logical_and(rq, cq)
    tr = jnp.logical_and(jnp.logical_not(rq), cq)
    bl = jnp.logical_and(rq, jnp.logical_not(cq))
    strict = (col % c) < (row % c)
    incl = (col % c) <= (row % c)
    one = jnp.ones((n, n), F32)
    zero = jnp.zeros((n, n), F32)

    def f(m):
        return jnp.where(m, one, zero)

    return dict(
        ab0=f(jnp.logical_and(strict, tl)), ab1=f(jnp.logical_and(strict, br)),
        ak0=f(jnp.logical_and(strict, tr)), ak1=f(jnp.logical_and(strict, bl)),
        rb0=f(jnp.logical_and(incl, bl)), rb1=f(jnp.logical_and(incl, tr)),
        rk0=f(jnp.logical_and(incl, br)), rk1=f(jnp.logical_and(incl, tl)),
        eye=f(row == col),
    )


def _wkv_chunk(c, n_rounds, units, tri, msk, m0, m1, bdmask):
    nu = range(len(units))
    pre = []
    for r, lw, k, v, a, b, state in units:
        g_inc = jnp.dot(tri, lw, preferred_element_type=F32, precision=lax.Precision.HIGHEST)
        e_inc = jnp.exp(g_inc)
        e_neg = jnp.exp(-g_inc)
        at = a * jnp.exp(g_inc - lw)
        rt = r * e_inc
        bt = b * e_neg
        kt = k * e_neg
        pre.append(dict(
            lr=jnp.concatenate([at, rt], axis=0), rl=jnp.concatenate([rt, at], axis=0),
            bk=jnp.concatenate([bt, kt], axis=0), kb=jnp.concatenate([kt, bt], axis=0),
            vst=jnp.concatenate([v * m1, v * m0], axis=0), v=v, state=state, gc=e_inc[c - 1:c, :]))
    sc0 = [_dot_nt(p['lr'] * m0, p['bk']) for p in pre]
    sc1 = [_dot_nt(p['rl'] * m1, p['kb']) for p in pre]
    xs = [_dot_nt(p['lr'], p['state']) for p in pre]
    ab = [sc0[i] * msk['ab0'] + sc1[i] * msk['ab1'] for i in nu]
    ak = [sc0[i] * msk['ak0'] + sc1[i] * msk['ak1'] for i in nu]
    rbk = [jnp.concatenate([sc0[i] * msk['rb0'] + sc1[i] * msk['rb1'],
                            sc0[i] * msk['rk0'] + sc1[i] * msk['rk1']], axis=1) for i in nu]
    akv = [_dot(ak[i], pre[i]['vst']) for i in nu]
    tinv = [msk['eye'] + ab[i] for i in nu]
    pw = ab
    for _ in range(n_rounds):
        pw = [_dot(pw[i], pw[i]) for i in nu]
        tinv = [tinv[i] + _dot(tinv[i], pw[i]) for i in nu]
    zst = [jnp.concatenate([xs[i][:c] * m0, xs[i][:c] * m1], axis=0) + akv[i] for i in nu]
    w = [_dot(tinv[i], zst[i]) for i in nu]
    ys = [_dot(rbk[i], jnp.concatenate([w[i], pre[i]['vst']], axis=0)) for i in nu]
    ds = [_dot_tn(jnp.concatenate([w[i][:c] + w[i][c:], pre[i]['v']], axis=0), pre[i]['bk']) for i in nu]
    outs = []
    for i in nu:
        y = xs[i][c:] + ys[i][:c] + ys[i][c:]
        outs.append((y, (pre[i]['state'] + ds[i] * bdmask) * pre[i]['gc']))
    return outs


def _wkv_kernel(c, n_chunks, bb, npair, r_ref, lw_ref, k_ref, v_ref, a_ref, b_ref, g_ref, rk_ref, lnw_ref,
                lnb_ref, seg_ref, o_ref, s_out, state, ybuf):
    tstep = pl.program_id(2)
    lane = lax.broadcasted_iota(jnp.int32, (1, LANES), 1)
    m0 = jnp.where(lane < HEAD_DIM, 1.0, 0.0).astype(F32)
    m1 = 1.0 - m0
    rowi = lax.broadcasted_iota(jnp.int32, (LANES, LANES), 0)
    coli = lax.broadcasted_iota(jnp.int32, (LANES, LANES), 1)
    bdmask = jnp.where((rowi >= HEAD_DIM) == (coli >= HEAD_DIM), 1.0, 0.0).astype(F32)
    half = LANES // 2

    @pl.when(tstep == 0)
    def _():
        state[...] = jnp.zeros_like(state)

    ti = lax.broadcasted_iota(jnp.int32, (c, c), 0)
    tj = lax.broadcasted_iota(jnp.int32, (c, c), 1)
    tri = jnp.where(tj <= ti, 1.0, 0.0).astype(F32)
    msk = _wkv_masks(c)
    n_rounds = max(c.bit_length() - 2, 0)

    def chunk_body(ci, carry):
        t0 = pl.multiple_of(ci * c, c)
        rows = pl.ds(t0, c)
        units = [(ib, ip) for ib in range(bb) for ip in range(npair)]
        ins = []
        for ib, ip in units:
            ls = slice(ip * LANES, (ip + 1) * LANES)
            ins.append((r_ref[ib, rows, ls], lw_ref[ib, rows, ls], k_ref[ib, rows, ls],
                        v_ref[ib, rows, ls], a_ref[ib, rows, ls], b_ref[ib, rows, ls],
                        state[ib * npair + ip]))
        outs = _wkv_chunk(c, n_rounds, ins, tri, msk, m0, m1, bdmask)
        for (ib, ip), (y, ns) in zip(units, outs):
            state[ib * npair + ip] = ns
            ybuf[ib, rows, slice(ip * LANES, (ip + 1) * LANES)] = y
        return carry

    lax.fori_loop(0, n_chunks, chunk_body, 0)

    seg = seg_ref[...]
    for ib in range(bb):
        for ip in range(npair):
            ls = slice(ip * LANES, (ip + 1) * LANES)
            y = ybuf[ib, :, ls]
            mu = _split_dot(y, seg) * (1.0 / HEAD_DIM)
            yc = y - mu
            var = _split_dot(yc * yc, seg) * (1.0 / HEAD_DIM)
            yn = yc * lax.rsqrt(var + GN_EPS) * lnw_ref[:, ls] + lnb_ref[:, ls]
            rr = r_ref[ib, :, ls]
            bonus = _split_dot(rr * k_ref[ib, :, ls] * rk_ref[:, ls], seg) * v_ref[ib, :, ls]
            o_ref[ib, :, ls] = ((yn + bonus) * g_ref[ib, :, ls]).astype(o_ref.dtype)

    @pl.when(tstep == pl.num_programs(2) - 1)
    def _():
        for ib in range(bb):
            for ip in range(npair):
                st = state[ib * npair + ip]
                s_out[ib, ip] = st[:half] + st[half:]


def _wkv(r, lw, k, v, a, b, g, w, chunk, tblk, bb, npair):
    bsz, t, d = r.shape
    np_total = d // LANES
    assert bsz % bb == 0 and np_total % npair == 0 and t % tblk == 0 and tblk % chunk == 0
    blk = pl.BlockSpec((bb, tblk, npair * LANES), lambda ib, ip, it: (ib, it, ip))
    par = pl.BlockSpec((1, npair * LANES), lambda ib, ip, it: (0, ip))
    st_spec = pl.BlockSpec((bb, npair, HEAD_DIM, LANES), lambda ib, ip, it: (ib, ip, 0, 0))
    return pl.pallas_call(
        functools.partial(_wkv_kernel, chunk, tblk // chunk, bb, npair),
        out_shape=[jax.ShapeDtypeStruct((bsz, t, d), BF16),
                   jax.ShapeDtypeStruct((bsz, np_total, HEAD_DIM, LANES), F32)],
        grid=(bsz // bb, np_total // npair, t // tblk),
        in_specs=[blk] * 7 + [par] * 3 + [_const_spec(w['seg2'].shape)],
        out_specs=[blk, st_spec],
        scratch_shapes=[pltpu.VMEM((bb * npair, LANES, LANES), F32),
                        pltpu.VMEM((bb, tblk, npair * LANES), F32)],
        compiler_params=_cparams(3),
        name="wkv",
    )(r, lw, k, v, a, b, g, w['rk'], w['lnw'], w['lnb'], w['seg2'])


def _wkv_decode_kernel(n_tok, hb, r_ref, lw_ref, k_ref, v_ref, a_ref, b_ref, g_ref, rk_ref, lnw_ref, lnb_ref,
                       s_in, o_ref, s_out, w_s, y_s):
    bsz = r_ref.shape[2]
    w_s[...] = jnp.exp(lw_ref[...])
    for h in range(hb):
        rows = slice(h * HEAD_DIM, (h + 1) * HEAD_DIM)

        def body(i, carry, h=h, rows=rows):
            s = s_in[h, i]
            for t in range(n_tok):
                sa = jnp.sum(s * a_ref[t, rows, :], axis=0, keepdims=True)
                vi = v_ref[t, pl.ds(h * HEAD_DIM + i, 1), :]
                s = s * w_s[t, rows, :] + sa * b_ref[t, rows, :] + vi * k_ref[t, rows, :]
                y_s[t, pl.ds(h * HEAD_DIM + i, 1), :] = jnp.sum(s * r_ref[t, rows, :], axis=0, keepdims=True)
            s_out[h, i] = s
            return carry

        lax.fori_loop(0, HEAD_DIM, body, 0)

    for t in range(n_tok):
        outs = []
        for h in range(hb):
            rows = slice(h * HEAD_DIM, (h + 1) * HEAD_DIM)
            y = y_s[t, rows, :]
            mu = jnp.mean(y, axis=0, keepdims=True)
            yc = y - mu
            var = jnp.mean(yc * yc, axis=0, keepdims=True)
            yn = yc * lax.rsqrt(var + GN_EPS) * lnw_ref[rows, :] + lnb_ref[rows, :]
            bonus = jnp.sum(r_ref[t, rows, :] * k_ref[t, rows, :] * rk_ref[rows, :], axis=0,
                            keepdims=True) * v_ref[t, rows, :]
            outs.append((yn + bonus) * g_ref[t, rows, :])
        o_ref[t * bsz:(t + 1) * bsz, :] = jnp.concatenate(outs, axis=0).T.astype(o_ref.dtype)


def _wkv_decode(r, lw, k, v, a, b, g, rk, lnw, lnb, state, hb):
    n_tok, d, bsz = r.shape
    n_heads = d // HEAD_DIM
    assert n_heads % hb == 0
    blk = pl.BlockSpec((n_tok, hb * HEAD_DIM, bsz), lambda i: (0, i, 0))
    par = pl.BlockSpec((hb * HEAD_DIM, bsz), lambda i: (i, 0))
    st = pl.BlockSpec((hb, HEAD_DIM, HEAD_DIM, bsz), lambda i: (i, 0, 0, 0))
    return pl.pallas_call(
        functools.partial(_wkv_decode_kernel, n_tok, hb),
        out_shape=[jax.ShapeDtypeStruct((n_tok * bsz, d), BF16), jax.ShapeDtypeStruct(state.shape, F32)],
        grid=(n_heads // hb,),
        in_specs=[blk] * 7 + [par] * 3 + [st],
        out_specs=[pl.BlockSpec((n_tok * bsz, hb * HEAD_DIM), lambda i: (0, i)), st],
        scratch_shapes=[pltpu.VMEM((n_tok, hb * HEAD_DIM, bsz), F32)] * 2,
        compiler_params=_cparams(1),
        name="wkv_decode",
    )(r, lw, k, v, a, b, g, rk, lnw, lnb, state)


def _ffn_kernel(block_mode, tiles_per_seq, fc, final_norm, *refs):
    refs = list(refs)
    x_ref, o_ref, p_ref = refs[:3]
    pos = 3
    if block_mode:
        st_ref = refs[pos]
        pos += 1
    (wo_ref, gffn_ref, wup_ref, cw_ref, cb_ref, wd_ref, wg_ref, wp_ref, gout_ref) = refs[pos:pos + 9]
    pos += 9
    y_o, u_o = refs[pos:pos + 2]
    pos += 2
    x1_s, hn_s, act_s = refs[pos:pos + 3]
    pos += 3
    i = pl.program_id(0)
    dff = wd_ref.shape[0]
    n_f = dff // fc
    tm = x_ref.shape[0]
    if block_mode:
        c1_s, c2_s = refs[pos:pos + 2]

        @pl.when(i == 0)
        def _():
            c2_s[...] = st_ref[:, 0:2 * dff]
            c1_s[...] = st_ref[:, 2 * dff:4 * dff]
    else:
        c_s = refs[pos]

        @pl.when(i % tiles_per_seq == 0)
        def _():
            c_s[...] = jnp.zeros_like(c_s)

    x1 = x_ref[...] + jnp.dot(o_ref[...], wo_ref[...], preferred_element_type=F32)
    x1_s[...] = x1
    hn_s[...] = _rms(x1, gffn_ref[...]).astype(BF16)

    def up(f):
        hn = hn_s[...]
        return tuple(jnp.dot(hn, wup_ref[:, half * dff + f * fc:half * dff + (f + 1) * fc],
                             preferred_element_type=F32) for half in range(2))

    u_next = up(0)
    for f in range(n_f):
        u_cur = u_next
        if f + 1 < n_f:
            u_next = up(f + 1)
        conv = []
        for half in range(2):
            cols = slice(half * dff + f * fc, half * dff + (f + 1) * fc)
            u = u_cur[half]
            if block_mode:
                u1, u2 = c1_s[:, cols], c2_s[:, cols]
                c2_s[:, cols] = u1
                c1_s[:, cols] = u
                u_o[0, :, cols] = u
            else:
                carry = c_s[:, cols]
                u1, u2 = _prev_rows_carry(u, 1, carry), _prev_rows_carry(u, 2, carry)
                c_s[:, cols] = u[tm - SUBLANES:tm, :]
                u_o[0, :, cols] = u[tm - SUBLANES:tm, :]
            conv.append(cb_ref[:, cols] + cw_ref[0:1, cols] * u2 + cw_ref[1:2, cols] * u1 + cw_ref[2:3, cols] * u)
        gate, val = conv
        act_s[:, f * fc:(f + 1) * fc] = (gate * _sigmoid(gate) * val).astype(BF16)

    x2 = x1_s[...] + jnp.dot(act_s[...], wd_ref[...], preferred_element_type=F32)
    x3 = x2 + _sigmoid(_dot(x2, wg_ref[...])) * _dot(p_ref[...], wp_ref[...])
    if final_norm:
        x3 = _rms(x3, gout_ref[...])
    y_o[...] = x3


def _ffn(x, o, p, seq_len, conv_state, w, final_norm, tm, fc):
    n, d = x.shape
    dff = w['wd'].shape[0]
    pdim = p.shape[1]
    block_mode = conv_state is not None
    assert dff % fc == 0
    if block_mode:
        tm = n // seq_len
        tiles_per_seq = 1
    else:
        tm = min(tm, seq_len)
        assert seq_len % tm == 0 and n % seq_len == 0
        tiles_per_seq = seq_len // tm
    rowd = pl.BlockSpec((tm, d), lambda i: (i, 0))
    rowp = pl.BlockSpec((tm, pdim), lambda i: (i, 0))

    def resident(a):
        nd = a.ndim
        return pl.BlockSpec(a.shape, lambda i: (0,) * nd, pipeline_mode=pl.Buffered(1))

    in_specs = [rowd, rowd, rowp]
    args = [x, o, p]
    if block_mode:
        in_specs.append(resident(conv_state))
        args.append(conv_state)
    consts = [w['wo'], w['gffn'], w['wup'], w['cw'], w['cb'], w['wd'], w['wg'], w['wp'], w['gout']]
    in_specs += [resident(c) for c in consts]
    args += consts
    if block_mode:
        u_shape = jax.ShapeDtypeStruct((seq_len, tm, 2 * dff), F32)
        u_spec = pl.BlockSpec((1, tm, 2 * dff), lambda i: (i, 0, 0))
        scratch_c = [pltpu.VMEM((tm, 2 * dff), F32)] * 2
    else:
        u_shape = jax.ShapeDtypeStruct((n // tm, SUBLANES, 2 * dff), F32)
        u_spec = pl.BlockSpec((1, SUBLANES, 2 * dff), lambda i: (i, 0, 0))
        scratch_c = [pltpu.VMEM((SUBLANES, 2 * dff), F32)]
    return pl.pallas_call(
        functools.partial(_ffn_kernel, block_mode, tiles_per_seq, fc, final_norm),
        out_shape=[jax.ShapeDtypeStruct((n, d), F32), u_shape],
        grid=(n // tm,),
        in_specs=in_specs,
        out_specs=[rowd, u_spec],
        scratch_shapes=[pltpu.VMEM((tm, d), F32), pltpu.VMEM((tm, d), BF16), pltpu.VMEM((tm, dff), BF16)] + scratch_c,
        compiler_params=_cparams(1),
        name="ffn",
    )(*args)


def _qkv_kernel(prompt_mode, *refs):
    if prompt_mode:
        (x_ref, g_ref, w_ref, cos_ref, s1_ref, s2_ref, q_o, kt_o, vt_o, k16_o, vt16_o) = refs
    else:
        (x_ref, g_ref, w_ref, cos_ref, s1_ref, s2_ref, q_o, kt_o, vt_o, k_o, v_o) = refs
    hn = _rms(x_ref[...], g_ref[...]).astype(BF16)
    d = hn.shape[1]
    reps = d // LANES
    cos = jnp.concatenate([cos_ref[...]] * reps, axis=1)
    s1 = jnp.concatenate([s1_ref[...]] * reps, axis=1)
    s2 = jnp.concatenate([s2_ref[...]] * reps, axis=1)
    half = ROT_DIM // 2

    def rope(z):
        return z * cos + pltpu.roll(z, d - half, axis=1) * s1 + pltpu.roll(z, half, axis=1) * s2

    q = rope(jnp.dot(hn, w_ref[:, 0:d], preferred_element_type=F32)) * (HEAD_DIM ** -0.5)
    k = rope(jnp.dot(hn, w_ref[:, d:2 * d], preferred_element_type=F32))
    v = jnp.dot(hn, w_ref[:, 2 * d:3 * d], preferred_element_type=F32)
    vt = v.T
    kt_o[0] = k.T
    vt_o[0] = vt
    if prompt_mode:
        q_o[...] = (q * LOG2E).astype(BF16)
        k16_o[...] = k.astype(BF16)
        vt16_o[0] = vt.astype(BF16)
    else:
        q_o[...] = q
        k_o[...] = k
        v_o[...] = v


def _qkv(x, group, gmix, wqkv, cos, s1, s2, prompt_mode, tm):
    n, d = x.shape
    tm = min(tm, group)
    assert group % tm == 0 and n % group == 0
    tpg = group // tm
    row = pl.BlockSpec((tm, d), lambda i: (i, 0))
    rowt = pl.BlockSpec((tm, LANES), lambda i: (i, 0))
    tr = pl.BlockSpec((1, d, tm), lambda i: (i // tpg, 0, i % tpg))
    big = jax.ShapeDtypeStruct((n, d), F32)
    big16 = jax.ShapeDtypeStruct((n, d), BF16)
    bigt = jax.ShapeDtypeStruct((n // group, d, group), F32)
    if prompt_mode:
        out_shape = [big16, bigt, bigt, big16, jax.ShapeDtypeStruct((n // group, d, group), BF16)]
        out_specs = [row, tr, tr, row, tr]
    else:
        out_shape = [big, bigt, bigt, big, big]
        out_specs = [row, tr, tr, row, row]
    return pl.pallas_call(
        functools.partial(_qkv_kernel, prompt_mode),
        out_shape=out_shape,
        grid=(n // tm,),
        in_specs=[row, _const_spec(gmix.shape), _const_spec(wqkv.shape), rowt, rowt, rowt],
        out_specs=out_specs,
        compiler_params=_cparams(1),
        name="qkv",
    )(x, gmix, wqkv, cos, s1, s2)


def _moba_kernel(n_blocks, grp, q_ref, k_ref, vt_ref, o_ref, o_s, m_s, l_s, g_s, w_s):
    qi = pl.program_id(2)
    blk = MOBA_BLOCK
    q = q_ref[0]
    lane = lax.broadcasted_iota(jnp.int32, (1, LANES), 1)
    zero16 = jnp.zeros_like(q)
    qh = [jnp.where(lane < HEAD_DIM, q, zero16), jnp.where(lane >= HEAD_DIM, q, zero16)]
    ones_rows = jnp.where(lax.broadcasted_iota(jnp.int32, (2 * SUBLANES, blk), 0) == 0, 1.0, 0.0).astype(BF16)
    neg_inf = jnp.float32(-jnp.inf)
    for h in range(2):
        m_s[h] = jnp.full((n_blocks, blk), NEG_BIG, F32)
        l_s[h] = jnp.zeros((n_blocks, blk), F32)
        g_s[h] = jnp.full((n_blocks, blk), neg_inf, F32)

    def scores(j0, nb):
        k0 = pl.multiple_of(j0 * blk, blk)
        kg = k_ref[0, pl.ds(k0, nb * blk), :]
        return [lax.dot_general(kg, qh[h], (((1,), (1,)), ((), ())), preferred_element_type=F32)
                for h in range(2)]

    def partials(j0, nb, st, diag):
        parts = []
        for h in range(2):
            for jb in range(nb):
                s = st[h][jb * blk:(jb + 1) * blk]
                if diag:
                    kr = lax.broadcasted_iota(jnp.int32, (blk, blk), 0)
                    qc = lax.broadcasted_iota(jnp.int32, (blk, blk), 1)
                    s = jnp.where(kr <= qc, s, NEG_BIG)
                gate = jnp.sum(s, axis=0, keepdims=True) * (1.0 / blk)
                mj = jnp.max(s, axis=0, keepdims=True)
                p = jnp.exp2(s - mj).astype(BF16)
                parts.append((h, jb, gate, mj, p))
        pvs = []
        for h, jb, gate, mj, p in parts:
            kb0 = pl.multiple_of((j0 + jb) * blk, blk)
            vt_ext = jnp.concatenate([vt_ref[0, h * HEAD_DIM:(h + 1) * HEAD_DIM, pl.ds(kb0, blk)], ones_rows],
                                     axis=0)
            pvs.append(jnp.dot(vt_ext, p, preferred_element_type=F32))
        for (h, jb, gate, mj, p), pv in zip(parts, pvs):
            j = j0 + jb
            o_s[h * n_blocks + j] = pv[:HEAD_DIM]
            l_s[h, pl.ds(j, 1), :] = pv[HEAD_DIM:HEAD_DIM + 1]
            m_s[h, pl.ds(j, 1), :] = mj
            g_s[h, pl.ds(j, 1), :] = gate

    def group_body(gi, carry):
        partials(gi * grp, grp, scores(gi * grp, grp), False)
        return carry

    lax.fori_loop(0, (qi + grp - 1) // grp, group_body, 0)
    partials(qi, 1, scores(qi, 1), True)

    jrow = lax.broadcasted_iota(jnp.int32, (n_blocks, 1), 0)
    jrow_f = jrow.astype(F32)
    outs = []
    for h in range(2):
        cur = jnp.where(jrow < qi, g_s[h], neg_inf)
        sel = jrow == qi
        for _ in range(MOBA_TOPK):
            mx = jnp.max(cur, axis=0, keepdims=True)
            first = jnp.min(jnp.where(cur == mx, jrow_f, float(n_blocks)), axis=0, keepdims=True)
            onehot = jrow_f == first
            sel = jnp.logical_or(sel, jnp.logical_and(onehot, mx > neg_inf))
            cur = jnp.where(onehot, neg_inf, cur)
        m = m_s[h]
        mx = jnp.max(jnp.where(sel, m, NEG_BIG), axis=0, keepdims=True)
        w = jnp.where(sel, jnp.exp2(m - mx), 0.0)
        w_s[h] = w
        lsum = jnp.sum(w * l_s[h], axis=0, keepdims=True)

        def merge(j, acc, h=h):
            return acc + w_s[h, pl.ds(j, 1), :] * o_s[h * n_blocks + j]

        acc = lax.fori_loop(0, qi + 1, merge, jnp.zeros((HEAD_DIM, blk), F32))
        outs.append(acc / lsum)
    o_ref[0] = jnp.concatenate(outs, axis=0).T.astype(o_ref.dtype)


def _moba_prompt(q16, k16, vt16, grp):
    bsz, t, d = q16.shape
    n_blocks = t // MOBA_BLOCK
    assert n_blocks % grp == 0
    stat = pltpu.VMEM((2, n_blocks, MOBA_BLOCK), F32)
    return pl.pallas_call(
        functools.partial(_moba_kernel, n_blocks, grp),
        out_shape=jax.ShapeDtypeStruct((bsz, t, d), BF16),
        grid=(bsz, d // LANES, n_blocks),
        in_specs=[pl.BlockSpec((1, MOBA_BLOCK, LANES), lambda b, p, i: (b, i, p)),
                  pl.BlockSpec((1, t, LANES), lambda b, p, i: (b, 0, p)),
                  pl.BlockSpec((1, LANES, t), lambda b, p, i: (b, p, 0))],
        out_specs=pl.BlockSpec((1, MOBA_BLOCK, LANES), lambda b, p, i: (b, i, p)),
        scratch_shapes=[pltpu.VMEM((2 * n_blocks, HEAD_DIM, MOBA_BLOCK), F32), stat, stat, stat, stat],
        compiler_params=_cparams(3),
        name="moba_prompt",
    )(q16, k16, vt16)


def _moba_decode_kernel(n_tok, n_heads, pages_per_step, n_steps, pt_ref, q_ref, kn_ref, vn_ref, *refs):
    kp = refs[:pages_per_step]
    vp = refs[pages_per_step:2 * pages_per_step]
    o_ref = refs[2 * pages_per_step]
    m_s, l_s, g_s, acc_s = refs[2 * pages_per_step + 1:]
    step = pl.program_id(1)
    d = q_ref.shape[2]
    page = kp[0].shape[2]
    pages_per_blk = MOBA_BLOCK // page
    blk_per_step = pages_per_step // pages_per_blk
    n_past = n_steps * blk_per_step
    nrow = n_tok * n_heads
    q = q_ref[0]
    lane_head = lax.broadcasted_iota(jnp.int32, (n_heads, d), 1) // HEAD_DIM
    row_head = lax.broadcasted_iota(jnp.int32, (n_heads, d), 0)
    hmask = lane_head == row_head
    qbd = jnp.concatenate(
        [jnp.where(hmask, jnp.broadcast_to(q[t:t + 1, :], (n_heads, d)), 0.0) for t in range(n_tok)], axis=0)
    qbd16 = qbd.astype(BF16)

    for jb in range(blk_per_step):
        kts = [kp[jb * pages_per_blk + i][0] for i in range(pages_per_blk)]
        vts = [vp[jb * pages_per_blk + i][0] for i in range(pages_per_blk)]
        s = jnp.concatenate([_dot(qbd16, kt) for kt in kts], axis=1)
        gate = jnp.sum(s, axis=1, keepdims=True) * (1.0 / MOBA_BLOCK)
        mj = jnp.max(s, axis=1, keepdims=True)
        p = jnp.exp(s - mj)
        lj = jnp.sum(p, axis=1, keepdims=True)
        oj = sum(_dot_nt(p[:, i * page:(i + 1) * page], vts[i]) for i in range(pages_per_blk))
        slot = step * blk_per_step + jb
        m_s[slot] = jnp.broadcast_to(mj, (nrow, LANES))
        l_s[slot] = jnp.broadcast_to(lj, (nrow, LANES))
        g_s[slot] = jnp.broadcast_to(gate, (nrow, LANES))
        acc_s[slot] = oj

    @pl.when(step == n_steps - 1)
    def _():
        gates = [g_s[j][:, 0:1] for j in range(n_past)]
        kn = kn_ref[0]
        vn = vn_ref[0]
        trow = lax.broadcasted_iota(jnp.int32, (nrow, 1), 0) // n_heads
        s_own = [jnp.where(trow >= t, jnp.sum(qbd * kn[t:t + 1, :], axis=1, keepdims=True), NEG_BIG)
                 for t in range(n_tok)]
        m = functools.reduce(jnp.maximum, s_own)
        p_own = [jnp.exp(s - m) for s in s_own]
        l = sum(p_own)
        acc = sum(p_own[t] * vn[t:t + 1, :] for t in range(n_tok))
        for j in range(n_past):
            rank = sum(jnp.where(jnp.logical_or(gates[i] > gates[j],
                                                jnp.logical_and(gates[i] == gates[j], i < j)), 1.0, 0.0)
                       for i in range(n_past) if i != j)
            selj = rank < float(MOBA_TOPK)
            mj = m_s[j][:, 0:1]
            m_new = jnp.where(selj, jnp.maximum(m, mj), m)
            alpha = jnp.exp(m - m_new)
            beta = jnp.where(selj, jnp.exp(mj - m_new), 0.0)
            l = l * alpha + l_s[j][:, 0:1] * beta
            acc = acc * alpha + acc_s[j] * beta
            m = m_new
        lane_head_r = lax.broadcasted_iota(jnp.int32, (nrow, d), 1) // HEAD_DIM
        row_head_r = lax.broadcasted_iota(jnp.int32, (nrow, d), 0) % n_heads
        out = jnp.where(lane_head_r == row_head_r, acc / l, 0.0)
        o_ref[0] = jnp.concatenate(
            [jnp.sum(out[t * n_heads:(t + 1) * n_heads], axis=0, keepdims=True) for t in range(n_tok)],
            axis=0).astype(o_ref.dtype)


def _moba_decode(q, kn, vn, cache_kt, cache_vt, page_table, pages_per_step):
    bsz, n_tok, d = q.shape
    n_heads = d // HEAD_DIM
    pool, _, page = cache_kt.shape
    n_pages = page_table.shape[1]
    assert n_pages % pages_per_step == 0 and MOBA_BLOCK % page == 0
    assert (n_pages * page) % MOBA_BLOCK == 0 and n_tok <= MOBA_BLOCK
    assert pages_per_step % (MOBA_BLOCK // page) == 0
    n_steps = n_pages // pages_per_step
    n_past = n_pages * page // MOBA_BLOCK
    nrow = n_tok * n_heads
    tok = pl.BlockSpec((1, n_tok, d), lambda b, s, pt: (b, 0, 0))

    def page_spec(i):
        return pl.BlockSpec((1, d, page), lambda b, s, pt: (pt[b, s * pages_per_step + i], 0, 0))

    grid_spec = pltpu.PrefetchScalarGridSpec(
        num_scalar_prefetch=1,
        grid=(bsz, n_steps),
        in_specs=[tok, tok, tok] + [page_spec(i) for i in range(pages_per_step)] * 2,
        out_specs=tok,
        scratch_shapes=[pltpu.VMEM((n_past, nrow, LANES), F32)] * 3 + [pltpu.VMEM((n_past, nrow, d), F32)],
    )
    return pl.pallas_call(
        functools.partial(_moba_decode_kernel, n_tok, n_heads, pages_per_step, n_steps),
        out_shape=jax.ShapeDtypeStruct((bsz, n_tok, d), BF16),
        grid_spec=grid_spec,
        compiler_params=_cparams(2),
        name="moba_decode",
    )(page_table, q, kn, vn, *([cache_kt] * pages_per_step), *([cache_vt] * pages_per_step))


def _pad_cols(w, n):
    return jnp.pad(w, ((0, 0), (0, n - w.shape[1])))


def _pad_rows(w, n):
    return jnp.pad(w, ((0, n - w.shape[0]), (0, 0)))


def _rope_tables(pos):
    half = ROT_DIM // 2
    inv = ROPE_THETA ** (-2.0 * jnp.arange(half, dtype=F32) / ROT_DIM)
    ang = pos.astype(F32)[:, None] * inv[None, :]
    cos = jnp.cos(ang)
    sin = jnp.sin(ang)
    t = pos.shape[0]
    ones = jnp.ones((t, HEAD_DIM - ROT_DIM), F32)
    zeros_r = jnp.zeros((t, HEAD_DIM - ROT_DIM), F32)
    zeros_h = jnp.zeros((t, half), F32)
    c = jnp.concatenate([cos, cos, ones], axis=1)
    s1 = jnp.concatenate([-sin, zeros_h, zeros_r], axis=1)
    s2 = jnp.concatenate([zeros_h, sin, zeros_r], axis=1)
    rep = LANES // HEAD_DIM
    return jnp.tile(c, (1, rep)), jnp.tile(s1, (1, rep)), jnp.tile(s2, (1, rep))


def _unpack_state(s):
    b, hp, dv, dk2 = s.shape
    return s.reshape(b, hp, dv, 2, dk2 // 2).transpose(0, 1, 3, 2, 4).reshape(b, hp * 2, dv, dk2 // 2)


def kernel(x_prompt, x_sample, state_wkv, state_shift, cache_k, cache_v, state_conv, page_table, p_prompt, p_sample, rw_mix, rw_rkv, rw_w0, rw_w1, rw_w2, rw_a0, rw_a1, rw_a2, rw_g1, rw_g2, rw_kk, rw_ka, rw_rk, rw_lnw, rw_lnb, rw_wo, mb_wqkv, mb_wo, norm_mix, norm_ffn, ff_wup, ff_conv_w, ff_conv_b, ff_wdown, ple_wp, ple_wg, norm_out):
    bp, tp, d = x_prompt.shape
    bs, ts, _ = x_sample.shape
    depth = norm_mix.shape[0]
    n_heads = d // HEAD_DIM
    dff = ff_wdown.shape[1]
    pool, page = cache_k.shape[1], cache_k.shape[2]
    past_len = page_table.shape[1] * page
    assert d % LANES == 0 and depth == 2 and ts >= 2

    lane_head = jnp.arange(d) // HEAD_DIM
    segr = (lane_head[:, None] == jnp.arange(LANES)[None, :]).astype(BF16)
    segb = segr.T
    seg2 = ((jnp.arange(LANES) // HEAD_DIM)[:, None] == (jnp.arange(LANES) // HEAD_DIM)[None, :]).astype(BF16)

    def row(v):
        return v.reshape(1, -1).astype(F32)

    def col(v):
        return jnp.broadcast_to(v.reshape(-1, 1).astype(F32), (v.size, bs))

    rw = dict(
        gmix=row(norm_mix[0]), mix=rw_mix[0], wrkv=rw_rkv[0].astype(BF16), w0=row(rw_w0[0]),
        w1=_pad_cols(rw_w1[0], LANES).astype(BF16), w2=_pad_rows(rw_w2[0], LANES).astype(BF16),
        a0=row(rw_a0[0]), a1=_pad_cols(rw_a1[0], LANES).astype(BF16), a2=_pad_rows(rw_a2[0], LANES).astype(BF16),
        g1=_pad_cols(rw_g1[0], 2 * LANES).astype(BF16), g2=_pad_rows(rw_g2[0], 2 * LANES).astype(BF16),
        kk=row(rw_kk[0]), ka=row(rw_ka[0]), segr=segr, segb=segb,
        rk=row(rw_rk[0]), lnw=row(rw_lnw[0]), lnb=row(rw_lnb[0]), seg2=seg2)

    def ffw(i, wo):
        return dict(wo=wo.astype(BF16), gffn=row(norm_ffn[i]), wup=ff_wup[i].astype(BF16), cw=ff_conv_w[i],
                    cb=row(ff_conv_b[i]), wd=ff_wdown[i].astype(BF16), wg=ple_wg[i].astype(BF16),
                    wp=ple_wp[i].astype(BF16), gout=row(norm_out))

    fw0 = ffw(0, rw_wo[0])
    fw1 = ffw(1, mb_wo[0])
    wqkv = mb_wqkv[0].astype(BF16)
    gmix1 = row(norm_mix[1])

    xp = x_prompt.reshape(bp * tp, d)
    pp = p_prompt.reshape(depth, bp * tp, -1)
    xs = x_sample.transpose(1, 0, 2).reshape(ts * bs, d)
    ps = p_sample.transpose(0, 2, 1, 3).reshape(depth, ts * bs, -1)

    def conv_state_p(u):
        return u.reshape(bp, -1, SUBLANES, 2 * dff)[:, -1, SUBLANES - 2:]

    def conv_state_s(u):
        return u[ts - 2:].transpose(1, 0, 2)

    r, lw, k, v, a, b, g, shift_p = _rwkv_proj(xp, tp, None, rw)
    sh = lambda z: z.reshape(bp, tp, d)
    o_p, wkv_p = _wkv(sh(r), sh(lw), sh(k), sh(v), sh(a), sh(b), sh(g), rw,
                      chunk=64, tblk=min(512, tp), bb=bp, npair=4)
    xp, up_ = _ffn(xp, o_p.reshape(bp * tp, d), pp[0], tp, None, fw0, False, 512, 256)
    conv_p0 = conv_state_p(up_)

    r, lw, k, v, a, b, g, shift_s = _rwkv_proj(xs, ts, state_shift[0], rw)
    st_in = jnp.transpose(state_wkv[0], (1, 2, 3, 0))
    o_s, st_out = _wkv_decode(r, lw, k, v, a, b, g, col(rw_rk[0]), col(rw_lnw[0]), col(rw_lnb[0]), st_in, 2)
    wkv_s = jnp.transpose(st_out, (3, 0, 1, 2))
    xs, us_ = _ffn(xs, o_s, ps[0], ts, state_conv[0].reshape(bs, 4 * dff), fw0, False, 512, 256)
    conv_s0 = conv_state_s(us_)

    cos, s1, s2 = _rope_tables(jnp.arange(tp, dtype=jnp.int32))
    cos_p, s1_p, s2_p = (jnp.tile(z, (bp, 1)) for z in (cos, s1, s2))
    q16, kt, vt, k16, vt16 = _qkv(xp, tp, gmix1, wqkv, cos_p, s1_p, s2_p, True, 256)
    k_p = kt.reshape(bp, n_heads, HEAD_DIM, tp).transpose(0, 3, 1, 2)
    v_p = vt.reshape(bp, n_heads, HEAD_DIM, tp).transpose(0, 3, 1, 2)
    o_p = _moba_prompt(q16.reshape(bp, tp, d), k16.reshape(bp, tp, d), vt16, min(8, tp // MOBA_BLOCK))
    xp, up_ = _ffn(xp, o_p.reshape(bp * tp, d), pp[1], tp, None, fw1, True, 512, 256)
    conv_p1 = conv_state_p(up_)

    cos, s1, s2 = _rope_tables(past_len + jnp.arange(ts, dtype=jnp.int32))
    cos_s, s1_s, s2_s = (jnp.repeat(z, bs, axis=0) for z in (cos, s1, s2))
    q, kt, vt, k, v = _qkv(xs, bs, gmix1, wqkv, cos_s, s1_s, s2_s, False, 128)
    k_s = kt.reshape(ts, n_heads, HEAD_DIM, bs).transpose(3, 0, 1, 2)
    v_s = vt.reshape(ts, n_heads, HEAD_DIM, bs).transpose(3, 0, 1, 2)
    by_seq = lambda z: z.reshape(ts, bs, d).transpose(1, 0, 2)
    cache_kt = jnp.transpose(cache_k[0], (0, 2, 3, 1)).reshape(pool, d, page)
    cache_vt = jnp.transpose(cache_v[0], (0, 2, 3, 1)).reshape(pool, d, page)
    o_s = _moba_decode(by_seq(q), by_seq(k), by_seq(v), cache_kt, cache_vt, page_table, 4)
    o_s = o_s.transpose(1, 0, 2).reshape(ts * bs, d)
    xs, us_ = _ffn(xs, o_s, ps[1], ts, state_conv[1].reshape(bs, 4 * dff), fw1, True, 512, 256)
    conv_s1 = conv_state_s(us_)

    return (xp.reshape(bp, tp, d), by_seq(xs),
            _unpack_state(wkv_p)[None], shift_p.reshape(1, bp, d), k_p[None], v_p[None],
            jnp.stack([conv_p0, conv_p1]),
            wkv_s[None], shift_s[None], k_s[None], v_s[None],
            jnp.stack([conv_s0, conv_s1]))
```

```python
import functools

import jax
import jax.numpy as jnp
from jax import lax
from jax.experimental import pallas as pl
from jax.experimental.pallas import tpu as pltpu

F32 = jnp.float32
BF16 = jnp.bfloat16

HEAD_DIM = 64
GN_EPS = 64e-5
RMS_EPS = 1e-6
MOBA_BLOCK = 256
MOBA_TOPK = 3
ROPE_THETA = 500000.0
ROT_DIM = HEAD_DIM // 4
NEG_BIG = -1e30
LOG2E = 1.4426950408889634
LANES = 128
SUBLANES = 8
VMEM_LIMIT_BYTES = 56 * 1024 * 1024


def _cparams(n_axes):
    return pltpu.CompilerParams(dimension_semantics=("arbitrary",) * n_axes,
                                vmem_limit_bytes=VMEM_LIMIT_BYTES)


def _const_spec(shape):
    nd = len(shape)
    return pl.BlockSpec(shape, lambda *_: (0,) * nd)


def _dot(a, b):
    return jnp.dot(a.astype(BF16), b.astype(BF16), preferred_element_type=F32)


def _dot_nt(a, b):
    return lax.dot_general(a.astype(BF16), b.astype(BF16), (((1,), (1,)), ((), ())),
                           preferred_element_type=F32)


def _dot_tn(a, b):
    return lax.dot_general(a.astype(BF16), b.astype(BF16), (((0,), (0,)), ((), ())),
                           preferred_element_type=F32)


def _split_dot(x, m):
    hi = x.astype(BF16)
    lo = (x - hi.astype(F32)).astype(BF16)
    return (jnp.dot(hi, m, preferred_element_type=F32) + jnp.dot(lo, m, preferred_element_type=F32))


def _sigmoid(z):
    return 1.0 / (1.0 + jnp.exp(-z))


def _rms(x, g):
    return x * lax.rsqrt(jnp.mean(x * x, axis=-1, keepdims=True) + RMS_EPS) * g


def _seg_sum(x, segr, segb):
    return _split_dot(_split_dot(x, segr), segb)


def _prev_rows_carry(u, k, carry):
    rolled = pltpu.roll(u, k, axis=0)
    row = lax.broadcasted_iota(jnp.int32, (u.shape[0], 1), 0)
    out = rolled
    for j in range(k):
        out = jnp.where(row == j, carry[SUBLANES - k + j:SUBLANES - k + j + 1, :], out)
    return out


def _rwkv_proj_kernel(block_mode, tiles_per_seq, *refs):
    if block_mode:
        (x_ref, init_ref, gmix_ref, mix_ref, wrkv_ref, w0_ref, w1_ref, w2_ref, a0_ref, a1_ref,
         a2_ref, g1_ref, g2_ref, kk_ref, ka_ref, segr_ref, segb_ref,
         r_o, lw_o, k_o, v_o, a_o, b_o, g_o, xn_o, carry) = refs
    else:
        (x_ref, gmix_ref, mix_ref, wrkv_ref, w0_ref, w1_ref, w2_ref, a0_ref, a1_ref, a2_ref,
         g1_ref, g2_ref, kk_ref, ka_ref, segr_ref, segb_ref,
         r_o, lw_o, k_o, v_o, a_o, b_o, g_o, xn_o, carry) = refs
    xn = _rms(x_ref[...], gmix_ref[...])
    tm = xn.shape[0]
    if block_mode:
        @pl.when(pl.program_id(0) == 0)
        def _():
            carry[...] = init_ref[...]
        xprev = carry[...]
        carry[...] = xn
        xn_o[...] = xn
    else:
        @pl.when(pl.program_id(0) % tiles_per_seq == 0)
        def _():
            carry[...] = jnp.zeros_like(carry)
        xprev = _prev_rows_carry(xn, 1, carry)
        carry[...] = xn[tm - SUBLANES:tm, :]
        xn_o[0] = xn[tm - 1:tm, :]
    dx = xprev - xn

    def xm(i):
        return (xn + dx * mix_ref[i:i + 1, :]).astype(BF16)

    r = jnp.dot(xm(0), wrkv_ref[0], preferred_element_type=F32)
    k = jnp.dot(xm(1), wrkv_ref[1], preferred_element_type=F32)
    v = jnp.dot(xm(2), wrkv_ref[2], preferred_element_type=F32)
    wl = w0_ref[...] + _dot(jnp.tanh(jnp.dot(xm(3), w1_ref[...], preferred_element_type=F32)), w2_ref[...])
    sp = jnp.maximum(-wl, 0.0) + jnp.log(1.0 + jnp.exp(-jnp.abs(wl)))
    lw = -jnp.exp(-sp - 0.5)
    a = _sigmoid(a0_ref[...] + _dot(jnp.dot(xm(4), a1_ref[...], preferred_element_type=F32), a2_ref[...]))
    g = _dot(_sigmoid(jnp.dot(xm(5), g1_ref[...], preferred_element_type=F32)), g2_ref[...])
    kk = k * kk_ref[...]
    n2 = _seg_sum(kk * kk, segr_ref[...], segb_ref[...])
    kk = kk / jnp.maximum(jnp.sqrt(n2), 1e-12)
    outs = (r, lw, k * (1.0 + (a - 1.0) * ka_ref[...]), v, -kk, kk * a, g)
    for o_ref, val in zip((r_o, lw_o, k_o, v_o, a_o, b_o, g_o), outs):
        if block_mode:
            o_ref[0] = val.T
        else:
            o_ref[...] = val


def _rwkv_proj(x, seq_len, init, w):
    n, d = x.shape
    block_mode = init is not None
    consts = [w['gmix'], w['mix'], w['wrkv'], w['w0'], w['w1'], w['w2'], w['a0'], w['a1'], w['a2'],
              w['g1'], w['g2'], w['kk'], w['ka'], w['segr'], w['segb']]
    if block_mode:
        tm = n // seq_len
        tiles_per_seq = 1
        row = pl.BlockSpec((tm, d), lambda i: (i, 0))
        in_specs = [row, _const_spec(init.shape)] + [_const_spec(c.shape) for c in consts]
        args = [x, init] + consts
        out_shape = [jax.ShapeDtypeStruct((seq_len, d, tm), F32)] * 7 + [jax.ShapeDtypeStruct((tm, d), F32)]
        out_specs = [pl.BlockSpec((1, d, tm), lambda i: (i, 0, 0))] * 7 + [_const_spec((tm, d))]
        scratch = [pltpu.VMEM((tm, d), F32)]
    else:
        tm = min(256, seq_len)
        assert seq_len % tm == 0 and n % seq_len == 0
        tiles_per_seq = seq_len // tm
        row = pl.BlockSpec((tm, d), lambda i: (i, 0))
        in_specs = [row] + [_const_spec(c.shape) for c in consts]
        args = [x] + consts
        out_shape = [jax.ShapeDtypeStruct((n, d), F32)] * 7 + [jax.ShapeDtypeStruct((n // seq_len, 1, d), F32)]
        out_specs = [row] * 7 + [pl.BlockSpec((1, 1, d), lambda i: (i // tiles_per_seq, 0, 0))]
        scratch = [pltpu.VMEM((SUBLANES, d), F32)]
    return pl.pallas_call(
        functools.partial(_rwkv_proj_kernel, block_mode, tiles_per_seq),
        out_shape=out_shape,
        grid=(n // tm,),
        in_specs=in_specs,
        out_specs=out_specs,
        scratch_shapes=scratch,
        compiler_params=_cparams(1),
        name="rwkv_proj",
    )(*args)


def _wkv_masks(c):
    n = 2 * c
    row = lax.broadcasted_iota(jnp.int32, (n, n), 0)
    col = lax.broadcasted_iota(jnp.int32, (n, n), 1)
    rq = row >= c
    cq = col >= c
    tl = jnp.logical_and(jnp.logical_not(rq), jnp.logical_not(cq))
    br = jnp.logical_and(rq, cq)
    tr = jnp.logical_and(jnp.logical_not(rq), cq)
    bl = jnp.logical_and(rq, jnp.logical_not(cq))
    strict = (col % c) < (row % c)
    incl = (col % c) <= (row % c)
    one = jnp.ones((n, n), F32)
    zero = jnp.zeros((n, n), F32)

    def f(m):
        return jnp.where(m, one, zero)

    return dict(
        ab0=f(jnp.logical_and(strict, tl)), ab1=f(jnp.logical_and(strict, br)),
        ak0=f(jnp.logical_and(strict, tr)), ak1=f(jnp.logical_and(strict, bl)),
        rb0=f(jnp.logical_and(incl, bl)), rb1=f(jnp.logical_and(incl, tr)),
        rk0=f(jnp.logical_and(incl, br)), rk1=f(jnp.logical_and(incl, tl)),
        eye=f(row == col),
    )


def _wkv_chunk(c, n_rounds, units, tri, msk, m0, m1, bdmask):
    nu = range(len(units))
    pre = []
    for r, lw, k, v, a, b, state in units:
        g_inc = jnp.dot(tri, lw, preferred_element_type=F32, precision=lax.Precision.HIGHEST)
        e_inc = jnp.exp(g_inc)
        e_neg = jnp.exp(-g_inc)
        at = a * jnp.exp(g_inc - lw)
        rt = r * e_inc
        bt = b * e_neg
        kt = k * e_neg
        pre.append(dict(
            lr=jnp.concatenate([at, rt], axis=0), rl=jnp.concatenate([rt, at], axis=0),
            bk=jnp.concatenate([bt, kt], axis=0), kb=jnp.concatenate([kt, bt], axis=0),
            vst=jnp.concatenate([v * m1, v * m0], axis=0), v=v, state=state, gc=e_inc[c - 1:c, :]))
    sc0 = [_dot_nt(p['lr'] * m0, p['bk']) for p in pre]
    sc1 = [_dot_nt(p['rl'] * m1, p['kb']) for p in pre]
    xs = [_dot_nt(p['lr'], p['state']) for p in pre]
    ab = [sc0[i] * msk['ab0'] + sc1[i] * msk['ab1'] for i in nu]
    ak = [sc0[i] * msk['ak0'] + sc1[i] * msk['ak1'] for i in nu]
    rbk = [jnp.concatenate([sc0[i] * msk['rb0'] + sc1[i] * msk['rb1'],
                            sc0[i] * msk['rk0'] + sc1[i] * msk['rk1']], axis=1) for i in nu]
    akv = [_dot(ak[i], pre[i]['vst']) for i in nu]
    tinv = [msk['eye'] + ab[i] for i in nu]
    pw = ab
    for _ in range(n_rounds):
        pw = [_dot(pw[i], pw[i]) for i in nu]
        tinv = [tinv[i] + _dot(tinv[i], pw[i]) for i in nu]
    zst = [jnp.concatenate([xs[i][:c] * m0, xs[i][:c] * m1], axis=0) + akv[i] for i in nu]
    w = [_dot(tinv[i], zst[i]) for i in nu]
    ys = [_dot(rbk[i], jnp.concatenate([w[i], pre[i]['vst']], axis=0)) for i in nu]
    ds = [_dot_tn(jnp.concatenate([w[i][:c] + w[i][c:], pre[i]['v']], axis=0), pre[i]['bk']) for i in nu]
    outs = []
    for i in nu:
        y = xs[i][c:] + ys[i][:c] + ys[i][c:]
        outs.append((y, (pre[i]['state'] + ds[i] * bdmask) * pre[i]['gc']))
    return outs


def _wkv_kernel(c, n_chunks, bb, npair, r_ref, lw_ref, k_ref, v_ref, a_ref, b_ref, g_ref, rk_ref, lnw_ref,
                lnb_ref, seg_ref, o_ref, s_out, state, ybuf):
    tstep = pl.program_id(2)
    lane = lax.broadcasted_iota(jnp.int32, (1, LANES), 1)
    m0 = jnp.where(lane < HEAD_DIM, 1.0, 0.0).astype(F32)
    m1 = 1.0 - m0
    rowi = lax.broadcasted_iota(jnp.int32, (LANES, LANES), 0)
    coli = lax.broadcasted_iota(jnp.int32, (LANES, LANES), 1)
    bdmask = jnp.where((rowi >= HEAD_DIM) == (coli >= HEAD_DIM), 1.0, 0.0).astype(F32)
    half = LANES // 2

    @pl.when(tstep == 0)
    def _():
        state[...] = jnp.zeros_like(state)

    ti = lax.broadcasted_iota(jnp.int32, (c, c), 0)
    tj = lax.broadcasted_iota(jnp.int32, (c, c), 1)
    tri = jnp.where(tj <= ti, 1.0, 0.0).astype(F32)
    msk = _wkv_masks(c)
    n_rounds = max(c.bit_length() - 2, 0)

    def chunk_body(ci, carry):
        t0 = pl.multiple_of(ci * c, c)
        rows = pl.ds(t0, c)
        units = [(ib, ip) for ib in range(bb) for ip in range(npair)]
        ins = []
        for ib, ip in units:
            ls = slice(ip * LANES, (ip + 1) * LANES)
            ins.append((r_ref[ib, rows, ls], lw_ref[ib, rows, ls], k_ref[ib, rows, ls],
                        v_ref[ib, rows, ls], a_ref[ib, rows, ls], b_ref[ib, rows, ls],
                        state[ib * npair + ip]))
        outs = _wkv_chunk(c, n_rounds, ins, tri, msk, m0, m1, bdmask)
        for (ib, ip), (y, ns) in zip(units, outs):
            state[ib * npair + ip] = ns
            ybuf[ib, rows, slice(ip * LANES, (ip + 1) * LANES)] = y
        return carry

    lax.fori_loop(0, n_chunks, chunk_body, 0)

    seg = seg_ref[...]
    for ib in range(bb):
        for ip in range(npair):
            ls = slice(ip * LANES, (ip + 1) * LANES)
            y = ybuf[ib, :, ls]
            mu = _split_dot(y, seg) * (1.0 / HEAD_DIM)
            yc = y - mu
            var = _split_dot(yc * yc, seg) * (1.0 / HEAD_DIM)
            yn = yc * lax.rsqrt(var + GN_EPS) * lnw_ref[:, ls] + lnb_ref[:, ls]
            rr = r_ref[ib, :, ls]
            bonus = _split_dot(rr * k_ref[ib, :, ls] * rk_ref[:, ls], seg) * v_ref[ib, :, ls]
            o_ref[ib, :, ls] = ((yn + bonus) * g_ref[ib, :, ls]).astype(o_ref.dtype)

    @pl.when(tstep == pl.num_programs(2) - 1)
    def _():
        for ib in range(bb):
            for ip in range(npair):
                st = state[ib * npair + ip]
                s_out[ib, ip] = st[:half] + st[half:]


def _wkv(r, lw, k, v, a, b, g, w, chunk, tblk, bb, npair):
    bsz, t, d = r.shape
    np_total = d // LANES
    assert bsz % bb == 0 and np_total % npair == 0 and t % tblk == 0 and tblk % chunk == 0
    blk = pl.BlockSpec((bb, tblk, npair * LANES), lambda ib, ip, it: (ib, it, ip))
    par = pl.BlockSpec((1, npair * LANES), lambda ib, ip, it: (0, ip))
    st_spec = pl.BlockSpec((bb, npair, HEAD_DIM, LANES), lambda ib, ip, it: (ib, ip, 0, 0))
    return pl.pallas_call(
        functools.partial(_wkv_kernel, chunk, tblk // chunk, bb, npair),
        out_shape=[jax.ShapeDtypeStruct((bsz, t, d), BF16),
                   jax.ShapeDtypeStruct((bsz, np_total, HEAD_DIM, LANES), F32)],
        grid=(bsz // bb, np_total // npair, t // tblk),
        in_specs=[blk] * 7 + [par] * 3 + [_const_spec(w['seg2'].shape)],
        out_specs=[blk, st_spec],
        scratch_shapes=[pltpu.VMEM((bb * npair, LANES, LANES), F32),
                        pltpu.VMEM((bb, tblk, npair * LANES), F32)],
        compiler_params=_cparams(3),
        name="wkv",
    )(r, lw, k, v, a, b, g, w['rk'], w['lnw'], w['lnb'], w['seg2'])


def _wkv_decode_kernel(n_tok, hb, r_ref, lw_ref, k_ref, v_ref, a_ref, b_ref, g_ref, rk_ref, lnw_ref, lnb_ref,
                       s_in, o_ref, s_out, w_s, y_s):
    bsz = r_ref.shape[2]
    w_s[...] = jnp.exp(lw_ref[...])
    for h in range(hb):
        rows = slice(h * HEAD_DIM, (h + 1) * HEAD_DIM)

        def body(i, carry, h=h, rows=rows):
            s = s_in[h, i]
            for t in range(n_tok):
                sa = jnp.sum(s * a_ref[t, rows, :], axis=0, keepdims=True)
                vi = v_ref[t, pl.ds(h * HEAD_DIM + i, 1), :]
                s = s * w_s[t, rows, :] + sa * b_ref[t, rows, :] + vi * k_ref[t, rows, :]
                y_s[t, pl.ds(h * HEAD_DIM + i, 1), :] = jnp.sum(s * r_ref[t, rows, :], axis=0, keepdims=True)
            s_out[h, i] = s
            return carry

        lax.fori_loop(0, HEAD_DIM, body, 0)

    for t in range(n_tok):
        outs = []
        for h in range(hb):
            rows = slice(h * HEAD_DIM, (h + 1) * HEAD_DIM)
            y = y_s[t, rows, :]
            mu = jnp.mean(y, axis=0, keepdims=True)
            yc = y - mu
            var = jnp.mean(yc * yc, axis=0, keepdims=True)
            yn = yc * lax.rsqrt(var + GN_EPS) * lnw_ref[rows, :] + lnb_ref[rows, :]
            bonus = jnp.sum(r_ref[t, rows, :] * k_ref[t, rows, :] * rk_ref[rows, :], axis=0,
                            keepdims=True) * v_ref[t, rows, :]
            outs.append((yn + bonus) * g_ref[t, rows, :])
        o_ref[t * bsz:(t + 1) * bsz, :] = jnp.concatenate(outs, axis=0).T.astype(o_ref.dtype)


def _wkv_decode(r, lw, k, v, a, b, g, rk, lnw, lnb, state, hb):
    n_tok, d, bsz = r.shape
    n_heads = d // HEAD_DIM
    assert n_heads % hb == 0
    blk = pl.BlockSpec((n_tok, hb * HEAD_DIM, bsz), lambda i: (0, i, 0))
    par = pl.BlockSpec((hb * HEAD_DIM, bsz), lambda i: (i, 0))
    st = pl.BlockSpec((hb, HEAD_DIM, HEAD_DIM, bsz), lambda i: (i, 0, 0, 0))
    return pl.pallas_call(
        functools.partial(_wkv_decode_kernel, n_tok, hb),
        out_shape=[jax.ShapeDtypeStruct((n_tok * bsz, d), BF16), jax.ShapeDtypeStruct(state.shape, F32)],
        grid=(n_heads // hb,),
        in_specs=[blk] * 7 + [par] * 3 + [st],
        out_specs=[pl.BlockSpec((n_tok * bsz, hb * HEAD_DIM), lambda i: (0, i)), st],
        scratch_shapes=[pltpu.VMEM((n_tok, hb * HEAD_DIM, bsz), F32)] * 2,
        compiler_params=_cparams(1),
        name="wkv_decode",
    )(r, lw, k, v, a, b, g, rk, lnw, lnb, state)


def _ffn_kernel(block_mode, tiles_per_seq, fc, final_norm, *refs):
    refs = list(refs)
    x_ref, o_ref, p_ref = refs[:3]
    pos = 3
    if block_mode:
        st_ref = refs[pos]
        pos += 1
    (wo_ref, gffn_ref, wup_ref, cw_ref, cb_ref, wd_ref, wg_ref, wp_ref, gout_ref) = refs[pos:pos + 9]
    pos += 9
    y_o, u_o = refs[pos:pos + 2]
    pos += 2
    x1_s, hn_s, act_s = refs[pos:pos + 3]
    pos += 3
    i = pl.program_id(0)
    dff = wd_ref.shape[0]
    n_f = dff // fc
    tm = x_ref.shape[0]
    if block_mode:
        c1_s, c2_s = refs[pos:pos + 2]

        @pl.when(i == 0)
        def _():
            c2_s[...] = st_ref[:, 0:2 * dff]
            c1_s[...] = st_ref[:, 2 * dff:4 * dff]
    else:
        c_s = refs[pos]

        @pl.when(i % tiles_per_seq == 0)
        def _():
            c_s[...] = jnp.zeros_like(c_s)

    x1 = x_ref[...] + jnp.dot(o_ref[...], wo_ref[...], preferred_element_type=F32)
    x1_s[...] = x1
    hn_s[...] = _rms(x1, gffn_ref[...]).astype(BF16)

    def up(f):
        hn = hn_s[...]
        return tuple(jnp.dot(hn, wup_ref[:, half * dff + f * fc:half * dff + (f + 1) * fc],
                             preferred_element_type=F32) for half in range(2))

    u_next = up(0)
    for f in range(n_f):
        u_cur = u_next
        if f + 1 < n_f:
            u_next = up(f + 1)
        conv = []
        for half in range(2):
            cols = slice(half * dff + f * fc, half * dff + (f + 1) * fc)
            u = u_cur[half]
            if block_mode:
                u1, u2 = c1_s[:, cols], c2_s[:, cols]
                c2_s[:, cols] = u1
                c1_s[:, cols] = u
                u_o[0, :, cols] = u
            else:
                carry = c_s[:, cols]
                u1, u2 = _prev_rows_carry(u, 1, carry), _prev_rows_carry(u, 2, carry)
                c_s[:, cols] = u[tm - SUBLANES:tm, :]
                u_o[0, :, cols] = u[tm - SUBLANES:tm, :]
            conv.append(cb_ref[:, cols] + cw_ref[0:1, cols] * u2 + cw_ref[1:2, cols] * u1 + cw_ref[2:3, cols] * u)
        gate, val = conv
        act_s[:, f * fc:(f + 1) * fc] = (gate * _sigmoid(gate) * val).astype(BF16)

    x2 = x1_s[...] + jnp.dot(act_s[...], wd_ref[...], preferred_element_type=F32)
    x3 = x2 + _sigmoid(_dot(x2, wg_ref[...])) * _dot(p_ref[...], wp_ref[...])
    if final_norm:
        x3 = _rms(x3, gout_ref[...])
    y_o[...] = x3


def _ffn(x, o, p, seq_len, conv_state, w, final_norm, tm, fc):
    n, d = x.shape
    dff = w['wd'].shape[0]
    pdim = p.shape[1]
    block_mode = conv_state is not None
    assert dff % fc == 0
    if block_mode:
        tm = n // seq_len
        tiles_per_seq = 1
    else:
        tm = min(tm, seq_len)
        assert seq_len % tm == 0 and n % seq_len == 0
        tiles_per_seq = seq_len // tm
    rowd = pl.BlockSpec((tm, d), lambda i: (i, 0))
    rowp = pl.BlockSpec((tm, pdim), lambda i: (i, 0))

    def resident(a):
        nd = a.ndim
        return pl.BlockSpec(a.shape, lambda i: (0,) * nd, pipeline_mode=pl.Buffered(1))

    in_specs = [rowd, rowd, rowp]
    args = [x, o, p]
    if block_mode:
        in_specs.append(resident(conv_state))
        args.append(conv_state)
    consts = [w['wo'], w['gffn'], w['wup'], w['cw'], w['cb'], w['wd'], w['wg'], w['wp'], w['gout']]
    in_specs += [resident(c) for c in consts]
    args += consts
    if block_mode:
        u_shape = jax.ShapeDtypeStruct((seq_len, tm, 2 * dff), F32)
        u_spec = pl.BlockSpec((1, tm, 2 * dff), lambda i: (i, 0, 0))
        scratch_c = [pltpu.VMEM((tm, 2 * dff), F32)] * 2
    else:
        u_shape = jax.ShapeDtypeStruct((n // tm, SUBLANES, 2 * dff), F32)
        u_spec = pl.BlockSpec((1, SUBLANES, 2 * dff), lambda i: (i, 0, 0))
        scratch_c = [pltpu.VMEM((SUBLANES, 2 * dff), F32)]
    return pl.pallas_call(
        functools.partial(_ffn_kernel, block_mode, tiles_per_seq, fc, final_norm),
        out_shape=[jax.ShapeDtypeStruct((n, d), F32), u_shape],
        grid=(n // tm,),
        in_specs=in_specs,
        out_specs=[rowd, u_spec],
        scratch_shapes=[pltpu.VMEM((tm, d), F32), pltpu.VMEM((tm, d), BF16), pltpu.VMEM((tm, dff), BF16)] + scratch_c,
        compiler_params=_cparams(1),
        name="ffn",
    )(*args)


def _qkv_kernel(prompt_mode, *refs):
    if prompt_mode:
        (x_ref, g_ref, w_ref, cos_ref, s1_ref, s2_ref, q_o, kt_o, vt_o, k16_o, vt16_o) = refs
    else:
        (x_ref, g_ref, w_ref, cos_ref, s1_ref, s2_ref, q_o, kt_o, vt_o, k_o, v_o) = refs
    hn = _rms(x_ref[...], g_ref[...]).astype(BF16)
    d = hn.shape[1]
    reps = d // LANES
    cos = jnp.concatenate([cos_ref[...]] * reps, axis=1)
    s1 = jnp.concatenate([s1_ref[...]] * reps, axis=1)
    s2 = jnp.concatenate([s2_ref[...]] * reps, axis=1)
    half = ROT_DIM // 2

    def rope(z):
        return z * cos + pltpu.roll(z, d - half, axis=1) * s1 + pltpu.roll(z, half, axis=1) * s2

    q = rope(jnp.dot(hn, w_ref[:, 0:d], preferred_element_type=F32)) * (HEAD_DIM ** -0.5)
    k = rope(jnp.dot(hn, w_ref[:, d:2 * d], preferred_element_type=F32))
    v = jnp.dot(hn, w_ref[:, 2 * d:3 * d], preferred_element_type=F32)
    vt = v.T
    kt_o[0] = k.T
    vt_o[0] = vt
    if prompt_mode:
        q_o[...] = (q * LOG2E).astype(BF16)
        k16_o[...] = k.astype(BF16)
        vt16_o[0] = vt.astype(BF16)
    else:
        q_o[...] = q
        k_o[...] = k
        v_o[...] = v


def _qkv(x, group, gmix, wqkv, cos, s1, s2, prompt_mode, tm):
    n, d = x.shape
    tm = min(tm, group)
    assert group % tm == 0 and n % group == 0
    tpg = group // tm
    row = pl.BlockSpec((tm, d), lambda i: (i, 0))
    rowt = pl.BlockSpec((tm, LANES), lambda i: (i, 0))
    tr = pl.BlockSpec((1, d, tm), lambda i: (i // tpg, 0, i % tpg))
    big = jax.ShapeDtypeStruct((n, d), F32)
    big16 = jax.ShapeDtypeStruct((n, d), BF16)
    bigt = jax.ShapeDtypeStruct((n // group, d, group), F32)
    if prompt_mode:
        out_shape = [big16, bigt, bigt, big16, jax.ShapeDtypeStruct((n // group, d, group), BF16)]
        out_specs = [row, tr, tr, row, tr]
    else:
        out_shape = [big, bigt, bigt, big, big]
        out_specs = [row, tr, tr, row, row]
    return pl.pallas_call(
        functools.partial(_qkv_kernel, prompt_mode),
        out_shape=out_shape,
        grid=(n // tm,),
        in_specs=[row, _const_spec(gmix.shape), _const_spec(wqkv.shape), rowt, rowt, rowt],
        out_specs=out_specs,
        compiler_params=_cparams(1),
        name="qkv",
    )(x, gmix, wqkv, cos, s1, s2)


def _moba_kernel(n_blocks, grp, q_ref, k_ref, vt_ref, o_ref, o_s, m_s, l_s, g_s, w_s):
    qi = pl.program_id(2)
    blk = MOBA_BLOCK
    sub = min(2, grp)
    q = q_ref[0]
    lane = lax.broadcasted_iota(jnp.int32, (1, LANES), 1)
    zero16 = jnp.zeros_like(q)
    qh = [jnp.where(lane < HEAD_DIM, q, zero16), jnp.where(lane >= HEAD_DIM, q, zero16)]
    ones_rows = jnp.where(lax.broadcasted_iota(jnp.int32, (2 * SUBLANES, blk), 0) == 0, 1.0, 0.0).astype(BF16)
    neg_inf = jnp.float32(-jnp.inf)
    for h in range(2):
        m_s[h] = jnp.full((n_blocks, blk), NEG_BIG, F32)
        l_s[h] = jnp.zeros((n_blocks, blk), F32)
        g_s[h] = jnp.full((n_blocks, blk), neg_inf, F32)

    def scores(j0, nb):
        k0 = pl.multiple_of(j0 * blk, blk)
        kg = k_ref[0, pl.ds(k0, nb * blk), :]
        return [lax.dot_general(kg, qh[h], (((1,), (1,)), ((), ())), preferred_element_type=F32)
                for h in range(2)]

    def partials(j0, nb, st, diag):
        parts = []
        for h in range(2):
            for jb in range(nb):
                s = st[h][jb * blk:(jb + 1) * blk]
                gate = jnp.sum(s, axis=0, keepdims=True) * (1.0 / blk)
                s = s.astype(BF16)
                if diag:
                    kr = lax.broadcasted_iota(jnp.int32, (blk, blk), 0)
                    qc = lax.broadcasted_iota(jnp.int32, (blk, blk), 1)
                    s = jnp.where(kr <= qc, s, jnp.full_like(s, NEG_BIG))
                mj = jnp.max(s, axis=0, keepdims=True)
                p = jnp.exp2(s - mj)
                parts.append((h, jb, gate, mj.astype(F32), p))
        pvs = []
        for h, jb, gate, mj, p in parts:
            kb0 = pl.multiple_of((j0 + jb) * blk, blk)
            vt_ext = jnp.concatenate([vt_ref[0, h * HEAD_DIM:(h + 1) * HEAD_DIM, pl.ds(kb0, blk)], ones_rows],
                                     axis=0)
            pvs.append(jnp.dot(vt_ext, p, preferred_element_type=F32))
        for (h, jb, gate, mj, p), pv in zip(parts, pvs):
            j = j0 + jb
            o_s[h * n_blocks + j] = pv[:HEAD_DIM]
            l_s[h, pl.ds(j, 1), :] = pv[HEAD_DIM:HEAD_DIM + 1]
            m_s[h, pl.ds(j, 1), :] = mj
            g_s[h, pl.ds(j, 1), :] = gate

    def group_body(gi, carry):
        j0 = gi * grp
        st = scores(j0, sub)
        for s in range(1, grp // sub):
            st_next = scores(j0 + s * sub, sub)
            partials(j0 + (s - 1) * sub, sub, st, False)
            st = st_next
        partials(j0 + grp - sub, sub, st, False)
        return carry

    lax.fori_loop(0, (qi + grp - 1) // grp, group_body, 0)
    partials(qi, 1, scores(qi, 1), True)

    jrow = lax.broadcasted_iota(jnp.int32, (n_blocks, 1), 0)
    jrow_f = jrow.astype(F32)
    lsum = []
    for h in range(2):
        cur = jnp.where(jrow < qi, g_s[h], neg_inf)
        sel = jrow == qi
        for _ in range(MOBA_TOPK):
            mx = jnp.max(cur, axis=0, keepdims=True)
            first = jnp.min(jnp.where(cur == mx, jrow_f, float(n_blocks)), axis=0, keepdims=True)
            onehot = jrow_f == first
            sel = jnp.logical_or(sel, jnp.logical_and(onehot, mx > neg_inf))
            cur = jnp.where(onehot, neg_inf, cur)
        m = m_s[h]
        mx = jnp.max(jnp.where(sel, m, NEG_BIG), axis=0, keepdims=True)
        w = jnp.where(sel, jnp.exp2(m - mx), 0.0)
        w_s[h] = w
        lsum.append(jnp.sum(w * l_s[h], axis=0, keepdims=True))

    def merge(j, accs):
        return tuple(accs[h] + w_s[h, pl.ds(j, 1), :] * o_s[h * n_blocks + j] for h in range(2))

    accs = lax.fori_loop(0, qi + 1, merge, tuple(jnp.zeros((HEAD_DIM, blk), F32) for _ in range(2)))
    o_ref[0] = jnp.concatenate([accs[h] / lsum[h] for h in range(2)], axis=0).T.astype(o_ref.dtype)


def _moba_prompt(q16, k16, vt16, grp):
    bsz, t, d = q16.shape
    n_blocks = t // MOBA_BLOCK
    assert n_blocks % grp == 0
    stat = pltpu.VMEM((2, n_blocks, MOBA_BLOCK), F32)
    return pl.pallas_call(
        functools.partial(_moba_kernel, n_blocks, grp),
        out_shape=jax.ShapeDtypeStruct((bsz, t, d), BF16),
        grid=(bsz, d // LANES, n_blocks),
        in_specs=[pl.BlockSpec((1, MOBA_BLOCK, LANES), lambda b, p, i: (b, i, p)),
                  pl.BlockSpec((1, t, LANES), lambda b, p, i: (b, 0, p)),
                  pl.BlockSpec((1, LANES, t), lambda b, p, i: (b, p, 0))],
        out_specs=pl.BlockSpec((1, MOBA_BLOCK, LANES), lambda b, p, i: (b, i, p)),
        scratch_shapes=[pltpu.VMEM((2 * n_blocks, HEAD_DIM, MOBA_BLOCK), F32), stat, stat, stat, stat],
        compiler_params=_cparams(3),
        name="moba_prompt",
    )(q16, k16, vt16)


def _moba_decode_kernel(n_tok, n_heads, pages_per_step, n_steps, pt_ref, q_ref, kn_ref, vn_ref, *refs):
    kp = refs[:pages_per_step]
    vp = refs[pages_per_step:2 * pages_per_step]
    o_ref = refs[2 * pages_per_step]
    m_s, l_s, g_s, acc_s = refs[2 * pages_per_step + 1:]
    step = pl.program_id(1)
    d = q_ref.shape[2]
    page = kp[0].shape[2]
    pages_per_blk = MOBA_BLOCK // page
    blk_per_step = pages_per_step // pages_per_blk
    n_past = n_steps * blk_per_step
    nrow = n_tok * n_heads
    q = q_ref[0]
    lane_head = lax.broadcasted_iota(jnp.int32, (n_heads, d), 1) // HEAD_DIM
    row_head = lax.broadcasted_iota(jnp.int32, (n_heads, d), 0)
    hmask = lane_head == row_head
    qbd = jnp.concatenate(
        [jnp.where(hmask, jnp.broadcast_to(q[t:t + 1, :], (n_heads, d)), 0.0) for t in range(n_tok)], axis=0)
    qbd16 = qbd.astype(BF16)

    scores = [jnp.concatenate([_dot(qbd16, kp[jb * pages_per_blk + i][0]) for i in range(pages_per_blk)], axis=1)
              for jb in range(blk_per_step)]
    stats = []
    for s in scores:
        gate = jnp.sum(s, axis=1, keepdims=True) * (1.0 / MOBA_BLOCK)
        mj = jnp.max(s, axis=1, keepdims=True)
        p = jnp.exp(s - mj)
        stats.append((gate, mj, jnp.sum(p, axis=1, keepdims=True), p.astype(BF16)))
    outs = [sum(_dot_nt(p[:, i * page:(i + 1) * page], vp[jb * pages_per_blk + i][0]) for i in range(pages_per_blk))
            for jb, (_, _, _, p) in enumerate(stats)]
    for jb, ((gate, mj, lj, _), oj) in enumerate(zip(stats, outs)):
        slot = step * blk_per_step + jb
        m_s[slot] = jnp.broadcast_to(mj, (nrow, LANES))
        l_s[slot] = jnp.broadcast_to(lj, (nrow, LANES))
        g_s[slot] = jnp.broadcast_to(gate, (nrow, LANES))
        acc_s[slot] = oj

    @pl.when(step == n_steps - 1)
    def _():
        gates = [g_s[j][:, 0:1] for j in range(n_past)]
        kn = kn_ref[0]
        vn = vn_ref[0]
        trow = lax.broadcasted_iota(jnp.int32, (nrow, 1), 0) // n_heads
        s_own = [jnp.where(trow >= t, jnp.sum(qbd * kn[t:t + 1, :], axis=1, keepdims=True), NEG_BIG)
                 for t in range(n_tok)]
        m = functools.reduce(jnp.maximum, s_own)
        p_own = [jnp.exp(s - m) for s in s_own]
        l = sum(p_own)
        acc = sum(p_own[t] * vn[t:t + 1, :] for t in range(n_tok))
        for j in range(n_past):
            rank = sum(jnp.where(jnp.logical_or(gates[i] > gates[j],
                                                jnp.logical_and(gates[i] == gates[j], i < j)), 1.0, 0.0)
                       for i in range(n_past) if i != j)
            selj = rank < float(MOBA_TOPK)
            mj = m_s[j][:, 0:1]
            m_new = jnp.where(selj, jnp.maximum(m, mj), m)
            alpha = jnp.exp(m - m_new)
            beta = jnp.where(selj, jnp.exp(mj - m_new), 0.0)
            l = l * alpha + l_s[j][:, 0:1] * beta
            acc = acc * alpha + acc_s[j] * beta
            m = m_new
        lane_head_r = lax.broadcasted_iota(jnp.int32, (nrow, d), 1) // HEAD_DIM
        row_head_r = lax.broadcasted_iota(jnp.int32, (nrow, d), 0) % n_heads
        out = jnp.where(lane_head_r == row_head_r, acc / l, 0.0)
        o_ref[0] = jnp.concatenate(
            [jnp.sum(out[t * n_heads:(t + 1) * n_heads], axis=0, keepdims=True) for t in range(n_tok)],
            axis=0).astype(o_ref.dtype)


def _moba_decode(q, kn, vn, cache_kt, cache_vt, page_table, pages_per_step):
    bsz, n_tok, d = q.shape
    n_heads = d // HEAD_DIM
    pool, _, page = cache_kt.shape
    n_pages = page_table.shape[1]
    assert n_pages % pages_per_step == 0 and MOBA_BLOCK % page == 0
    assert (n_pages * page) % MOBA_BLOCK == 0 and n_tok <= MOBA_BLOCK
    assert pages_per_step % (MOBA_BLOCK // page) == 0
    n_steps = n_pages // pages_per_step
    n_past = n_pages * page // MOBA_BLOCK
    nrow = n_tok * n_heads
    tok = pl.BlockSpec((1, n_tok, d), lambda b, s, pt: (b, 0, 0))

    def page_spec(i):
        return pl.BlockSpec((1, d, page), lambda b, s, pt: (pt[b, s * pages_per_step + i], 0, 0))

    grid_spec = pltpu.PrefetchScalarGridSpec(
        num_scalar_prefetch=1,
        grid=(bsz, n_steps),
        in_specs=[tok, tok, tok] + [page_spec(i) for i in range(pages_per_step)] * 2,
        out_specs=tok,
        scratch_shapes=[pltpu.VMEM((n_past, nrow, LANES), F32)] * 3 + [pltpu.VMEM((n_past, nrow, d), F32)],
    )
    return pl.pallas_call(
        functools.partial(_moba_decode_kernel, n_tok, n_heads, pages_per_step, n_steps),
        out_shape=jax.ShapeDtypeStruct((bsz, n_tok, d), BF16),
        grid_spec=grid_spec,
        compiler_params=_cparams(2),
        name="moba_decode",
    )(page_table, q, kn, vn, *([cache_kt] * pages_per_step), *([cache_vt] * pages_per_step))


def _pad_cols(w, n):
    return jnp.pad(w, ((0, 0), (0, n - w.shape[1])))


def _pad_rows(w, n):
    return jnp.pad(w, ((0, n - w.shape[0]), (0, 0)))


def _rope_tables(pos):
    half = ROT_DIM // 2
    inv = ROPE_THETA ** (-2.0 * jnp.arange(half, dtype=F32) / ROT_DIM)
    ang = pos.astype(F32)[:, None] * inv[None, :]
    cos = jnp.cos(ang)
    sin = jnp.sin(ang)
    t = pos.shape[0]
    ones = jnp.ones((t, HEAD_DIM - ROT_DIM), F32)
    zeros_r = jnp.zeros((t, HEAD_DIM - ROT_DIM), F32)
    zeros_h = jnp.zeros((t, half), F32)
    c = jnp.concatenate([cos, cos, ones], axis=1)
    s1 = jnp.concatenate([-sin, zeros_h, zeros_r], axis=1)
    s2 = jnp.concatenate([zeros_h, sin, zeros_r], axis=1)
    rep = LANES // HEAD_DIM
    return jnp.tile(c, (1, rep)), jnp.tile(s1, (1, rep)), jnp.tile(s2, (1, rep))


def _unpack_state(s):
    b, hp, dv, dk2 = s.shape
    return s.reshape(b, hp, dv, 2, dk2 // 2).transpose(0, 1, 3, 2, 4).reshape(b, hp * 2, dv, dk2 // 2)


def kernel(x_prompt, x_sample, state_wkv, state_shift, cache_k, cache_v, state_conv, page_table, p_prompt, p_sample, rw_mix, rw_rkv, rw_w0, rw_w1, rw_w2, rw_a0, rw_a1, rw_a2, rw_g1, rw_g2, rw_kk, rw_ka, rw_rk, rw_lnw, rw_lnb, rw_wo, mb_wqkv, mb_wo, norm_mix, norm_ffn, ff_wup, ff_conv_w, ff_conv_b, ff_wdown, ple_wp, ple_wg, norm_out):
    bp, tp, d = x_prompt.shape
    bs, ts, _ = x_sample.shape
    depth = norm_mix.shape[0]
    n_heads = d // HEAD_DIM
    dff = ff_wdown.shape[1]
    pool, page = cache_k.shape[1], cache_k.shape[2]
    past_len = page_table.shape[1] * page
    assert d % LANES == 0 and depth == 2 and ts >= 2

    lane_head = jnp.arange(d) // HEAD_DIM
    segr = (lane_head[:, None] == jnp.arange(LANES)[None, :]).astype(BF16)
    segb = segr.T
    seg2 = ((jnp.arange(LANES) // HEAD_DIM)[:, None] == (jnp.arange(LANES) // HEAD_DIM)[None, :]).astype(BF16)

    def row(v):
        return v.reshape(1, -1).astype(F32)

    def col(v):
        return jnp.broadcast_to(v.reshape(-1, 1).astype(F32), (v.size, bs))

    rw = dict(
        gmix=row(norm_mix[0]), mix=rw_mix[0], wrkv=rw_rkv[0].astype(BF16), w0=row(rw_w0[0]),
        w1=_pad_cols(rw_w1[0], LANES).astype(BF16), w2=_pad_rows(rw_w2[0], LANES).astype(BF16),
        a0=row(rw_a0[0]), a1=_pad_cols(rw_a1[0], LANES).astype(BF16), a2=_pad_rows(rw_a2[0], LANES).astype(BF16),
        g1=_pad_cols(rw_g1[0], 2 * LANES).astype(BF16), g2=_pad_rows(rw_g2[0], 2 * LANES).astype(BF16),
        kk=row(rw_kk[0]), ka=row(rw_ka[0]), segr=segr, segb=segb,
        rk=row(rw_rk[0]), lnw=row(rw_lnw[0]), lnb=row(rw_lnb[0]), seg2=seg2)

    def ffw(i, wo):
        return dict(wo=wo.astype(BF16), gffn=row(norm_ffn[i]), wup=ff_wup[i].astype(BF16), cw=ff_conv_w[i],
                    cb=row(ff_conv_b[i]), wd=ff_wdown[i].astype(BF16), wg=ple_wg[i].astype(BF16),
                    wp=ple_wp[i].astype(BF16), gout=row(norm_out))

    fw0 = ffw(0, rw_wo[0])
    fw1 = ffw(1, mb_wo[0])
    wqkv = mb_wqkv[0].astype(BF16)
    gmix1 = row(norm_mix[1])

    xp = x_prompt.reshape(bp * tp, d)
    pp = p_prompt.reshape(depth, bp * tp, -1)
    xs = x_sample.transpose(1, 0, 2).reshape(ts * bs, d)
    ps = p_sample.transpose(0, 2, 1, 3).reshape(depth, ts * bs, -1)

    def conv_state_p(u):
        return u.reshape(bp, -1, SUBLANES, 2 * dff)[:, -1, SUBLANES - 2:]

    def conv_state_s(u):
        return u[ts - 2:].transpose(1, 0, 2)

    r, lw, k, v, a, b, g, shift_p = _rwkv_proj(xp, tp, None, rw)
    sh = lambda z: z.reshape(bp, tp, d)
    o_p, wkv_p = _wkv(sh(r), sh(lw), sh(k), sh(v), sh(a), sh(b), sh(g), rw,
                      chunk=64, tblk=min(512, tp), bb=bp, npair=4)
    xp, up_ = _ffn(xp, o_p.reshape(bp * tp, d), pp[0], tp, None, fw0, False, 512, 256)
    conv_p0 = conv_state_p(up_)

    r, lw, k, v, a, b, g, shift_s = _rwkv_proj(xs, ts, state_shift[0], rw)
    st_in = jnp.transpose(state_wkv[0], (1, 2, 3, 0))
    o_s, st_out = _wkv_decode(r, lw, k, v, a, b, g, col(rw_rk[0]), col(rw_lnw[0]), col(rw_lnb[0]), st_in, 2)
    wkv_s = jnp.transpose(st_out, (3, 0, 1, 2))
    xs, us_ = _ffn(xs, o_s, ps[0], ts, state_conv[0].reshape(bs, 4 * dff), fw0, False, 512, 256)
    conv_s0 = conv_state_s(us_)

    cos, s1, s2 = _rope_tables(jnp.arange(tp, dtype=jnp.int32))
    cos_p, s1_p, s2_p = (jnp.tile(z, (bp, 1)) for z in (cos, s1, s2))
    q16, kt, vt, k16, vt16 = _qkv(xp, tp, gmix1, wqkv, cos_p, s1_p, s2_p, True, 256)
    k_p = kt.reshape(bp, n_heads, HEAD_DIM, tp).transpose(0, 3, 1, 2)
    v_p = vt.reshape(bp, n_heads, HEAD_DIM, tp).transpose(0, 3, 1, 2)
    o_p = _moba_prompt(q16.reshape(bp, tp, d), k16.reshape(bp, tp, d), vt16, min(8, tp // MOBA_BLOCK))
    xp, up_ = _ffn(xp, o_p.reshape(bp * tp, d), pp[1], tp, None, fw1, True, 512, 256)
    conv_p1 = conv_state_p(up_)

    cos, s1, s2 = _rope_tables(past_len + jnp.arange(ts, dtype=jnp.int32))
    cos_s, s1_s, s2_s = (jnp.repeat(z, bs, axis=0) for z in (cos, s1, s2))
    q, kt, vt, k, v = _qkv(xs, bs, gmix1, wqkv, cos_s, s1_s, s2_s, False, 128)
    k_s = kt.reshape(ts, n_heads, HEAD_DIM, bs).transpose(3, 0, 1, 2)
    v_s = vt.reshape(ts, n_heads, HEAD_DIM, bs).transpose(3, 0, 1, 2)
    by_seq = lambda z: z.reshape(ts, bs, d).transpose(1, 0, 2)
    cache_kt = jnp.transpose(cache_k[0], (0, 2, 3, 1)).reshape(pool, d, page)
    cache_vt = jnp.transpose(cache_v[0], (0, 2, 3, 1)).reshape(pool, d, page)
    o_s = _moba_decode(by_seq(q), by_seq(k), by_seq(v), cache_kt, cache_vt, page_table,
                       min(8, page_table.shape[1]))
    o_s = o_s.transpose(1, 0, 2).reshape(ts * bs, d)
    xs, us_ = _ffn(xs, o_s, ps[1], ts, state_conv[1].reshape(bs, 4 * dff), fw1, True, 512, 256)
    conv_s1 = conv_state_s(us_)

    return (xp.reshape(bp, tp, d), by_seq(xs),
            _unpack_state(wkv_p)[None], shift_p.reshape(1, bp, d), k_p[None], v_p[None],
            jnp.stack([conv_p0, conv_p1]),
            wkv_s[None], shift_s[None], k_s[None], v_s[None],
            jnp.stack([conv_s0, conv_s1]))
```

```python
import functools

import jax
import jax.numpy as jnp
from jax import lax
from jax.experimental import pallas as pl
from jax.experimental.pallas import tpu as pltpu

F32 = jnp.float32
BF16 = jnp.bfloat16

HEAD_DIM = 64
GN_EPS = 64e-5
RMS_EPS = 1e-6
MOBA_BLOCK = 256
MOBA_TOPK = 3
ROPE_THETA = 500000.0
ROT_DIM = HEAD_DIM // 4
NEG_BIG = -1e30
LOG2E = 1.4426950408889634
LANES = 128
SUBLANES = 8
VMEM_LIMIT_BYTES = 56 * 1024 * 1024


def _cparams(n_axes):
    return pltpu.CompilerParams(dimension_semantics=("arbitrary",) * n_axes,
                                vmem_limit_bytes=VMEM_LIMIT_BYTES)


def _const_spec(shape):
    nd = len(shape)
    return pl.BlockSpec(shape, lambda *_: (0,) * nd)


def _dot(a, b):
    return jnp.dot(a.astype(BF16), b.astype(BF16), preferred_element_type=F32)


def _dot_nt(a, b):
    return lax.dot_general(a.astype(BF16), b.astype(BF16), (((1,), (1,)), ((), ())),
                           preferred_element_type=F32)


def _dot_tn(a, b):
    return lax.dot_general(a.astype(BF16), b.astype(BF16), (((0,), (0,)), ((), ())),
                           preferred_element_type=F32)


def _split_dot(x, m):
    hi = x.astype(BF16)
    lo = (x - hi.astype(F32)).astype(BF16)
    return (jnp.dot(hi, m, preferred_element_type=F32) + jnp.dot(lo, m, preferred_element_type=F32))


def _sigmoid(z):
    return 1.0 / (1.0 + jnp.exp(-z))


def _rms(x, g):
    return x * lax.rsqrt(jnp.mean(x * x, axis=-1, keepdims=True) + RMS_EPS) * g


def _seg_sum(x, segr, segb):
    return _split_dot(_split_dot(x, segr), segb)


def _prev_rows_carry(u, k, carry):
    rolled = pltpu.roll(u, k, axis=0)
    row = lax.broadcasted_iota(jnp.int32, (u.shape[0], 1), 0)
    out = rolled
    for j in range(k):
        out = jnp.where(row == j, carry[SUBLANES - k + j:SUBLANES - k + j + 1, :], out)
    return out


def _rwkv_proj_kernel(block_mode, tiles_per_seq, *refs):
    if block_mode:
        (x_ref, init_ref, gmix_ref, mix_ref, wrkv_ref, w0_ref, w1_ref, w2_ref, a0_ref, a1_ref,
         a2_ref, g1_ref, g2_ref, kk_ref, ka_ref, segr_ref, segb_ref,
         r_o, lw_o, k_o, v_o, a_o, b_o, g_o, xn_o, carry) = refs
    else:
        (x_ref, gmix_ref, mix_ref, wrkv_ref, w0_ref, w1_ref, w2_ref, a0_ref, a1_ref, a2_ref,
         g1_ref, g2_ref, kk_ref, ka_ref, segr_ref, segb_ref,
         r_o, lw_o, k_o, v_o, a_o, b_o, g_o, xn_o, carry) = refs
    xn = _rms(x_ref[...], gmix_ref[...])
    tm = xn.shape[0]
    if block_mode:
        @pl.when(pl.program_id(0) == 0)
        def _():
            carry[...] = init_ref[...]
        xprev = carry[...]
        carry[...] = xn
        xn_o[...] = xn
    else:
        @pl.when(pl.program_id(0) % tiles_per_seq == 0)
        def _():
            carry[...] = jnp.zeros_like(carry)
        xprev = _prev_rows_carry(xn, 1, carry)
        carry[...] = xn[tm - SUBLANES:tm, :]
        xn_o[0] = xn[tm - 1:tm, :]
    dx = xprev - xn

    def xm(i):
        return (xn + dx * mix_ref[i:i + 1, :]).astype(BF16)

    r = jnp.dot(xm(0), wrkv_ref[0], preferred_element_type=F32)
    k = jnp.dot(xm(1), wrkv_ref[1], preferred_element_type=F32)
    v = jnp.dot(xm(2), wrkv_ref[2], preferred_element_type=F32)
    wl = w0_ref[...] + _dot(jnp.tanh(jnp.dot(xm(3), w1_ref[...], preferred_element_type=F32)), w2_ref[...])
    sp = jnp.maximum(-wl, 0.0) + jnp.log(1.0 + jnp.exp(-jnp.abs(wl)))
    lw = -jnp.exp(-sp - 0.5)
    a = _sigmoid(a0_ref[...] + _dot(jnp.dot(xm(4), a1_ref[...], preferred_element_type=F32), a2_ref[...]))
    g = _dot(_sigmoid(jnp.dot(xm(5), g1_ref[...], preferred_element_type=F32)), g2_ref[...])
    kk = k * kk_ref[...]
    n2 = _seg_sum(kk * kk, segr_ref[...], segb_ref[...])
    kk = kk / jnp.maximum(jnp.sqrt(n2), 1e-12)
    outs = (r, lw, k * (1.0 + (a - 1.0) * ka_ref[...]), v, -kk, kk * a, g)
    for o_ref, val in zip((r_o, lw_o, k_o, v_o, a_o, b_o, g_o), outs):
        if block_mode:
            o_ref[0] = val.T
        else:
            o_ref[...] = val


def _rwkv_proj(x, seq_len, init, w):
    n, d = x.shape
    block_mode = init is not None
    consts = [w['gmix'], w['mix'], w['wrkv'], w['w0'], w['w1'], w['w2'], w['a0'], w['a1'], w['a2'],
              w['g1'], w['g2'], w['kk'], w['ka'], w['segr'], w['segb']]
    if block_mode:
        tm = n // seq_len
        tiles_per_seq = 1
        row = pl.BlockSpec((tm, d), lambda i: (i, 0))
        in_specs = [row, _const_spec(init.shape)] + [_const_spec(c.shape) for c in consts]
        args = [x, init] + consts
        out_shape = [jax.ShapeDtypeStruct((seq_len, d, tm), F32)] * 7 + [jax.ShapeDtypeStruct((tm, d), F32)]
        out_specs = [pl.BlockSpec((1, d, tm), lambda i: (i, 0, 0))] * 7 + [_const_spec((tm, d))]
        scratch = [pltpu.VMEM((tm, d), F32)]
    else:
        tm = min(256, seq_len)
        assert seq_len % tm == 0 and n % seq_len == 0
        tiles_per_seq = seq_len // tm
        row = pl.BlockSpec((tm, d), lambda i: (i, 0))
        in_specs = [row] + [_const_spec(c.shape) for c in consts]
        args = [x] + consts
        out_shape = [jax.ShapeDtypeStruct((n, d), F32)] * 7 + [jax.ShapeDtypeStruct((n // seq_len, 1, d), F32)]
        out_specs = [row] * 7 + [pl.BlockSpec((1, 1, d), lambda i: (i // tiles_per_seq, 0, 0))]
        scratch = [pltpu.VMEM((SUBLANES, d), F32)]
    return pl.pallas_call(
        functools.partial(_rwkv_proj_kernel, block_mode, tiles_per_seq),
        out_shape=out_shape,
        grid=(n // tm,),
        in_specs=in_specs,
        out_specs=out_specs,
        scratch_shapes=scratch,
        compiler_params=_cparams(1),
        name="rwkv_proj",
    )(*args)


def _wkv_masks(c):
    n = 2 * c
    row = lax.broadcasted_iota(jnp.int32, (n, n), 0)
    col = lax.broadcasted_iota(jnp.int32, (n, n), 1)
    rq = row >= c
    cq = col >= c
    tl = jnp.logical_and(jnp.logical_not(rq), jnp.logical_not(cq))
    br = jnp.logical_and(rq, cq)
    tr = jnp.logical_and(jnp.logical_not(rq), cq)
    bl = jnp.logical_and(rq, jnp.logical_not(cq))
    strict = (col % c) < (row % c)
    incl = (col % c) <= (row % c)
    one = jnp.ones((n, n), F32)
    zero = jnp.zeros((n, n), F32)

    def f(m):
        return jnp.where(m, one, zero)

    return dict(
        ab0=f(jnp.logical_and(strict, tl)), ab1=f(jnp.logical_and(strict, br)),
        ak0=f(jnp.logical_and(strict, tr)), ak1=f(jnp.logical_and(strict, bl)),
        rb0=f(jnp.logical_and(incl, bl)), rb1=f(jnp.logical_and(incl, tr)),
        rk0=f(jnp.logical_and(incl, br)), rk1=f(jnp.logical_and(incl, tl)),
        eye=f(row == col),
    )


def _wkv_chunk(c, n_rounds, units, msk, m0, m1, bdmask):
    nu = range(len(units))
    pre = []
    row = lax.broadcasted_iota(jnp.int32, (c, 1), 0)
    for r, lw, k, v, a, b, state in units:
        g_inc = lw
        shift = 1
        while shift < c:
            g_inc = g_inc + jnp.where(row >= shift, pltpu.roll(g_inc, shift, axis=0), 0.0)
            shift *= 2
        e_inc = jnp.exp(g_inc)
        e_neg = jnp.exp(-g_inc)
        at = a * jnp.exp(g_inc - lw)
        rt = r * e_inc
        bt = b * e_neg
        kt = k * e_neg
        pre.append(dict(
            lr=jnp.concatenate([at, rt], axis=0), rl=jnp.concatenate([rt, at], axis=0),
            bk=jnp.concatenate([bt, kt], axis=0), kb=jnp.concatenate([kt, bt], axis=0),
            vst=jnp.concatenate([v * m1, v * m0], axis=0), v=v, state=state, gc=e_inc[c - 1:c, :]))
    sc0 = [_dot_nt(p['lr'] * m0, p['bk']) for p in pre]
    sc1 = [_dot_nt(p['rl'] * m1, p['kb']) for p in pre]
    xs = [_dot_nt(p['lr'], p['state']) for p in pre]
    ab = [sc0[i] * msk['ab0'] + sc1[i] * msk['ab1'] for i in nu]
    ak = [sc0[i] * msk['ak0'] + sc1[i] * msk['ak1'] for i in nu]
    rbk = [jnp.concatenate([sc0[i] * msk['rb0'] + sc1[i] * msk['rb1'],
                            sc0[i] * msk['rk0'] + sc1[i] * msk['rk1']], axis=1) for i in nu]
    akv = [_dot(ak[i], pre[i]['vst']) for i in nu]
    tinv = [msk['eye'] + ab[i] for i in nu]
    pw = ab
    for _ in range(n_rounds):
        pw = [_dot(pw[i], pw[i]) for i in nu]
        tinv = [tinv[i] + _dot(tinv[i], pw[i]) for i in nu]
    zst = [jnp.concatenate([xs[i][:c] * m0, xs[i][:c] * m1], axis=0) + akv[i] for i in nu]
    w = [_dot(tinv[i], zst[i]) for i in nu]
    ys = [_dot(rbk[i], jnp.concatenate([w[i], pre[i]['vst']], axis=0)) for i in nu]
    ds = [_dot_tn(jnp.concatenate([w[i][:c] + w[i][c:], pre[i]['v']], axis=0), pre[i]['bk']) for i in nu]
    outs = []
    for i in nu:
        y = xs[i][c:] + ys[i][:c] + ys[i][c:]
        outs.append((y, (pre[i]['state'] + ds[i] * bdmask) * pre[i]['gc']))
    return outs


def _wkv_kernel(c, n_chunks, bb, npair, r_ref, lw_ref, k_ref, v_ref, a_ref, b_ref, g_ref, rk_ref, lnw_ref,
                lnb_ref, seg_ref, o_ref, s_out, state, ybuf):
    tstep = pl.program_id(2)
    lane = lax.broadcasted_iota(jnp.int32, (1, LANES), 1)
    m0 = jnp.where(lane < HEAD_DIM, 1.0, 0.0).astype(F32)
    m1 = 1.0 - m0
    rowi = lax.broadcasted_iota(jnp.int32, (LANES, LANES), 0)
    coli = lax.broadcasted_iota(jnp.int32, (LANES, LANES), 1)
    bdmask = jnp.where((rowi >= HEAD_DIM) == (coli >= HEAD_DIM), 1.0, 0.0).astype(F32)
    half = LANES // 2

    @pl.when(tstep == 0)
    def _():
        state[...] = jnp.zeros_like(state)

    msk = _wkv_masks(c)
    n_rounds = max(c.bit_length() - 2, 0)

    def chunk_body(ci, carry):
        t0 = pl.multiple_of(ci * c, c)
        rows = pl.ds(t0, c)
        units = [(ib, ip) for ib in range(bb) for ip in range(npair)]
        ins = []
        for ib, ip in units:
            ls = slice(ip * LANES, (ip + 1) * LANES)
            ins.append((r_ref[ib, rows, ls], lw_ref[ib, rows, ls], k_ref[ib, rows, ls],
                        v_ref[ib, rows, ls], a_ref[ib, rows, ls], b_ref[ib, rows, ls],
                        state[ib * npair + ip]))
        outs = _wkv_chunk(c, n_rounds, ins, msk, m0, m1, bdmask)
        for (ib, ip), (y, ns) in zip(units, outs):
            state[ib * npair + ip] = ns
            ybuf[ib, rows, slice(ip * LANES, (ip + 1) * LANES)] = y
        return carry

    lax.fori_loop(0, n_chunks, chunk_body, 0)

    seg = seg_ref[...]
    for ib in range(bb):
        for ip in range(npair):
            ls = slice(ip * LANES, (ip + 1) * LANES)
            y = ybuf[ib, :, ls]
            mu = _split_dot(y, seg) * (1.0 / HEAD_DIM)
            yc = y - mu
            var = _split_dot(yc * yc, seg) * (1.0 / HEAD_DIM)
            yn = yc * lax.rsqrt(var + GN_EPS) * lnw_ref[:, ls] + lnb_ref[:, ls]
            rr = r_ref[ib, :, ls]
            bonus = _split_dot(rr * k_ref[ib, :, ls] * rk_ref[:, ls], seg) * v_ref[ib, :, ls]
            o_ref[ib, :, ls] = ((yn + bonus) * g_ref[ib, :, ls]).astype(o_ref.dtype)

    @pl.when(tstep == pl.num_programs(2) - 1)
    def _():
        for ib in range(bb):
            for ip in range(npair):
                st = state[ib * npair + ip]
                s_out[ib, ip] = st[:half] + st[half:]


def _wkv(r, lw, k, v, a, b, g, w, chunk, tblk, bb, npair):
    bsz, t, d = r.shape
    np_total = d // LANES
    assert bsz % bb == 0 and np_total % npair == 0 and t % tblk == 0 and tblk % chunk == 0
    blk = pl.BlockSpec((bb, tblk, npair * LANES), lambda ib, ip, it: (ib, it, ip))
    par = pl.BlockSpec((1, npair * LANES), lambda ib, ip, it: (0, ip))
    st_spec = pl.BlockSpec((bb, npair, HEAD_DIM, LANES), lambda ib, ip, it: (ib, ip, 0, 0))
    return pl.pallas_call(
        functools.partial(_wkv_kernel, chunk, tblk // chunk, bb, npair),
        out_shape=[jax.ShapeDtypeStruct((bsz, t, d), BF16),
                   jax.ShapeDtypeStruct((bsz, np_total, HEAD_DIM, LANES), F32)],
        grid=(bsz // bb, np_total // npair, t // tblk),
        in_specs=[blk] * 7 + [par] * 3 + [_const_spec(w['seg2'].shape)],
        out_specs=[blk, st_spec],
        scratch_shapes=[pltpu.VMEM((bb * npair, LANES, LANES), F32),
                        pltpu.VMEM((bb, tblk, npair * LANES), F32)],
        compiler_params=_cparams(3),
        name="wkv",
    )(r, lw, k, v, a, b, g, w['rk'], w['lnw'], w['lnb'], w['seg2'])


def _wkv_decode_kernel(n_tok, hb, r_ref, lw_ref, k_ref, v_ref, a_ref, b_ref, g_ref, rk_ref, lnw_ref, lnb_ref,
                       s_in, o_ref, s_out, w_s, y_s):
    bsz = r_ref.shape[2]
    w_s[...] = jnp.exp(lw_ref[...])
    for h in range(hb):
        rows = slice(h * HEAD_DIM, (h + 1) * HEAD_DIM)

        def body(i, carry, h=h, rows=rows):
            s = s_in[h, i]
            for t in range(n_tok):
                sa = jnp.sum(s * a_ref[t, rows, :], axis=0, keepdims=True)
                vi = v_ref[t, pl.ds(h * HEAD_DIM + i, 1), :]
                s = s * w_s[t, rows, :] + sa * b_ref[t, rows, :] + vi * k_ref[t, rows, :]
                y_s[t, pl.ds(h * HEAD_DIM + i, 1), :] = jnp.sum(s * r_ref[t, rows, :], axis=0, keepdims=True)
            s_out[h, i] = s
            return carry

        lax.fori_loop(0, HEAD_DIM, body, 0)

    for t in range(n_tok):
        outs = []
        for h in range(hb):
            rows = slice(h * HEAD_DIM, (h + 1) * HEAD_DIM)
            y = y_s[t, rows, :]
            mu = jnp.mean(y, axis=0, keepdims=True)
            yc = y - mu
            var = jnp.mean(yc * yc, axis=0, keepdims=True)
            yn = yc * lax.rsqrt(var + GN_EPS) * lnw_ref[rows, :] + lnb_ref[rows, :]
            bonus = jnp.sum(r_ref[t, rows, :] * k_ref[t, rows, :] * rk_ref[rows, :], axis=0,
                            keepdims=True) * v_ref[t, rows, :]
            outs.append((yn + bonus) * g_ref[t, rows, :])
        o_ref[t * bsz:(t + 1) * bsz, :] = jnp.concatenate(outs, axis=0).T.astype(o_ref.dtype)


def _wkv_decode(r, lw, k, v, a, b, g, rk, lnw, lnb, state, hb):
    n_tok, d, bsz = r.shape
    n_heads = d // HEAD_DIM
    assert n_heads % hb == 0
    blk = pl.BlockSpec((n_tok, hb * HEAD_DIM, bsz), lambda i: (0, i, 0))
    par = pl.BlockSpec((hb * HEAD_DIM, bsz), lambda i: (i, 0))
    st = pl.BlockSpec((hb, HEAD_DIM, HEAD_DIM, bsz), lambda i: (i, 0, 0, 0))
    return pl.pallas_call(
        functools.partial(_wkv_decode_kernel, n_tok, hb),
        out_shape=[jax.ShapeDtypeStruct((n_tok * bsz, d), BF16), jax.ShapeDtypeStruct(state.shape, F32)],
        grid=(n_heads // hb,),
        in_specs=[blk] * 7 + [par] * 3 + [st],
        out_specs=[pl.BlockSpec((n_tok * bsz, hb * HEAD_DIM), lambda i: (0, i)), st],
        scratch_shapes=[pltpu.VMEM((n_tok, hb * HEAD_DIM, bsz), F32)] * 2,
        compiler_params=_cparams(1),
        name="wkv_decode",
    )(r, lw, k, v, a, b, g, rk, lnw, lnb, state)


def _ffn_kernel(block_mode, tiles_per_seq, fc, final_norm, *refs):
    refs = list(refs)
    x_ref, o_ref, p_ref = refs[:3]
    pos = 3
    if block_mode:
        st_ref = refs[pos]
        pos += 1
    (wo_ref, gffn_ref, wup_ref, cw_ref, cb_ref, wd_ref, wg_ref, wp_ref, gout_ref) = refs[pos:pos + 9]
    pos += 9
    y_o, u_o = refs[pos:pos + 2]
    pos += 2
    x1_s, hn_s, act_s = refs[pos:pos + 3]
    pos += 3
    i = pl.program_id(0)
    dff = wd_ref.shape[0]
    n_f = dff // fc
    tm = x_ref.shape[0]
    if block_mode:
        c1_s, c2_s = refs[pos:pos + 2]

        @pl.when(i == 0)
        def _():
            c2_s[...] = st_ref[:, 0:2 * dff]
            c1_s[...] = st_ref[:, 2 * dff:4 * dff]
    else:
        c_s = refs[pos]

        @pl.when(i % tiles_per_seq == 0)
        def _():
            c_s[...] = jnp.zeros_like(c_s)

    x1 = x_ref[...] + jnp.dot(o_ref[...], wo_ref[...], preferred_element_type=F32)
    x1_s[...] = x1
    hn_s[...] = _rms(x1, gffn_ref[...]).astype(BF16)

    def up(f):
        hn = hn_s[...]
        return tuple(jnp.dot(hn, wup_ref[:, half * dff + f * fc:half * dff + (f + 1) * fc],
                             preferred_element_type=F32) for half in range(2))

    u_next = up(0)
    for f in range(n_f):
        u_cur = u_next
        if f + 1 < n_f:
            u_next = up(f + 1)
        conv = []
        for half in range(2):
            cols = slice(half * dff + f * fc, half * dff + (f + 1) * fc)
            u = u_cur[half]
            if block_mode:
                u1, u2 = c1_s[:, cols], c2_s[:, cols]
                c2_s[:, cols] = u1
                c1_s[:, cols] = u
                u_o[0, :, cols] = u
            else:
                carry = c_s[:, cols]
                u1, u2 = _prev_rows_carry(u, 1, carry), _prev_rows_carry(u, 2, carry)
                c_s[:, cols] = u[tm - SUBLANES:tm, :]
                u_o[0, :, cols] = u[tm - SUBLANES:tm, :]
            conv.append(cb_ref[:, cols] + cw_ref[0:1, cols] * u2 + cw_ref[1:2, cols] * u1 + cw_ref[2:3, cols] * u)
        gate, val = conv
        act_s[:, f * fc:(f + 1) * fc] = (gate * _sigmoid(gate) * val).astype(BF16)

    x2 = x1_s[...] + jnp.dot(act_s[...], wd_ref[...], preferred_element_type=F32)
    x3 = x2 + _sigmoid(_dot(x2, wg_ref[...])) * _dot(p_ref[...], wp_ref[...])
    if final_norm:
        x3 = _rms(x3, gout_ref[...])
    y_o[...] = x3


def _ffn(x, o, p, layer, seq_len, conv_state, w, final_norm, tm, fc):
    n, d = x.shape
    dff = w['wd'].shape[0]
    pdim = p.shape[2]
    block_mode = conv_state is not None
    assert dff % fc == 0
    if block_mode:
        tm = n // seq_len
        tiles_per_seq = 1
    else:
        tm = min(tm, seq_len)
        assert seq_len % tm == 0 and n % seq_len == 0
        tiles_per_seq = seq_len // tm
    rowd = pl.BlockSpec((tm, d), lambda i: (i, 0))
    rowp = pl.BlockSpec((None, tm, pdim), lambda i: (layer, i, 0))

    def resident(a):
        nd = a.ndim
        return pl.BlockSpec(a.shape, lambda i: (0,) * nd, pipeline_mode=pl.Buffered(1))

    in_specs = [rowd, rowd, rowp]
    args = [x, o, p]
    if block_mode:
        in_specs.append(resident(conv_state))
        args.append(conv_state)
    consts = [w['wo'], w['gffn'], w['wup'], w['cw'], w['cb'], w['wd'], w['wg'], w['wp'], w['gout']]
    in_specs += [resident(c) for c in consts]
    args += consts
    if block_mode:
        u_shape = jax.ShapeDtypeStruct((seq_len, tm, 2 * dff), F32)
        u_spec = pl.BlockSpec((1, tm, 2 * dff), lambda i: (i, 0, 0))
        scratch_c = [pltpu.VMEM((tm, 2 * dff), F32)] * 2
    else:
        u_shape = jax.ShapeDtypeStruct((n // tm, SUBLANES, 2 * dff), F32)
        u_spec = pl.BlockSpec((1, SUBLANES, 2 * dff), lambda i: (i, 0, 0))
        scratch_c = [pltpu.VMEM((SUBLANES, 2 * dff), F32)]
    return pl.pallas_call(
        functools.partial(_ffn_kernel, block_mode, tiles_per_seq, fc, final_norm),
        out_shape=[jax.ShapeDtypeStruct((n, d), F32), u_shape],
        grid=(n // tm,),
        in_specs=in_specs,
        out_specs=[rowd, u_spec],
        scratch_shapes=[pltpu.VMEM((tm, d), F32), pltpu.VMEM((tm, d), BF16), pltpu.VMEM((tm, dff), BF16)] + scratch_c,
        compiler_params=_cparams(1),
        name="ffn",
    )(*args)


def _qkv_kernel(prompt_mode, *refs):
    if prompt_mode:
        (x_ref, g_ref, w_ref, cos_ref, s1_ref, s2_ref, q_o, kt_o, vt_o, k16_o, vt16_o) = refs
    else:
        (x_ref, g_ref, w_ref, cos_ref, s1_ref, s2_ref, q_o, kt_o, vt_o, k_o, v_o) = refs
    hn = _rms(x_ref[...], g_ref[...]).astype(BF16)
    d = hn.shape[1]
    reps = d // LANES
    cos = jnp.concatenate([cos_ref[...]] * reps, axis=1)
    s1 = jnp.concatenate([s1_ref[...]] * reps, axis=1)
    s2 = jnp.concatenate([s2_ref[...]] * reps, axis=1)
    half = ROT_DIM // 2

    def rope(z):
        return z * cos + pltpu.roll(z, d - half, axis=1) * s1 + pltpu.roll(z, half, axis=1) * s2

    q = rope(jnp.dot(hn, w_ref[:, 0:d], preferred_element_type=F32)) * (HEAD_DIM ** -0.5)
    k = rope(jnp.dot(hn, w_ref[:, d:2 * d], preferred_element_type=F32))
    v = jnp.dot(hn, w_ref[:, 2 * d:3 * d], preferred_element_type=F32)
    vt = v.T
    kt_o[0] = k.T
    vt_o[0] = vt
    if prompt_mode:
        q_o[...] = (q * LOG2E).astype(BF16)
        k16_o[...] = k.astype(BF16)
        vt16_o[0] = vt.astype(BF16)
    else:
        q_o[...] = q
        k_o[...] = k
        v_o[...] = v


def _qkv(x, group, gmix, wqkv, cos, s1, s2, prompt_mode, tm):
    n, d = x.shape
    tm = min(tm, group)
    assert group % tm == 0 and n % group == 0
    tpg = group // tm
    row = pl.BlockSpec((tm, d), lambda i: (i, 0))
    rowt = pl.BlockSpec((tm, LANES), lambda i: (i, 0))
    tr = pl.BlockSpec((1, d, tm), lambda i: (i // tpg, 0, i % tpg))
    big = jax.ShapeDtypeStruct((n, d), F32)
    big16 = jax.ShapeDtypeStruct((n, d), BF16)
    bigt = jax.ShapeDtypeStruct((n // group, d, group), F32)
    if prompt_mode:
        out_shape = [big16, bigt, bigt, big16, jax.ShapeDtypeStruct((n // group, d, group), BF16)]
        out_specs = [row, tr, tr, row, tr]
    else:
        out_shape = [big, bigt, bigt, big, big]
        out_specs = [row, tr, tr, row, row]
    return pl.pallas_call(
        functools.partial(_qkv_kernel, prompt_mode),
        out_shape=out_shape,
        grid=(n // tm,),
        in_specs=[row, _const_spec(gmix.shape), _const_spec(wqkv.shape), rowt, rowt, rowt],
        out_specs=out_specs,
        compiler_params=_cparams(1),
        name="qkv",
    )(x, gmix, wqkv, cos, s1, s2)


def _moba_kernel(n_blocks, grp, tiles_per_step, q_ref, k_ref, vt_ref, o_ref, *scratch):
    def tile(t, carry):
        r0 = pl.multiple_of(t * MOBA_BLOCK, MOBA_BLOCK)
        qi = pl.program_id(2) * tiles_per_step + t
        o_ref[0, pl.ds(r0, MOBA_BLOCK), :] = _moba_tile(
            n_blocks, grp, qi, q_ref[0, pl.ds(r0, MOBA_BLOCK), :], k_ref, vt_ref, *scratch).astype(o_ref.dtype)
        return carry

    lax.fori_loop(0, tiles_per_step, tile, 0)


def _moba_tile(n_blocks, grp, qi, q, k_ref, vt_ref, o_s, m_s, l_s, g_s, w_s):
    blk = MOBA_BLOCK
    sub = min(2, grp)
    lane = lax.broadcasted_iota(jnp.int32, (1, LANES), 1)
    zero16 = jnp.zeros_like(q)
    qh = [jnp.where(lane < HEAD_DIM, q, zero16), jnp.where(lane >= HEAD_DIM, q, zero16)]
    ones_rows = jnp.where(lax.broadcasted_iota(jnp.int32, (2 * SUBLANES, blk), 0) == 0, 1.0, 0.0).astype(BF16)
    neg_inf = jnp.float32(-jnp.inf)
    for h in range(2):
        m_s[h] = jnp.full((n_blocks, blk), NEG_BIG, F32)
        l_s[h] = jnp.zeros((n_blocks, blk), F32)
        g_s[h] = jnp.full((n_blocks, blk), neg_inf, F32)

    def scores(j0, nb):
        k0 = pl.multiple_of(j0 * blk, blk)
        kg = k_ref[0, pl.ds(k0, nb * blk), :]
        return [lax.dot_general(kg, qh[h], (((1,), (1,)), ((), ())), preferred_element_type=F32)
                for h in range(2)]

    def partials(j0, nb, st, diag):
        parts = []
        for h in range(2):
            for jb in range(nb):
                s = st[h][jb * blk:(jb + 1) * blk]
                gate = jnp.sum(s, axis=0, keepdims=True) * (1.0 / blk)
                s = s.astype(BF16)
                if diag:
                    kr = lax.broadcasted_iota(jnp.int32, (blk, blk), 0)
                    qc = lax.broadcasted_iota(jnp.int32, (blk, blk), 1)
                    s = jnp.where(kr <= qc, s, jnp.full_like(s, NEG_BIG))
                mj = jnp.max(s, axis=0, keepdims=True)
                p = jnp.exp2(s - mj)
                parts.append((h, jb, gate, mj.astype(F32), p))
        pvs = []
        for h, jb, gate, mj, p in parts:
            kb0 = pl.multiple_of((j0 + jb) * blk, blk)
            vt_ext = jnp.concatenate([vt_ref[0, h * HEAD_DIM:(h + 1) * HEAD_DIM, pl.ds(kb0, blk)], ones_rows],
                                     axis=0)
            pvs.append(jnp.dot(vt_ext, p, preferred_element_type=F32))
        for (h, jb, gate, mj, p), pv in zip(parts, pvs):
            j = j0 + jb
            o_s[h * n_blocks + j] = pv[:HEAD_DIM]
            l_s[h, pl.ds(j, 1), :] = pv[HEAD_DIM:HEAD_DIM + 1]
            m_s[h, pl.ds(j, 1), :] = mj
            g_s[h, pl.ds(j, 1), :] = gate

    def group_body(gi, carry):
        j0 = gi * grp
        st = scores(j0, sub)
        for s in range(1, grp // sub):
            st_next = scores(j0 + s * sub, sub)
            partials(j0 + (s - 1) * sub, sub, st, False)
            st = st_next
        partials(j0 + grp - sub, sub, st, False)
        return carry

    lax.fori_loop(0, (qi + grp - 1) // grp, group_body, 0)
    partials(qi, 1, scores(qi, 1), True)

    jrow = lax.broadcasted_iota(jnp.int32, (n_blocks, 1), 0)
    jrow_f = jrow.astype(F32)
    lsum = []
    for h in range(2):
        cur = jnp.where(jrow < qi, g_s[h], neg_inf)
        sel = jrow == qi
        for _ in range(MOBA_TOPK):
            mx = jnp.max(cur, axis=0, keepdims=True)
            first = jnp.min(jnp.where(cur == mx, jrow_f, float(n_blocks)), axis=0, keepdims=True)
            onehot = jrow_f == first
            sel = jnp.logical_or(sel, jnp.logical_and(onehot, mx > neg_inf))
            cur = jnp.where(onehot, neg_inf, cur)
        m = m_s[h]
        mx = jnp.max(jnp.where(sel, m, NEG_BIG), axis=0, keepdims=True)
        w = jnp.where(sel, jnp.exp2(m - mx), 0.0)
        w_s[h] = w
        lsum.append(jnp.sum(w * l_s[h], axis=0, keepdims=True))

    def merge(j, accs):
        return tuple(accs[h] + w_s[h, pl.ds(j, 1), :] * o_s[h * n_blocks + j] for h in range(2))

    accs = lax.fori_loop(0, qi + 1, merge, tuple(jnp.zeros((HEAD_DIM, blk), F32) for _ in range(2)))
    return jnp.concatenate([accs[h] / lsum[h] for h in range(2)], axis=0).T


def _moba_prompt(q16, k16, vt16, grp, tiles_per_step):
    bsz, t, d = q16.shape
    n_blocks = t // MOBA_BLOCK
    assert n_blocks % grp == 0 and n_blocks % tiles_per_step == 0
    stat = pltpu.VMEM((2, n_blocks, MOBA_BLOCK), F32)
    rows = tiles_per_step * MOBA_BLOCK
    return pl.pallas_call(
        functools.partial(_moba_kernel, n_blocks, grp, tiles_per_step),
        out_shape=jax.ShapeDtypeStruct((bsz, t, d), BF16),
        grid=(bsz, d // LANES, n_blocks // tiles_per_step),
        in_specs=[pl.BlockSpec((1, rows, LANES), lambda b, p, i: (b, i, p)),
                  pl.BlockSpec((1, t, LANES), lambda b, p, i: (b, 0, p)),
                  pl.BlockSpec((1, LANES, t), lambda b, p, i: (b, p, 0))],
        out_specs=pl.BlockSpec((1, rows, LANES), lambda b, p, i: (b, i, p)),
        scratch_shapes=[pltpu.VMEM((2 * n_blocks, HEAD_DIM, MOBA_BLOCK), F32), stat, stat, stat, stat],
        compiler_params=_cparams(3),
        name="moba_prompt",
    )(q16, k16, vt16)


def _moba_decode_kernel(n_tok, n_heads, pages_per_step, n_steps, pt_ref, q_ref, kn_ref, vn_ref, *refs):
    kp = refs[:pages_per_step]
    vp = refs[pages_per_step:2 * pages_per_step]
    o_ref = refs[2 * pages_per_step]
    m_s, l_s, g_s, acc_s = refs[2 * pages_per_step + 1:]
    step = pl.program_id(1)
    d = q_ref.shape[2]
    page = kp[0].shape[2]
    pages_per_blk = MOBA_BLOCK // page
    blk_per_step = pages_per_step // pages_per_blk
    n_past = n_steps * blk_per_step
    nrow = n_tok * n_heads
    q = q_ref[0]
    lane_head = lax.broadcasted_iota(jnp.int32, (n_heads, d), 1) // HEAD_DIM
    row_head = lax.broadcasted_iota(jnp.int32, (n_heads, d), 0)
    hmask = lane_head == row_head
    qbd = jnp.concatenate(
        [jnp.where(hmask, jnp.broadcast_to(q[t:t + 1, :], (n_heads, d)), 0.0) for t in range(n_tok)], axis=0)
    qbd16 = qbd.astype(BF16)

    scores = [jnp.concatenate([_dot(qbd16, kp[jb * pages_per_blk + i][0]) for i in range(pages_per_blk)], axis=1)
              for jb in range(blk_per_step)]
    stats = []
    for s in scores:
        gate = jnp.sum(s, axis=1, keepdims=True) * (1.0 / MOBA_BLOCK)
        mj = jnp.max(s, axis=1, keepdims=True)
        p = jnp.exp(s - mj)
        stats.append((gate, mj, jnp.sum(p, axis=1, keepdims=True), p.astype(BF16)))
    outs = [sum(_dot_nt(p[:, i * page:(i + 1) * page], vp[jb * pages_per_blk + i][0]) for i in range(pages_per_blk))
            for jb, (_, _, _, p) in enumerate(stats)]
    for jb, ((gate, mj, lj, _), oj) in enumerate(zip(stats, outs)):
        slot = step * blk_per_step + jb
        m_s[slot] = jnp.broadcast_to(mj, (nrow, LANES))
        l_s[slot] = jnp.broadcast_to(lj, (nrow, LANES))
        g_s[slot] = jnp.broadcast_to(gate, (nrow, LANES))
        acc_s[slot] = oj

    @pl.when(step == n_steps - 1)
    def _():
        gates = [g_s[j][:, 0:1] for j in range(n_past)]
        kn = kn_ref[0]
        vn = vn_ref[0]
        trow = lax.broadcasted_iota(jnp.int32, (nrow, 1), 0) // n_heads
        s_own = [jnp.where(trow >= t, jnp.sum(qbd * kn[t:t + 1, :], axis=1, keepdims=True), NEG_BIG)
                 for t in range(n_tok)]
        m = functools.reduce(jnp.maximum, s_own)
        p_own = [jnp.exp(s - m) for s in s_own]
        l = sum(p_own)
        acc = sum(p_own[t] * vn[t:t + 1, :] for t in range(n_tok))
        for j in range(n_past):
            rank = sum(jnp.where(jnp.logical_or(gates[i] > gates[j],
                                                jnp.logical_and(gates[i] == gates[j], i < j)), 1.0, 0.0)
                       for i in range(n_past) if i != j)
            selj = rank < float(MOBA_TOPK)
            mj = m_s[j][:, 0:1]
            m_new = jnp.where(selj, jnp.maximum(m, mj), m)
            alpha = jnp.exp(m - m_new)
            beta = jnp.where(selj, jnp.exp(mj - m_new), 0.0)
            l = l * alpha + l_s[j][:, 0:1] * beta
            acc = acc * alpha + acc_s[j] * beta
            m = m_new
        lane_head_r = lax.broadcasted_iota(jnp.int32, (nrow, d), 1) // HEAD_DIM
        row_head_r = lax.broadcasted_iota(jnp.int32, (nrow, d), 0) % n_heads
        out = jnp.where(lane_head_r == row_head_r, acc / l, 0.0)
        o_ref[0] = jnp.concatenate(
            [jnp.sum(out[t * n_heads:(t + 1) * n_heads], axis=0, keepdims=True) for t in range(n_tok)],
            axis=0).astype(o_ref.dtype)


def _moba_decode(q, kn, vn, cache_kt, cache_vt, page_table, pages_per_step):
    bsz, n_tok, d = q.shape
    n_heads = d // HEAD_DIM
    pool, _, page = cache_kt.shape
    n_pages = page_table.shape[1]
    assert n_pages % pages_per_step == 0 and MOBA_BLOCK % page == 0
    assert (n_pages * page) % MOBA_BLOCK == 0 and n_tok <= MOBA_BLOCK
    assert pages_per_step % (MOBA_BLOCK // page) == 0
    n_steps = n_pages // pages_per_step
    n_past = n_pages * page // MOBA_BLOCK
    nrow = n_tok * n_heads
    tok = pl.BlockSpec((1, n_tok, d), lambda b, s, pt: (b, 0, 0))

    def page_spec(i):
        return pl.BlockSpec((1, d, page), lambda b, s, pt: (pt[b, s * pages_per_step + i], 0, 0))

    grid_spec = pltpu.PrefetchScalarGridSpec(
        num_scalar_prefetch=1,
        grid=(bsz, n_steps),
        in_specs=[tok, tok, tok] + [page_spec(i) for i in range(pages_per_step)] * 2,
        out_specs=tok,
        scratch_shapes=[pltpu.VMEM((n_past, nrow, LANES), F32)] * 3 + [pltpu.VMEM((n_past, nrow, d), F32)],
    )
    return pl.pallas_call(
        functools.partial(_moba_decode_kernel, n_tok, n_heads, pages_per_step, n_steps),
        out_shape=jax.ShapeDtypeStruct((bsz, n_tok, d), BF16),
        grid_spec=grid_spec,
        compiler_params=_cparams(2),
        name="moba_decode",
    )(page_table, q, kn, vn, *([cache_kt] * pages_per_step), *([cache_vt] * pages_per_step))


def _pad_cols(w, n):
    return jnp.pad(w, ((0, 0), (0, n - w.shape[1])))


def _pad_rows(w, n):
    return jnp.pad(w, ((0, n - w.shape[0]), (0, 0)))


def _rope_tables(pos):
    half = ROT_DIM // 2
    inv = ROPE_THETA ** (-2.0 * jnp.arange(half, dtype=F32) / ROT_DIM)
    ang = pos.astype(F32)[:, None] * inv[None, :]
    cos = jnp.cos(ang)
    sin = jnp.sin(ang)
    t = pos.shape[0]
    ones = jnp.ones((t, HEAD_DIM - ROT_DIM), F32)
    zeros_r = jnp.zeros((t, HEAD_DIM - ROT_DIM), F32)
    zeros_h = jnp.zeros((t, half), F32)
    c = jnp.concatenate([cos, cos, ones], axis=1)
    s1 = jnp.concatenate([-sin, zeros_h, zeros_r], axis=1)
    s2 = jnp.concatenate([zeros_h, sin, zeros_r], axis=1)
    rep = LANES // HEAD_DIM
    return jnp.tile(c, (1, rep)), jnp.tile(s1, (1, rep)), jnp.tile(s2, (1, rep))


def _unpack_state(s):
    b, hp, dv, dk2 = s.shape
    return s.reshape(b, hp, dv, 2, dk2 // 2).transpose(0, 1, 3, 2, 4).reshape(b, hp * 2, dv, dk2 // 2)


def kernel(x_prompt, x_sample, state_wkv, state_shift, cache_k, cache_v, state_conv, page_table, p_prompt, p_sample, rw_mix, rw_rkv, rw_w0, rw_w1, rw_w2, rw_a0, rw_a1, rw_a2, rw_g1, rw_g2, rw_kk, rw_ka, rw_rk, rw_lnw, rw_lnb, rw_wo, mb_wqkv, mb_wo, norm_mix, norm_ffn, ff_wup, ff_conv_w, ff_conv_b, ff_wdown, ple_wp, ple_wg, norm_out):
    bp, tp, d = x_prompt.shape
    bs, ts, _ = x_sample.shape
    depth = norm_mix.shape[0]
    n_heads = d // HEAD_DIM
    dff = ff_wdown.shape[1]
    pool, page = cache_k.shape[1], cache_k.shape[2]
    past_len = page_table.shape[1] * page
    assert d % LANES == 0 and depth == 2 and ts >= 2

    lane_head = jnp.arange(d) // HEAD_DIM
    segr = (lane_head[:, None] == jnp.arange(LANES)[None, :]).astype(BF16)
    segb = segr.T
    seg2 = ((jnp.arange(LANES) // HEAD_DIM)[:, None] == (jnp.arange(LANES) // HEAD_DIM)[None, :]).astype(BF16)

    def row(v):
        return v.reshape(1, -1).astype(F32)

    def col(v):
        return jnp.broadcast_to(v.reshape(-1, 1).astype(F32), (v.size, bs))

    rw = dict(
        gmix=row(norm_mix[0]), mix=rw_mix[0], wrkv=rw_rkv[0].astype(BF16), w0=row(rw_w0[0]),
        w1=_pad_cols(rw_w1[0], LANES).astype(BF16), w2=_pad_rows(rw_w2[0], LANES).astype(BF16),
        a0=row(rw_a0[0]), a1=_pad_cols(rw_a1[0], LANES).astype(BF16), a2=_pad_rows(rw_a2[0], LANES).astype(BF16),
        g1=_pad_cols(rw_g1[0], 2 * LANES).astype(BF16), g2=_pad_rows(rw_g2[0], 2 * LANES).astype(BF16),
        kk=row(rw_kk[0]), ka=row(rw_ka[0]), segr=segr, segb=segb,
        rk=row(rw_rk[0]), lnw=row(rw_lnw[0]), lnb=row(rw_lnb[0]), seg2=seg2)

    def ffw(i, wo):
        return dict(wo=wo.astype(BF16), gffn=row(norm_ffn[i]), wup=ff_wup[i].astype(BF16), cw=ff_conv_w[i],
                    cb=row(ff_conv_b[i]), wd=ff_wdown[i].astype(BF16), wg=ple_wg[i].astype(BF16),
                    wp=ple_wp[i].astype(BF16), gout=row(norm_out))

    fw0 = ffw(0, rw_wo[0])
    fw1 = ffw(1, mb_wo[0])
    wqkv = mb_wqkv[0].astype(BF16)
    gmix1 = row(norm_mix[1])

    xp = x_prompt.reshape(bp * tp, d)
    pp = p_prompt.reshape(depth, bp * tp, -1)
    xs = x_sample.transpose(1, 0, 2).reshape(ts * bs, d)
    ps = p_sample.transpose(0, 2, 1, 3).reshape(depth, ts * bs, -1)

    def conv_state_p(u):
        return u.reshape(bp, -1, SUBLANES, 2 * dff)[:, -1, SUBLANES - 2:]

    def conv_state_s(u):
        return u[ts - 2:].transpose(1, 0, 2)

    r, lw, k, v, a, b, g, shift_p = _rwkv_proj(xp, tp, None, rw)
    sh = lambda z: z.reshape(bp, tp, d)
    o_p, wkv_p = _wkv(sh(r), sh(lw), sh(k), sh(v), sh(a), sh(b), sh(g), rw,
                      chunk=64, tblk=min(512, tp), bb=bp, npair=4)
    xp, up_ = _ffn(xp, o_p.reshape(bp * tp, d), pp, 0, tp, None, fw0, False, 512, 256)
    conv_p0 = conv_state_p(up_)

    r, lw, k, v, a, b, g, shift_s = _rwkv_proj(xs, ts, state_shift[0], rw)
    st_in = jnp.transpose(state_wkv[0], (1, 2, 3, 0))
    o_s, st_out = _wkv_decode(r, lw, k, v, a, b, g, col(rw_rk[0]), col(rw_lnw[0]), col(rw_lnb[0]), st_in, 2)
    wkv_s = jnp.transpose(st_out, (3, 0, 1, 2))
    xs, us_ = _ffn(xs, o_s, ps, 0, ts, state_conv[0].reshape(bs, 4 * dff), fw0, False, 512, 256)
    conv_s0 = conv_state_s(us_)

    cos, s1, s2 = _rope_tables(jnp.arange(tp, dtype=jnp.int32))
    cos_p, s1_p, s2_p = (jnp.tile(z, (bp, 1)) for z in (cos, s1, s2))
    q16, kt, vt, k16, vt16 = _qkv(xp, tp, gmix1, wqkv, cos_p, s1_p, s2_p, True, 256)
    k_p = kt.reshape(bp, n_heads, HEAD_DIM, tp).transpose(0, 3, 1, 2)
    v_p = vt.reshape(bp, n_heads, HEAD_DIM, tp).transpose(0, 3, 1, 2)
    o_p = _moba_prompt(q16.reshape(bp, tp, d), k16.reshape(bp, tp, d), vt16, min(8, tp // MOBA_BLOCK),
                       min(4, tp // MOBA_BLOCK))
    xp, up_ = _ffn(xp, o_p.reshape(bp * tp, d), pp, 1, tp, None, fw1, True, 512, 256)
    conv_p1 = conv_state_p(up_)

    cos, s1, s2 = _rope_tables(past_len + jnp.arange(ts, dtype=jnp.int32))
    cos_s, s1_s, s2_s = (jnp.repeat(z, bs, axis=0) for z in (cos, s1, s2))
    q, kt, vt, k, v = _qkv(xs, bs, gmix1, wqkv, cos_s, s1_s, s2_s, False, 128)
    k_s = kt.reshape(ts, n_heads, HEAD_DIM, bs).transpose(3, 0, 1, 2)
    v_s = vt.reshape(ts, n_heads, HEAD_DIM, bs).transpose(3, 0, 1, 2)
    by_seq = lambda z: z.reshape(ts, bs, d).transpose(1, 0, 2)
    cache_kt = jnp.transpose(cache_k[0], (0, 2, 3, 1)).reshape(pool, d, page)
    cache_vt = jnp.transpose(cache_v[0], (0, 2, 3, 1)).reshape(pool, d, page)
    o_s = _moba_decode(by_seq(q), by_seq(k), by_seq(v), cache_kt, cache_vt, page_table,
                       min(8, page_table.shape[1]))
    o_s = o_s.transpose(1, 0, 2).reshape(ts * bs, d)
    xs, us_ = _ffn(xs, o_s, ps, 1, ts, state_conv[1].reshape(bs, 4 * dff), fw1, True, 512, 256)
    conv_s1 = conv_state_s(us_)

    return (xp.reshape(bp, tp, d), by_seq(xs),
            _unpack_state(wkv_p)[None], shift_p.reshape(1, bp, d), k_p[None], v_p[None],
            jnp.stack([conv_p0, conv_p1]),
            wkv_s[None], shift_s[None], k_s[None], v_s[None],
            jnp.stack([conv_s0, conv_s1]))
```

```python
import functools

import jax
import jax.numpy as jnp
from jax import lax
from jax.experimental import pallas as pl
from jax.experimental.pallas import tpu as pltpu

F32 = jnp.float32
BF16 = jnp.bfloat16

HEAD_DIM = 64
GN_EPS = 64e-5
RMS_EPS = 1e-6
MOBA_BLOCK = 256
MOBA_TOPK = 3
ROPE_THETA = 500000.0
ROT_DIM = HEAD_DIM // 4
NEG_BIG = -1e30
LOG2E = 1.4426950408889634
LANES = 128
SUBLANES = 8
VMEM_LIMIT_BYTES = 56 * 1024 * 1024

PROJ_ROW_TILE = 256
FFN_ROW_TILE = 512
FFN_CHUNK = 256
WKV_CHUNK = 64
WKV_TIME_BLOCK = 512
WKV_PAIRS_PER_STEP = 4
WKV_DECODE_HEADS = 2
MOBA_GROUP = 8
MOBA_SUBGROUP = 2
MOBA_TILES_PER_STEP = 4
DECODE_PAGES_PER_STEP = 8


def _cparams(n_axes):
    return pltpu.CompilerParams(dimension_semantics=("arbitrary",) * n_axes,
                                vmem_limit_bytes=VMEM_LIMIT_BYTES)


def _const_spec(shape):
    nd = len(shape)
    return pl.BlockSpec(shape, lambda *_: (0,) * nd)


def _dot(a, b):
    return jnp.dot(a.astype(BF16), b.astype(BF16), preferred_element_type=F32)


def _dot_nt(a, b):
    return lax.dot_general(a.astype(BF16), b.astype(BF16), (((1,), (1,)), ((), ())),
                           preferred_element_type=F32)


def _dot_tn(a, b):
    return lax.dot_general(a.astype(BF16), b.astype(BF16), (((0,), (0,)), ((), ())),
                           preferred_element_type=F32)


def _split_dot(x, m):
    hi = x.astype(BF16)
    lo = (x - hi.astype(F32)).astype(BF16)
    return (jnp.dot(hi, m, preferred_element_type=F32) + jnp.dot(lo, m, preferred_element_type=F32))


def _sigmoid(z):
    return 1.0 / (1.0 + jnp.exp(-z))


def _rms(x, g):
    return x * lax.rsqrt(jnp.mean(x * x, axis=-1, keepdims=True) + RMS_EPS) * g


def _seg_sum(x, segr, segb):
    return _split_dot(_split_dot(x, segr), segb)


def _prev_rows_carry(u, k, carry):
    rolled = pltpu.roll(u, k, axis=0)
    row = lax.broadcasted_iota(jnp.int32, (u.shape[0], 1), 0)
    out = rolled
    for j in range(k):
        out = jnp.where(row == j, carry[SUBLANES - k + j:SUBLANES - k + j + 1, :], out)
    return out


def _rwkv_proj_kernel(block_mode, tiles_per_seq, *refs):
    if block_mode:
        (x_ref, init_ref, gmix_ref, mix_ref, wrkv_ref, w0_ref, w1_ref, w2_ref, a0_ref, a1_ref,
         a2_ref, g1_ref, g2_ref, kk_ref, ka_ref, segr_ref, segb_ref,
         r_o, lw_o, k_o, v_o, a_o, b_o, g_o, xn_o, carry) = refs
    else:
        (x_ref, gmix_ref, mix_ref, wrkv_ref, w0_ref, w1_ref, w2_ref, a0_ref, a1_ref, a2_ref,
         g1_ref, g2_ref, kk_ref, ka_ref, segr_ref, segb_ref,
         r_o, lw_o, k_o, v_o, a_o, b_o, g_o, xn_o, carry) = refs
    xn = _rms(x_ref[...], gmix_ref[...])
    tm = xn.shape[0]
    if block_mode:
        @pl.when(pl.program_id(0) == 0)
        def _():
            carry[...] = init_ref[...]
        xprev = carry[...]
        carry[...] = xn
        xn_o[...] = xn
    else:
        @pl.when(pl.program_id(0) % tiles_per_seq == 0)
        def _():
            carry[...] = jnp.zeros_like(carry)
        xprev = _prev_rows_carry(xn, 1, carry)
        carry[...] = xn[tm - SUBLANES:tm, :]
        xn_o[0] = xn[tm - 1:tm, :]
    dx = xprev - xn

    def xm(i):
        return (xn + dx * mix_ref[i:i + 1, :]).astype(BF16)

    w_mid = jnp.tanh(jnp.dot(xm(3), w1_ref[...], preferred_element_type=F32))
    a_mid = jnp.dot(xm(4), a1_ref[...], preferred_element_type=F32)
    g_mid = _sigmoid(jnp.dot(xm(5), g1_ref[...], preferred_element_type=F32))
    r = jnp.dot(xm(0), wrkv_ref[0], preferred_element_type=F32)
    k = jnp.dot(xm(1), wrkv_ref[1], preferred_element_type=F32)
    v = jnp.dot(xm(2), wrkv_ref[2], preferred_element_type=F32)
    wl = w0_ref[...] + _dot(w_mid, w2_ref[...])
    sp = jnp.maximum(-wl, 0.0) + jnp.log(1.0 + jnp.exp(-jnp.abs(wl)))
    lw = -jnp.exp(-sp - 0.5)
    a = _sigmoid(a0_ref[...] + _dot(a_mid, a2_ref[...]))
    g = _dot(g_mid, g2_ref[...])
    kk = k * kk_ref[...]
    n2 = _seg_sum(kk * kk, segr_ref[...], segb_ref[...])
    kk = kk / jnp.maximum(jnp.sqrt(n2), 1e-12)
    outs = (r, lw, k * (1.0 + (a - 1.0) * ka_ref[...]), v, -kk, kk * a, g)
    for o_ref, val in zip((r_o, lw_o, k_o, v_o, a_o, b_o, g_o), outs):
        if block_mode:
            o_ref[0] = val.T
        else:
            o_ref[...] = val


def _rwkv_proj(x, seq_len, init, w):
    n, d = x.shape
    block_mode = init is not None
    consts = [w['gmix'], w['mix'], w['wrkv'], w['w0'], w['w1'], w['w2'], w['a0'], w['a1'], w['a2'],
              w['g1'], w['g2'], w['kk'], w['ka'], w['segr'], w['segb']]
    if block_mode:
        tm = n // seq_len
        tiles_per_seq = 1
        row = pl.BlockSpec((tm, d), lambda i: (i, 0))
        in_specs = [row, _const_spec(init.shape)] + [_const_spec(c.shape) for c in consts]
        args = [x, init] + consts
        out_shape = [jax.ShapeDtypeStruct((seq_len, d, tm), F32)] * 7 + [jax.ShapeDtypeStruct((tm, d), F32)]
        out_specs = [pl.BlockSpec((1, d, tm), lambda i: (i, 0, 0))] * 7 + [_const_spec((tm, d))]
        scratch = [pltpu.VMEM((tm, d), F32)]
    else:
        tm = min(PROJ_ROW_TILE, seq_len)
        assert seq_len % tm == 0 and n % seq_len == 0
        tiles_per_seq = seq_len // tm
        row = pl.BlockSpec((tm, d), lambda i: (i, 0))
        in_specs = [row] + [_const_spec(c.shape) for c in consts]
        args = [x] + consts
        out_shape = [jax.ShapeDtypeStruct((n, d), F32)] * 7 + [jax.ShapeDtypeStruct((n // seq_len, 1, d), F32)]
        out_specs = [row] * 7 + [pl.BlockSpec((1, 1, d), lambda i: (i // tiles_per_seq, 0, 0))]
        scratch = [pltpu.VMEM((SUBLANES, d), F32)]
    return pl.pallas_call(
        functools.partial(_rwkv_proj_kernel, block_mode, tiles_per_seq),
        out_shape=out_shape,
        grid=(n // tm,),
        in_specs=in_specs,
        out_specs=out_specs,
        scratch_shapes=scratch,
        compiler_params=_cparams(1),
        name="rwkv_proj",
    )(*args)


def _wkv_masks(c):
    n = 2 * c
    row = lax.broadcasted_iota(jnp.int32, (n, n), 0)
    col = lax.broadcasted_iota(jnp.int32, (n, n), 1)
    rq = row >= c
    cq = col >= c
    tl = jnp.logical_and(jnp.logical_not(rq), jnp.logical_not(cq))
    br = jnp.logical_and(rq, cq)
    tr = jnp.logical_and(jnp.logical_not(rq), cq)
    bl = jnp.logical_and(rq, jnp.logical_not(cq))
    strict = (col % c) < (row % c)
    incl = (col % c) <= (row % c)
    one = jnp.ones((n, n), F32)
    zero = jnp.zeros((n, n), F32)

    def f(m):
        return jnp.where(m, one, zero)

    return dict(
        ab0=f(jnp.logical_and(strict, tl)), ab1=f(jnp.logical_and(strict, br)),
        ak0=f(jnp.logical_and(strict, tr)), ak1=f(jnp.logical_and(strict, bl)),
        rb0=f(jnp.logical_and(incl, bl)), rb1=f(jnp.logical_and(incl, tr)),
        rk0=f(jnp.logical_and(incl, br)), rk1=f(jnp.logical_and(incl, tl)),
        eye=f(row == col),
    )


def _wkv_chunk(c, n_rounds, units, msk, m0, m1, bdmask):
    nu = range(len(units))
    pre = []
    row = lax.broadcasted_iota(jnp.int32, (c, 1), 0)
    for r, lw, k, v, a, b, state in units:
        g_inc = lw
        shift = 1
        while shift < c:
            g_inc = g_inc + jnp.where(row >= shift, pltpu.roll(g_inc, shift, axis=0), 0.0)
            shift *= 2
        e_inc = jnp.exp(g_inc)
        e_neg = jnp.exp(-g_inc)
        at = a * jnp.exp(g_inc - lw)
        rt = r * e_inc
        bt = b * e_neg
        kt = k * e_neg
        pre.append(dict(
            lr=jnp.concatenate([at, rt], axis=0), rl=jnp.concatenate([rt, at], axis=0),
            bk=jnp.concatenate([bt, kt], axis=0), kb=jnp.concatenate([kt, bt], axis=0),
            vst=jnp.concatenate([v * m1, v * m0], axis=0), v=v, state=state, gc=e_inc[c - 1:c, :]))
    sc0 = [_dot_nt(p['lr'] * m0, p['bk']) for p in pre]
    sc1 = [_dot_nt(p['rl'] * m1, p['kb']) for p in pre]
    xs = [_dot_nt(p['lr'], p['state']) for p in pre]
    ab = [sc0[i] * msk['ab0'] + sc1[i] * msk['ab1'] for i in nu]
    ak = [sc0[i] * msk['ak0'] + sc1[i] * msk['ak1'] for i in nu]
    rbk = [jnp.concatenate([sc0[i] * msk['rb0'] + sc1[i] * msk['rb1'],
                            sc0[i] * msk['rk0'] + sc1[i] * msk['rk1']], axis=1) for i in nu]
    akv = [_dot(ak[i], pre[i]['vst']) for i in nu]
    tinv = [msk['eye'] + ab[i] for i in nu]
    pw = ab
    for _ in range(n_rounds):
        pw = [_dot(pw[i], pw[i]) for i in nu]
        tinv = [tinv[i] + _dot(tinv[i], pw[i]) for i in nu]
    zst = [jnp.concatenate([xs[i][:c] * m0, xs[i][:c] * m1], axis=0) + akv[i] for i in nu]
    w = [_dot(tinv[i], zst[i]) for i in nu]
    ys = [_dot(rbk[i], jnp.concatenate([w[i], pre[i]['vst']], axis=0)) for i in nu]
    ds = [_dot_tn(jnp.concatenate([w[i][:c] + w[i][c:], pre[i]['v']], axis=0), pre[i]['bk']) for i in nu]
    outs = []
    for i in nu:
        y = xs[i][c:] + ys[i][:c] + ys[i][c:]
        outs.append((y, (pre[i]['state'] + ds[i] * bdmask) * pre[i]['gc']))
    return outs


def _wkv_kernel(c, n_chunks, bb, npair, r_ref, lw_ref, k_ref, v_ref, a_ref, b_ref, g_ref, rk_ref, lnw_ref,
                lnb_ref, seg_ref, o_ref, s_out, state, ybuf):
    tstep = pl.program_id(2)
    lane = lax.broadcasted_iota(jnp.int32, (1, LANES), 1)
    m0 = jnp.where(lane < HEAD_DIM, 1.0, 0.0).astype(F32)
    m1 = 1.0 - m0
    rowi = lax.broadcasted_iota(jnp.int32, (LANES, LANES), 0)
    coli = lax.broadcasted_iota(jnp.int32, (LANES, LANES), 1)
    bdmask = jnp.where((rowi >= HEAD_DIM) == (coli >= HEAD_DIM), 1.0, 0.0).astype(F32)
    half = LANES // 2

    @pl.when(tstep == 0)
    def _():
        state[...] = jnp.zeros_like(state)

    msk = _wkv_masks(c)
    n_rounds = max(c.bit_length() - 2, 0)

    def chunk_body(ci, carry):
        t0 = pl.multiple_of(ci * c, c)
        rows = pl.ds(t0, c)
        units = [(ib, ip) for ib in range(bb) for ip in range(npair)]
        ins = []
        for ib, ip in units:
            ls = slice(ip * LANES, (ip + 1) * LANES)
            ins.append((r_ref[ib, rows, ls], lw_ref[ib, rows, ls], k_ref[ib, rows, ls],
                        v_ref[ib, rows, ls], a_ref[ib, rows, ls], b_ref[ib, rows, ls],
                        state[ib * npair + ip]))
        outs = _wkv_chunk(c, n_rounds, ins, msk, m0, m1, bdmask)
        for (ib, ip), (y, ns) in zip(units, outs):
            state[ib * npair + ip] = ns
            ybuf[ib, rows, slice(ip * LANES, (ip + 1) * LANES)] = y
        return carry

    lax.fori_loop(0, n_chunks, chunk_body, 0)

    seg = seg_ref[...]
    for ib in range(bb):
        for ip in range(npair):
            ls = slice(ip * LANES, (ip + 1) * LANES)
            y = ybuf[ib, :, ls]
            mu = _split_dot(y, seg) * (1.0 / HEAD_DIM)
            yc = y - mu
            var = _split_dot(yc * yc, seg) * (1.0 / HEAD_DIM)
            yn = yc * lax.rsqrt(var + GN_EPS) * lnw_ref[:, ls] + lnb_ref[:, ls]
            rr = r_ref[ib, :, ls]
            bonus = _split_dot(rr * k_ref[ib, :, ls] * rk_ref[:, ls], seg) * v_ref[ib, :, ls]
            o_ref[ib, :, ls] = ((yn + bonus) * g_ref[ib, :, ls]).astype(o_ref.dtype)

    @pl.when(tstep == pl.num_programs(2) - 1)
    def _():
        for ib in range(bb):
            for ip in range(npair):
                st = state[ib * npair + ip]
                s_out[ib, ip] = st[:half] + st[half:]


def _wkv(r, lw, k, v, a, b, g, w, chunk, tblk, bb, npair):
    bsz, t, d = r.shape
    np_total = d // LANES
    assert bsz % bb == 0 and np_total % npair == 0 and t % tblk == 0 and tblk % chunk == 0
    blk = pl.BlockSpec((bb, tblk, npair * LANES), lambda ib, ip, it: (ib, it, ip))
    par = pl.BlockSpec((1, npair * LANES), lambda ib, ip, it: (0, ip))
    st_spec = pl.BlockSpec((bb, npair, HEAD_DIM, LANES), lambda ib, ip, it: (ib, ip, 0, 0))
    return pl.pallas_call(
        functools.partial(_wkv_kernel, chunk, tblk // chunk, bb, npair),
        out_shape=[jax.ShapeDtypeStruct((bsz, t, d), BF16),
                   jax.ShapeDtypeStruct((bsz, np_total, HEAD_DIM, LANES), F32)],
        grid=(bsz // bb, np_total // npair, t // tblk),
        in_specs=[blk] * 7 + [par] * 3 + [_const_spec(w['seg2'].shape)],
        out_specs=[blk, st_spec],
        scratch_shapes=[pltpu.VMEM((bb * npair, LANES, LANES), F32),
                        pltpu.VMEM((bb, tblk, npair * LANES), F32)],
        compiler_params=_cparams(3),
        name="wkv",
    )(r, lw, k, v, a, b, g, w['rk'], w['lnw'], w['lnb'], w['seg2'])


def _wkv_decode_kernel(n_tok, hb, r_ref, lw_ref, k_ref, v_ref, a_ref, b_ref, g_ref, rk_ref, lnw_ref, lnb_ref,
                       s_in, o_ref, s_out, w_s, y_s):
    bsz = r_ref.shape[2]
    w_s[...] = jnp.exp(lw_ref[...])
    for h in range(hb):
        rows = slice(h * HEAD_DIM, (h + 1) * HEAD_DIM)

        def body(i, carry, h=h, rows=rows):
            s = s_in[h, i]
            for t in range(n_tok):
                sa = jnp.sum(s * a_ref[t, rows, :], axis=0, keepdims=True)
                vi = v_ref[t, pl.ds(h * HEAD_DIM + i, 1), :]
                s = s * w_s[t, rows, :] + sa * b_ref[t, rows, :] + vi * k_ref[t, rows, :]
                y_s[t, pl.ds(h * HEAD_DIM + i, 1), :] = jnp.sum(s * r_ref[t, rows, :], axis=0, keepdims=True)
            s_out[h, i] = s
            return carry

        lax.fori_loop(0, HEAD_DIM, body, 0)

    for t in range(n_tok):
        outs = []
        for h in range(hb):
            rows = slice(h * HEAD_DIM, (h + 1) * HEAD_DIM)
            y = y_s[t, rows, :]
            mu = jnp.mean(y, axis=0, keepdims=True)
            yc = y - mu
            var = jnp.mean(yc * yc, axis=0, keepdims=True)
            yn = yc * lax.rsqrt(var + GN_EPS) * lnw_ref[rows, :] + lnb_ref[rows, :]
            bonus = jnp.sum(r_ref[t, rows, :] * k_ref[t, rows, :] * rk_ref[rows, :], axis=0,
                            keepdims=True) * v_ref[t, rows, :]
            outs.append((yn + bonus) * g_ref[t, rows, :])
        o_ref[t * bsz:(t + 1) * bsz, :] = jnp.concatenate(outs, axis=0).T.astype(o_ref.dtype)


def _wkv_decode(r, lw, k, v, a, b, g, rk, lnw, lnb, state, hb):
    n_tok, d, bsz = r.shape
    n_heads = d // HEAD_DIM
    assert n_heads % hb == 0
    blk = pl.BlockSpec((n_tok, hb * HEAD_DIM, bsz), lambda i: (0, i, 0))
    par = pl.BlockSpec((hb * HEAD_DIM, bsz), lambda i: (i, 0))
    st = pl.BlockSpec((hb, HEAD_DIM, HEAD_DIM, bsz), lambda i: (i, 0, 0, 0))
    return pl.pallas_call(
        functools.partial(_wkv_decode_kernel, n_tok, hb),
        out_shape=[jax.ShapeDtypeStruct((n_tok * bsz, d), BF16), jax.ShapeDtypeStruct(state.shape, F32)],
        grid=(n_heads // hb,),
        in_specs=[blk] * 7 + [par] * 3 + [st],
        out_specs=[pl.BlockSpec((n_tok * bsz, hb * HEAD_DIM), lambda i: (0, i)), st],
        scratch_shapes=[pltpu.VMEM((n_tok, hb * HEAD_DIM, bsz), F32)] * 2,
        compiler_params=_cparams(1),
        name="wkv_decode",
    )(r, lw, k, v, a, b, g, rk, lnw, lnb, state)


def _ffn_kernel(block_mode, tiles_per_seq, fc, final_norm, *refs):
    refs = list(refs)
    x_ref, o_ref, p_ref = refs[:3]
    pos = 3
    if block_mode:
        st_ref = refs[pos]
        pos += 1
    (wo_ref, gffn_ref, wup_ref, cw_ref, cb_ref, wd_ref, wg_ref, wp_ref, gout_ref) = refs[pos:pos + 9]
    pos += 9
    y_o, u_o = refs[pos:pos + 2]
    pos += 2
    x1_s, hn_s, act_s = refs[pos:pos + 3]
    pos += 3
    i = pl.program_id(0)
    dff = wd_ref.shape[0]
    n_f = dff // fc
    tm = x_ref.shape[0]
    if block_mode:
        c1_s, c2_s = refs[pos:pos + 2]

        @pl.when(i == 0)
        def _():
            c2_s[...] = st_ref[:, 0:2 * dff]
            c1_s[...] = st_ref[:, 2 * dff:4 * dff]
    else:
        c_s = refs[pos]

        @pl.when(i % tiles_per_seq == 0)
        def _():
            c_s[...] = jnp.zeros_like(c_s)

    x1 = x_ref[...] + jnp.dot(o_ref[...], wo_ref[...], preferred_element_type=F32)
    x1_s[...] = x1
    hn_s[...] = _rms(x1, gffn_ref[...]).astype(BF16)

    def up(f):
        hn = hn_s[...]
        return tuple(jnp.dot(hn, wup_ref[:, half * dff + f * fc:half * dff + (f + 1) * fc],
                             preferred_element_type=F32) for half in range(2))

    u_next = up(0)
    for f in range(n_f):
        u_cur = u_next
        if f + 1 < n_f:
            u_next = up(f + 1)
        conv = []
        for half in range(2):
            cols = slice(half * dff + f * fc, half * dff + (f + 1) * fc)
            u = u_cur[half]
            if block_mode:
                u1, u2 = c1_s[:, cols], c2_s[:, cols]
                c2_s[:, cols] = u1
                c1_s[:, cols] = u
                u_o[0, :, cols] = u
            else:
                carry = c_s[:, cols]
                u1, u2 = _prev_rows_carry(u, 1, carry), _prev_rows_carry(u, 2, carry)
                c_s[:, cols] = u[tm - SUBLANES:tm, :]
                u_o[0, :, cols] = u[tm - SUBLANES:tm, :]
            conv.append(cb_ref[:, cols] + cw_ref[0:1, cols] * u2 + cw_ref[1:2, cols] * u1 + cw_ref[2:3, cols] * u)
        gate, val = conv
        act_s[:, f * fc:(f + 1) * fc] = (gate * _sigmoid(gate) * val).astype(BF16)

    x2 = x1_s[...] + jnp.dot(act_s[...], wd_ref[...], preferred_element_type=F32)
    x3 = x2 + _sigmoid(_dot(x2, wg_ref[...])) * _dot(p_ref[...], wp_ref[...])
    if final_norm:
        x3 = _rms(x3, gout_ref[...])
    y_o[...] = x3


def _ffn(x, o, p, layer, seq_len, conv_state, w, final_norm, tm, fc):
    n, d = x.shape
    dff = w['wd'].shape[0]
    pdim = p.shape[2]
    block_mode = conv_state is not None
    assert dff % fc == 0
    if block_mode:
        tm = n // seq_len
        tiles_per_seq = 1
    else:
        tm = min(tm, seq_len)
        assert seq_len % tm == 0 and n % seq_len == 0
        tiles_per_seq = seq_len // tm
    rowd = pl.BlockSpec((tm, d), lambda i: (i, 0))
    rowp = pl.BlockSpec((None, tm, pdim), lambda i: (layer, i, 0))

    def resident(a):
        nd = a.ndim
        return pl.BlockSpec(a.shape, lambda i: (0,) * nd, pipeline_mode=pl.Buffered(1))

    in_specs = [rowd, rowd, rowp]
    args = [x, o, p]
    if block_mode:
        in_specs.append(resident(conv_state))
        args.append(conv_state)
    consts = [w['wo'], w['gffn'], w['wup'], w['cw'], w['cb'], w['wd'], w['wg'], w['wp'], w['gout']]
    in_specs += [resident(c) for c in consts]
    args += consts
    if block_mode:
        u_shape = jax.ShapeDtypeStruct((seq_len, tm, 2 * dff), F32)
        u_spec = pl.BlockSpec((1, tm, 2 * dff), lambda i: (i, 0, 0))
        scratch_c = [pltpu.VMEM((tm, 2 * dff), F32)] * 2
    else:
        u_shape = jax.ShapeDtypeStruct((n // tm, SUBLANES, 2 * dff), F32)
        u_spec = pl.BlockSpec((1, SUBLANES, 2 * dff), lambda i: (i, 0, 0))
        scratch_c = [pltpu.VMEM((SUBLANES, 2 * dff), F32)]
    return pl.pallas_call(
        functools.partial(_ffn_kernel, block_mode, tiles_per_seq, fc, final_norm),
        out_shape=[jax.ShapeDtypeStruct((n, d), F32), u_shape],
        grid=(n // tm,),
        in_specs=in_specs,
        out_specs=[rowd, u_spec],
        scratch_shapes=[pltpu.VMEM((tm, d), F32), pltpu.VMEM((tm, d), BF16), pltpu.VMEM((tm, dff), BF16)] + scratch_c,
        compiler_params=_cparams(1),
        name="ffn",
    )(*args)


def _qkv_kernel(prompt_mode, *refs):
    if prompt_mode:
        (x_ref, g_ref, w_ref, cos_ref, s1_ref, s2_ref, q_o, kt_o, vt_o, k16_o, vt16_o) = refs
    else:
        (x_ref, g_ref, w_ref, cos_ref, s1_ref, s2_ref, q_o, kt_o, vt_o, k_o, v_o) = refs
    hn = _rms(x_ref[...], g_ref[...]).astype(BF16)
    d = hn.shape[1]
    reps = d // LANES
    cos = jnp.concatenate([cos_ref[...]] * reps, axis=1)
    s1 = jnp.concatenate([s1_ref[...]] * reps, axis=1)
    s2 = jnp.concatenate([s2_ref[...]] * reps, axis=1)
    half = ROT_DIM // 2

    def rope(z):
        return z * cos + pltpu.roll(z, d - half, axis=1) * s1 + pltpu.roll(z, half, axis=1) * s2

    q = rope(jnp.dot(hn, w_ref[:, 0:d], preferred_element_type=F32)) * (HEAD_DIM ** -0.5)
    k = rope(jnp.dot(hn, w_ref[:, d:2 * d], preferred_element_type=F32))
    v = jnp.dot(hn, w_ref[:, 2 * d:3 * d], preferred_element_type=F32)
    vt = v.T
    kt_o[0] = k.T
    vt_o[0] = vt
    if prompt_mode:
        q_o[...] = (q * LOG2E).astype(BF16)
        k16_o[...] = k.astype(BF16)
        vt16_o[0] = vt.astype(BF16)
    else:
        q_o[...] = q
        k_o[...] = k
        v_o[...] = v


def _qkv(x, group, gmix, wqkv, cos, s1, s2, prompt_mode, tm):
    n, d = x.shape
    tm = min(tm, group)
    assert group % tm == 0 and n % group == 0
    tpg = group // tm
    row = pl.BlockSpec((tm, d), lambda i: (i, 0))
    rowt = pl.BlockSpec((tm, LANES), lambda i: (i, 0))
    tr = pl.BlockSpec((1, d, tm), lambda i: (i // tpg, 0, i % tpg))
    big = jax.ShapeDtypeStruct((n, d), F32)
    big16 = jax.ShapeDtypeStruct((n, d), BF16)
    bigt = jax.ShapeDtypeStruct((n // group, d, group), F32)
    if prompt_mode:
        out_shape = [big16, bigt, bigt, big16, jax.ShapeDtypeStruct((n // group, d, group), BF16)]
        out_specs = [row, tr, tr, row, tr]
    else:
        out_shape = [big, bigt, bigt, big, big]
        out_specs = [row, tr, tr, row, row]
    return pl.pallas_call(
        functools.partial(_qkv_kernel, prompt_mode),
        out_shape=out_shape,
        grid=(n // tm,),
        in_specs=[row, _const_spec(gmix.shape), _const_spec(wqkv.shape), rowt, rowt, rowt],
        out_specs=out_specs,
        compiler_params=_cparams(1),
        name="qkv",
    )(x, gmix, wqkv, cos, s1, s2)


def _moba_kernel(n_blocks, grp, tiles_per_step, q_ref, k_ref, vt_ref, o_ref, *scratch):
    def tile(t, carry):
        r0 = pl.multiple_of(t * MOBA_BLOCK, MOBA_BLOCK)
        qi = pl.program_id(2) * tiles_per_step + t
        o_ref[0, pl.ds(r0, MOBA_BLOCK), :] = _moba_tile(
            n_blocks, grp, qi, q_ref[0, pl.ds(r0, MOBA_BLOCK), :], k_ref, vt_ref, *scratch).astype(o_ref.dtype)
        return carry

    lax.fori_loop(0, tiles_per_step, tile, 0)


def _moba_tile(n_blocks, grp, qi, q, k_ref, vt_ref, o_s, m_s, l_s, g_s, w_s):
    blk = MOBA_BLOCK
    sub = min(MOBA_SUBGROUP, grp)
    lane = lax.broadcasted_iota(jnp.int32, (1, LANES), 1)
    zero16 = jnp.zeros_like(q)
    qh = [jnp.where(lane < HEAD_DIM, q, zero16), jnp.where(lane >= HEAD_DIM, q, zero16)]
    ones_rows = jnp.where(lax.broadcasted_iota(jnp.int32, (2 * SUBLANES, blk), 0) == 0, 1.0, 0.0).astype(BF16)
    neg_inf = jnp.float32(-jnp.inf)
    for h in range(2):
        m_s[h] = jnp.full((n_blocks, blk), NEG_BIG, F32)
        l_s[h] = jnp.zeros((n_blocks, blk), F32)
        g_s[h] = jnp.full((n_blocks, blk), neg_inf, F32)

    def scores(j0, nb):
        k0 = pl.multiple_of(j0 * blk, blk)
        kg = k_ref[0, pl.ds(k0, nb * blk), :]
        return [lax.dot_general(kg, qh[h], (((1,), (1,)), ((), ())), preferred_element_type=F32)
                for h in range(2)]

    def partials(j0, nb, st, diag):
        parts = []
        for h in range(2):
            for jb in range(nb):
                s = st[h][jb * blk:(jb + 1) * blk]
                gate = jnp.sum(s, axis=0, keepdims=True) * (1.0 / blk)
                s = s.astype(BF16)
                if diag:
                    kr = lax.broadcasted_iota(jnp.int32, (blk, blk), 0)
                    qc = lax.broadcasted_iota(jnp.int32, (blk, blk), 1)
                    s = jnp.where(kr <= qc, s, jnp.full_like(s, NEG_BIG))
                mj = jnp.max(s, axis=0, keepdims=True)
                p = jnp.exp2(s - mj)
                parts.append((h, jb, gate, mj.astype(F32), p))
        pvs = []
        for h, jb, gate, mj, p in parts:
            kb0 = pl.multiple_of((j0 + jb) * blk, blk)
            vt_ext = jnp.concatenate([vt_ref[0, h * HEAD_DIM:(h + 1) * HEAD_DIM, pl.ds(kb0, blk)], ones_rows],
                                     axis=0)
            pvs.append(jnp.dot(vt_ext, p, preferred_element_type=F32))
        for (h, jb, gate, mj, p), pv in zip(parts, pvs):
            j = j0 + jb
            o_s[h * n_blocks + j] = pv[:HEAD_DIM]
            l_s[h, pl.ds(j, 1), :] = pv[HEAD_DIM:HEAD_DIM + 1]
            m_s[h, pl.ds(j, 1), :] = mj
            g_s[h, pl.ds(j, 1), :] = gate

    def run_blocks(j0, nb):
        st = scores(j0, sub)
        for s in range(1, nb // sub):
            st_next = scores(j0 + s * sub, sub)
            partials(j0 + (s - 1) * sub, sub, st, False)
            st = st_next
        partials(j0 + nb - sub, sub, st, False)

    def group_body(gi, carry):
        run_blocks(gi * grp, grp)
        return carry

    n_full = qi // grp
    rem = qi - n_full * grp
    lax.fori_loop(0, n_full, group_body, 0)
    half = grp // 2
    if half >= sub:
        @pl.when(jnp.logical_and(rem > 0, rem <= half))
        def _():
            run_blocks(n_full * grp, half)

    @pl.when(rem > (half if half >= sub else 0))
    def _():
        run_blocks(n_full * grp, grp)

    partials(qi, 1, scores(qi, 1), True)

    jrow = lax.broadcasted_iota(jnp.int32, (n_blocks, 1), 0)
    jrow_f = jrow.astype(F32)
    lsum = []
    for h in range(2):
        cur = jnp.where(jrow < qi, g_s[h], neg_inf)
        sel = jrow == qi
        for _ in range(MOBA_TOPK):
            mx = jnp.max(cur, axis=0, keepdims=True)
            first = jnp.min(jnp.where(cur == mx, jrow_f, float(n_blocks)), axis=0, keepdims=True)
            onehot = jrow_f == first
            sel = jnp.logical_or(sel, jnp.logical_and(onehot, mx > neg_inf))
            cur = jnp.where(onehot, neg_inf, cur)
        m = m_s[h]
        mx = jnp.max(jnp.where(sel, m, NEG_BIG), axis=0, keepdims=True)
        w = jnp.where(sel, jnp.exp2(m - mx), 0.0)
        w_s[h] = w
        lsum.append(jnp.sum(w * l_s[h], axis=0, keepdims=True))

    def merge(j, accs):
        return tuple(accs[h] + w_s[h, pl.ds(j, 1), :] * o_s[h * n_blocks + j] for h in range(2))

    accs = lax.fori_loop(0, qi + 1, merge, tuple(jnp.zeros((HEAD_DIM, blk), F32) for _ in range(2)))
    return jnp.concatenate([accs[h] / lsum[h] for h in range(2)], axis=0).T


def _moba_prompt(q16, k16, vt16, grp, tiles_per_step):
    bsz, t, d = q16.shape
    n_blocks = t // MOBA_BLOCK
    assert n_blocks % grp == 0 and n_blocks % tiles_per_step == 0
    stat = pltpu.VMEM((2, n_blocks, MOBA_BLOCK), F32)
    rows = tiles_per_step * MOBA_BLOCK
    return pl.pallas_call(
        functools.partial(_moba_kernel, n_blocks, grp, tiles_per_step),
        out_shape=jax.ShapeDtypeStruct((bsz, t, d), BF16),
        grid=(bsz, d // LANES, n_blocks // tiles_per_step),
        in_specs=[pl.BlockSpec((1, rows, LANES), lambda b, p, i: (b, i, p)),
                  pl.BlockSpec((1, t, LANES), lambda b, p, i: (b, 0, p)),
                  pl.BlockSpec((1, LANES, t), lambda b, p, i: (b, p, 0))],
        out_specs=pl.BlockSpec((1, rows, LANES), lambda b, p, i: (b, i, p)),
        scratch_shapes=[pltpu.VMEM((2 * n_blocks, HEAD_DIM, MOBA_BLOCK), F32), stat, stat, stat, stat],
        compiler_params=_cparams(3),
        name="moba_prompt",
    )(q16, k16, vt16)


def _moba_decode_kernel(n_tok, n_heads, pages_per_step, n_steps, pt_ref, q_ref, kn_ref, vn_ref, *refs):
    kp = refs[:pages_per_step]
    vp = refs[pages_per_step:2 * pages_per_step]
    o_ref = refs[2 * pages_per_step]
    m_s, l_s, g_s, acc_s = refs[2 * pages_per_step + 1:]
    step = pl.program_id(1)
    d = q_ref.shape[2]
    page = kp[0].shape[2]
    pages_per_blk = MOBA_BLOCK // page
    blk_per_step = pages_per_step // pages_per_blk
    n_past = n_steps * blk_per_step
    nrow = n_tok * n_heads
    q = q_ref[0]
    lane_head = lax.broadcasted_iota(jnp.int32, (n_heads, d), 1) // HEAD_DIM
    row_head = lax.broadcasted_iota(jnp.int32, (n_heads, d), 0)
    hmask = lane_head == row_head
    qbd = jnp.concatenate(
        [jnp.where(hmask, jnp.broadcast_to(q[t:t + 1, :], (n_heads, d)), 0.0) for t in range(n_tok)], axis=0)
    qbd16 = qbd.astype(BF16)

    scores = [jnp.concatenate([_dot(qbd16, kp[jb * pages_per_blk + i][0]) for i in range(pages_per_blk)], axis=1)
              for jb in range(blk_per_step)]
    stats = []
    for s in scores:
        gate = jnp.sum(s, axis=1, keepdims=True) * (1.0 / MOBA_BLOCK)
        mj = jnp.max(s, axis=1, keepdims=True)
        p = jnp.exp(s - mj)
        stats.append((gate, mj, jnp.sum(p, axis=1, keepdims=True), p.astype(BF16)))
    outs = [sum(_dot_nt(p[:, i * page:(i + 1) * page], vp[jb * pages_per_blk + i][0]) for i in range(pages_per_blk))
            for jb, (_, _, _, p) in enumerate(stats)]
    for jb, ((gate, mj, lj, _), oj) in enumerate(zip(stats, outs)):
        slot = step * blk_per_step + jb
        m_s[slot] = jnp.broadcast_to(mj, (nrow, LANES))
        l_s[slot] = jnp.broadcast_to(lj, (nrow, LANES))
        g_s[slot] = jnp.broadcast_to(gate, (nrow, LANES))
        acc_s[slot] = oj

    @pl.when(step == n_steps - 1)
    def _():
        gates = [g_s[j][:, 0:1] for j in range(n_past)]
        kn = kn_ref[0]
        vn = vn_ref[0]
        trow = lax.broadcasted_iota(jnp.int32, (nrow, 1), 0) // n_heads
        s_own = [jnp.where(trow >= t, jnp.sum(qbd * kn[t:t + 1, :], axis=1, keepdims=True), NEG_BIG)
                 for t in range(n_tok)]
        m = functools.reduce(jnp.maximum, s_own)
        p_own = [jnp.exp(s - m) for s in s_own]
        l = sum(p_own)
        acc = sum(p_own[t] * vn[t:t + 1, :] for t in range(n_tok))
        for j in range(n_past):
            rank = sum(jnp.where(jnp.logical_or(gates[i] > gates[j],
                                                jnp.logical_and(gates[i] == gates[j], i < j)), 1.0, 0.0)
                       for i in range(n_past) if i != j)
            selj = rank < float(MOBA_TOPK)
            mj = m_s[j][:, 0:1]
            m_new = jnp.where(selj, jnp.maximum(m, mj), m)
            alpha = jnp.exp(m - m_new)
            beta = jnp.where(selj, jnp.exp(mj - m_new), 0.0)
            l = l * alpha + l_s[j][:, 0:1] * beta
            acc = acc * alpha + acc_s[j] * beta
            m = m_new
        lane_head_r = lax.broadcasted_iota(jnp.int32, (nrow, d), 1) // HEAD_DIM
        row_head_r = lax.broadcasted_iota(jnp.int32, (nrow, d), 0) % n_heads
        out = jnp.where(lane_head_r == row_head_r, acc / l, 0.0)
        o_ref[0] = jnp.concatenate(
            [jnp.sum(out[t * n_heads:(t + 1) * n_heads], axis=0, keepdims=True) for t in range(n_tok)],
            axis=0).astype(o_ref.dtype)


def _moba_decode(q, kn, vn, cache_kt, cache_vt, page_table, pages_per_step):
    bsz, n_tok, d = q.shape
    n_heads = d // HEAD_DIM
    pool, _, page = cache_kt.shape
    n_pages = page_table.shape[1]
    assert n_pages % pages_per_step == 0 and MOBA_BLOCK % page == 0
    assert (n_pages * page) % MOBA_BLOCK == 0 and n_tok <= MOBA_BLOCK
    assert pages_per_step % (MOBA_BLOCK // page) == 0
    n_steps = n_pages // pages_per_step
    n_past = n_pages * page // MOBA_BLOCK
    nrow = n_tok * n_heads
    tok = pl.BlockSpec((1, n_tok, d), lambda b, s, pt: (b, 0, 0))

    def page_spec(i):
        return pl.BlockSpec((1, d, page), lambda b, s, pt: (pt[b, s * pages_per_step + i], 0, 0))

    grid_spec = pltpu.PrefetchScalarGridSpec(
        num_scalar_prefetch=1,
        grid=(bsz, n_steps),
        in_specs=[tok, tok, tok] + [page_spec(i) for i in range(pages_per_step)] * 2,
        out_specs=tok,
        scratch_shapes=[pltpu.VMEM((n_past, nrow, LANES), F32)] * 3 + [pltpu.VMEM((n_past, nrow, d), F32)],
    )
    return pl.pallas_call(
        functools.partial(_moba_decode_kernel, n_tok, n_heads, pages_per_step, n_steps),
        out_shape=jax.ShapeDtypeStruct((bsz, n_tok, d), BF16),
        grid_spec=grid_spec,
        compiler_params=_cparams(2),
        name="moba_decode",
    )(page_table, q, kn, vn, *([cache_kt] * pages_per_step), *([cache_vt] * pages_per_step))


def _pad_cols(w, n):
    return jnp.pad(w, ((0, 0), (0, n - w.shape[1])))


def _pad_rows(w, n):
    return jnp.pad(w, ((0, n - w.shape[0]), (0, 0)))


def _rope_tables(pos):
    half = ROT_DIM // 2
    inv = ROPE_THETA ** (-2.0 * jnp.arange(half, dtype=F32) / ROT_DIM)
    ang = pos.astype(F32)[:, None] * inv[None, :]
    cos = jnp.cos(ang)
    sin = jnp.sin(ang)
    t = pos.shape[0]
    ones = jnp.ones((t, HEAD_DIM - ROT_DIM), F32)
    zeros_r = jnp.zeros((t, HEAD_DIM - ROT_DIM), F32)
    zeros_h = jnp.zeros((t, half), F32)
    c = jnp.concatenate([cos, cos, ones], axis=1)
    s1 = jnp.concatenate([-sin, zeros_h, zeros_r], axis=1)
    s2 = jnp.concatenate([zeros_h, sin, zeros_r], axis=1)
    rep = LANES // HEAD_DIM
    return jnp.tile(c, (1, rep)), jnp.tile(s1, (1, rep)), jnp.tile(s2, (1, rep))


def _unpack_state(s):
    b, hp, dv, dk2 = s.shape
    return s.reshape(b, hp, dv, 2, dk2 // 2).transpose(0, 1, 3, 2, 4).reshape(b, hp * 2, dv, dk2 // 2)


def kernel(x_prompt, x_sample, state_wkv, state_shift, cache_k, cache_v, state_conv, page_table, p_prompt, p_sample, rw_mix, rw_rkv, rw_w0, rw_w1, rw_w2, rw_a0, rw_a1, rw_a2, rw_g1, rw_g2, rw_kk, rw_ka, rw_rk, rw_lnw, rw_lnb, rw_wo, mb_wqkv, mb_wo, norm_mix, norm_ffn, ff_wup, ff_conv_w, ff_conv_b, ff_wdown, ple_wp, ple_wg, norm_out):
    bp, tp, d = x_prompt.shape
    bs, ts, _ = x_sample.shape
    depth = norm_mix.shape[0]
    n_heads = d // HEAD_DIM
    dff = ff_wdown.shape[1]
    pool, page = cache_k.shape[1], cache_k.shape[2]
    past_len = page_table.shape[1] * page
    assert d % LANES == 0 and depth == 2 and ts >= 2

    lane_head = jnp.arange(d) // HEAD_DIM
    segr = (lane_head[:, None] == jnp.arange(LANES)[None, :]).astype(BF16)
    segb = segr.T
    seg2 = ((jnp.arange(LANES) // HEAD_DIM)[:, None] == (jnp.arange(LANES) // HEAD_DIM)[None, :]).astype(BF16)

    def row(v):
        return v.reshape(1, -1).astype(F32)

    def col(v):
        return jnp.broadcast_to(v.reshape(-1, 1).astype(F32), (v.size, bs))

    rw = dict(
        gmix=row(norm_mix[0]), mix=rw_mix[0], wrkv=rw_rkv[0].astype(BF16), w0=row(rw_w0[0]),
        w1=_pad_cols(rw_w1[0], LANES).astype(BF16), w2=_pad_rows(rw_w2[0], LANES).astype(BF16),
        a0=row(rw_a0[0]), a1=_pad_cols(rw_a1[0], LANES).astype(BF16), a2=_pad_rows(rw_a2[0], LANES).astype(BF16),
        g1=_pad_cols(rw_g1[0], 2 * LANES).astype(BF16), g2=_pad_rows(rw_g2[0], 2 * LANES).astype(BF16),
        kk=row(rw_kk[0]), ka=row(rw_ka[0]), segr=segr, segb=segb,
        rk=row(rw_rk[0]), lnw=row(rw_lnw[0]), lnb=row(rw_lnb[0]), seg2=seg2)

    def ffw(i, wo):
        return dict(wo=wo.astype(BF16), gffn=row(norm_ffn[i]), wup=ff_wup[i].astype(BF16), cw=ff_conv_w[i],
                    cb=row(ff_conv_b[i]), wd=ff_wdown[i].astype(BF16), wg=ple_wg[i].astype(BF16),
                    wp=ple_wp[i].astype(BF16), gout=row(norm_out))

    fw0 = ffw(0, rw_wo[0])
    fw1 = ffw(1, mb_wo[0])
    wqkv = mb_wqkv[0].astype(BF16)
    gmix1 = row(norm_mix[1])

    xp = x_prompt.reshape(bp * tp, d)
    pp = p_prompt.reshape(depth, bp * tp, -1)
    xs = x_sample.transpose(1, 0, 2).reshape(ts * bs, d)
    ps = p_sample.transpose(0, 2, 1, 3).reshape(depth, ts * bs, -1)

    def conv_state_p(u):
        return u.reshape(bp, -1, SUBLANES, 2 * dff)[:, -1, SUBLANES - 2:]

    def conv_state_s(u):
        return u[ts - 2:].transpose(1, 0, 2)

    r, lw, k, v, a, b, g, shift_p = _rwkv_proj(xp, tp, None, rw)
    sh = lambda z: z.reshape(bp, tp, d)
    o_p, wkv_p = _wkv(sh(r), sh(lw), sh(k), sh(v), sh(a), sh(b), sh(g), rw,
                      chunk=WKV_CHUNK, tblk=min(WKV_TIME_BLOCK, tp), bb=bp, npair=WKV_PAIRS_PER_STEP)
    xp, up_ = _ffn(xp, o_p.reshape(bp * tp, d), pp, 0, tp, None, fw0, False, FFN_ROW_TILE, FFN_CHUNK)
    conv_p0 = conv_state_p(up_)

    r, lw, k, v, a, b, g, shift_s = _rwkv_proj(xs, ts, state_shift[0], rw)
    st_in = jnp.transpose(state_wkv[0], (1, 2, 3, 0))
    o_s, st_out = _wkv_decode(r, lw, k, v, a, b, g, col(rw_rk[0]), col(rw_lnw[0]), col(rw_lnb[0]), st_in,
                              WKV_DECODE_HEADS)
    wkv_s = jnp.transpose(st_out, (3, 0, 1, 2))
    xs, us_ = _ffn(xs, o_s, ps, 0, ts, state_conv[0].reshape(bs, 4 * dff), fw0, False, FFN_ROW_TILE, FFN_CHUNK)
    conv_s0 = conv_state_s(us_)

    cos, s1, s2 = _rope_tables(jnp.arange(tp, dtype=jnp.int32))
    cos_p, s1_p, s2_p = (jnp.tile(z, (bp, 1)) for z in (cos, s1, s2))
    q16, kt, vt, k16, vt16 = _qkv(xp, tp, gmix1, wqkv, cos_p, s1_p, s2_p, True, PROJ_ROW_TILE)
    k_p = kt.reshape(bp, n_heads, HEAD_DIM, tp).transpose(0, 3, 1, 2)
    v_p = vt.reshape(bp, n_heads, HEAD_DIM, tp).transpose(0, 3, 1, 2)
    o_p = _moba_prompt(q16.reshape(bp, tp, d), k16.reshape(bp, tp, d), vt16,
                       min(MOBA_GROUP, tp // MOBA_BLOCK), min(MOBA_TILES_PER_STEP, tp // MOBA_BLOCK))
    xp, up_ = _ffn(xp, o_p.reshape(bp * tp, d), pp, 1, tp, None, fw1, True, FFN_ROW_TILE, FFN_CHUNK)
    conv_p1 = conv_state_p(up_)

    cos, s1, s2 = _rope_tables(past_len + jnp.arange(ts, dtype=jnp.int32))
    cos_s, s1_s, s2_s = (jnp.repeat(z, bs, axis=0) for z in (cos, s1, s2))
    q, kt, vt, k, v = _qkv(xs, bs, gmix1, wqkv, cos_s, s1_s, s2_s, False, PROJ_ROW_TILE)
    k_s = kt.reshape(ts, n_heads, HEAD_DIM, bs).transpose(3, 0, 1, 2)
    v_s = vt.reshape(ts, n_heads, HEAD_DIM, bs).transpose(3, 0, 1, 2)
    by_seq = lambda z: z.reshape(ts, bs, d).transpose(1, 0, 2)
    cache_kt = jnp.transpose(cache_k[0], (0, 2, 3, 1)).reshape(pool, d, page)
    cache_vt = jnp.transpose(cache_v[0], (0, 2, 3, 1)).reshape(pool, d, page)
    o_s = _moba_decode(by_seq(q), by_seq(k), by_seq(v), cache_kt, cache_vt, page_table,
                       min(DECODE_PAGES_PER_STEP, page_table.shape[1]))
    o_s = o_s.transpose(1, 0, 2).reshape(ts * bs, d)
    xs, us_ = _ffn(xs, o_s, ps, 1, ts, state_conv[1].reshape(bs, 4 * dff), fw1, True, FFN_ROW_TILE, FFN_CHUNK)
    conv_s1 = conv_state_s(us_)

    return (xp.reshape(bp, tp, d), by_seq(xs),
            _unpack_state(wkv_p)[None], shift_p.reshape(1, bp, d), k_p[None], v_p[None],
            jnp.stack([conv_p0, conv_p1]),
            wkv_s[None], shift_s[None], k_s[None], v_s[None],
            jnp.stack([conv_s0, conv_s1]))
```

```python
import functools

import jax
import jax.numpy as jnp
from jax import lax
from jax.experimental import pallas as pl
from jax.experimental.pallas import tpu as pltpu

F32 = jnp.float32
BF16 = jnp.bfloat16

HEAD_DIM = 64
GN_EPS = 64e-5
RMS_EPS = 1e-6
MOBA_BLOCK = 256
MOBA_TOPK = 3
ROPE_THETA = 500000.0
ROT_DIM = HEAD_DIM // 4
NEG_BIG = -1e30
LOG2E = 1.4426950408889634
LANES = 128
SUBLANES = 8
VMEM_LIMIT_BYTES = 56 * 1024 * 1024

PROJ_ROW_TILE = 256
FFN_ROW_TILE = 512
FFN_CHUNK = 256
WKV_CHUNK = 64
WKV_TIME_BLOCK = 256
WKV_PAIRS_PER_STEP = 8
WKV_DECODE_HEADS = 2
MOBA_GROUP = 8
MOBA_SUBGROUP = 2
MOBA_TILES_PER_STEP = 4
DECODE_PAGES_PER_STEP = 8


def _cparams(n_axes):
    return pltpu.CompilerParams(dimension_semantics=("arbitrary",) * n_axes,
                                vmem_limit_bytes=VMEM_LIMIT_BYTES)


def _const_spec(shape):
    nd = len(shape)
    return pl.BlockSpec(shape, lambda *_: (0,) * nd)


def _dot(a, b):
    return jnp.dot(a.astype(BF16), b.astype(BF16), preferred_element_type=F32)


def _dot_nt(a, b):
    return lax.dot_general(a.astype(BF16), b.astype(BF16), (((1,), (1,)), ((), ())),
                           preferred_element_type=F32)


def _dot_tn(a, b):
    return lax.dot_general(a.astype(BF16), b.astype(BF16), (((0,), (0,)), ((), ())),
                           preferred_element_type=F32)


def _split_dot(x, m):
    hi = x.astype(BF16)
    lo = (x - hi.astype(F32)).astype(BF16)
    return (jnp.dot(hi, m, preferred_element_type=F32) + jnp.dot(lo, m, preferred_element_type=F32))


def _sigmoid(z):
    return 1.0 / (1.0 + jnp.exp(-z))


def _rms(x, g):
    return x * lax.rsqrt(jnp.mean(x * x, axis=-1, keepdims=True) + RMS_EPS) * g


def _seg_sum(x, segr, segb):
    return _split_dot(_split_dot(x, segr), segb)


def _prev_rows_carry(u, k, carry):
    rolled = pltpu.roll(u, k, axis=0)
    row = lax.broadcasted_iota(jnp.int32, (u.shape[0], 1), 0)
    out = rolled
    for j in range(k):
        out = jnp.where(row == j, carry[SUBLANES - k + j:SUBLANES - k + j + 1, :], out)
    return out


def _rwkv_proj_kernel(block_mode, tiles_per_seq, *refs):
    if block_mode:
        (x_ref, init_ref, gmix_ref, mix_ref, wrkv_ref, w0_ref, w1_ref, w2_ref, a0_ref, a1_ref,
         a2_ref, g1_ref, g2_ref, kk_ref, ka_ref, segr_ref, segb_ref,
         r_o, lw_o, k_o, v_o, a_o, b_o, g_o, xn_o, carry) = refs
    else:
        (x_ref, gmix_ref, mix_ref, wrkv_ref, w0_ref, w1_ref, w2_ref, a0_ref, a1_ref, a2_ref,
         g1_ref, g2_ref, kk_ref, ka_ref, segr_ref, segb_ref,
         r_o, lw_o, k_o, v_o, a_o, b_o, g_o, xn_o, carry) = refs
    xn = _rms(x_ref[...], gmix_ref[...])
    tm = xn.shape[0]
    if block_mode:
        @pl.when(pl.program_id(0) == 0)
        def _():
            carry[...] = init_ref[...]
        xprev = carry[...]
        carry[...] = xn
        xn_o[...] = xn
    else:
        @pl.when(pl.program_id(0) % tiles_per_seq == 0)
        def _():
            carry[...] = jnp.zeros_like(carry)
        xprev = _prev_rows_carry(xn, 1, carry)
        carry[...] = xn[tm - SUBLANES:tm, :]
        xn_o[0] = xn[tm - 1:tm, :]
    dx = xprev - xn

    def xm(i):
        return (xn + dx * mix_ref[i:i + 1, :]).astype(BF16)

    w_mid = jnp.tanh(jnp.dot(xm(3), w1_ref[...], preferred_element_type=F32))
    a_mid = jnp.dot(xm(4), a1_ref[...], preferred_element_type=F32)
    g_mid = _sigmoid(jnp.dot(xm(5), g1_ref[...], preferred_element_type=F32))
    r = jnp.dot(xm(0), wrkv_ref[0], preferred_element_type=F32)
    k = jnp.dot(xm(1), wrkv_ref[1], preferred_element_type=F32)
    v = jnp.dot(xm(2), wrkv_ref[2], preferred_element_type=F32)
    wl = w0_ref[...] + _dot(w_mid, w2_ref[...])
    sp = jnp.maximum(-wl, 0.0) + jnp.log(1.0 + jnp.exp(-jnp.abs(wl)))
    lw = -jnp.exp(-sp - 0.5)
    a = _sigmoid(a0_ref[...] + _dot(a_mid, a2_ref[...]))
    g = _dot(g_mid, g2_ref[...])
    kk = k * kk_ref[...]
    n2 = _seg_sum(kk * kk, segr_ref[...], segb_ref[...])
    kk = kk / jnp.maximum(jnp.sqrt(n2), 1e-12)
    outs = (r, lw, k * (1.0 + (a - 1.0) * ka_ref[...]), v, -kk, kk * a, g)
    for o_ref, val in zip((r_o, lw_o, k_o, v_o, a_o, b_o, g_o), outs):
        if block_mode:
            o_ref[0] = val.T
        else:
            o_ref[...] = val


def _rwkv_proj(x, seq_len, init, w):
    n, d = x.shape
    block_mode = init is not None
    consts = [w['gmix'], w['mix'], w['wrkv'], w['w0'], w['w1'], w['w2'], w['a0'], w['a1'], w['a2'],
              w['g1'], w['g2'], w['kk'], w['ka'], w['segr'], w['segb']]
    if block_mode:
        tm = n // seq_len
        tiles_per_seq = 1
        row = pl.BlockSpec((tm, d), lambda i: (i, 0))
        in_specs = [row, _const_spec(init.shape)] + [_const_spec(c.shape) for c in consts]
        args = [x, init] + consts
        out_shape = [jax.ShapeDtypeStruct((seq_len, d, tm), F32)] * 7 + [jax.ShapeDtypeStruct((tm, d), F32)]
        out_specs = [pl.BlockSpec((1, d, tm), lambda i: (i, 0, 0))] * 7 + [_const_spec((tm, d))]
        scratch = [pltpu.VMEM((tm, d), F32)]
    else:
        tm = min(PROJ_ROW_TILE, seq_len)
        assert seq_len % tm == 0 and n % seq_len == 0
        tiles_per_seq = seq_len // tm
        row = pl.BlockSpec((tm, d), lambda i: (i, 0))
        in_specs = [row] + [_const_spec(c.shape) for c in consts]
        args = [x] + consts
        out_shape = [jax.ShapeDtypeStruct((n, d), F32)] * 7 + [jax.ShapeDtypeStruct((n // seq_len, 1, d), F32)]
        out_specs = [row] * 7 + [pl.BlockSpec((1, 1, d), lambda i: (i // tiles_per_seq, 0, 0))]
        scratch = [pltpu.VMEM((SUBLANES, d), F32)]
    return pl.pallas_call(
        functools.partial(_rwkv_proj_kernel, block_mode, tiles_per_seq),
        out_shape=out_shape,
        grid=(n // tm,),
        in_specs=in_specs,
        out_specs=out_specs,
        scratch_shapes=scratch,
        compiler_params=_cparams(1),
        name="rwkv_proj",
    )(*args)


def _wkv_masks(c):
    n = 2 * c
    row = lax.broadcasted_iota(jnp.int32, (n, n), 0)
    col = lax.broadcasted_iota(jnp.int32, (n, n), 1)
    rq = row >= c
    cq = col >= c
    tl = jnp.logical_and(jnp.logical_not(rq), jnp.logical_not(cq))
    br = jnp.logical_and(rq, cq)
    tr = jnp.logical_and(jnp.logical_not(rq), cq)
    bl = jnp.logical_and(rq, jnp.logical_not(cq))
    strict = (col % c) < (row % c)
    incl = (col % c) <= (row % c)
    one = jnp.ones((n, n), F32)
    zero = jnp.zeros((n, n), F32)

    def f(m):
        return jnp.where(m, one, zero)

    return dict(
        ab0=f(jnp.logical_and(strict, tl)), ab1=f(jnp.logical_and(strict, br)),
        ak0=f(jnp.logical_and(strict, tr)), ak1=f(jnp.logical_and(strict, bl)),
        rb0=f(jnp.logical_and(incl, bl)), rb1=f(jnp.logical_and(incl, tr)),
        rk0=f(jnp.logical_and(incl, br)), rk1=f(jnp.logical_and(incl, tl)),
        eye=f(row == col),
    )


def _wkv_chunk(c, n_rounds, units, msk, m0, m1, bdmask):
    nu = range(len(units))
    pre = []
    row = lax.broadcasted_iota(jnp.int32, (c, 1), 0)
    for r, lw, k, v, a, b, state in units:
        g_inc = lw
        shift = 1
        while shift < c:
            g_inc = g_inc + jnp.where(row >= shift, pltpu.roll(g_inc, shift, axis=0), 0.0)
            shift *= 2
        e_inc = jnp.exp(g_inc)
        e_neg = jnp.exp(-g_inc)
        at = a * jnp.exp(g_inc - lw)
        rt = r * e_inc
        bt = b * e_neg
        kt = k * e_neg
        pre.append(dict(
            lr=jnp.concatenate([at, rt], axis=0), rl=jnp.concatenate([rt, at], axis=0),
            bk=jnp.concatenate([bt, kt], axis=0), kb=jnp.concatenate([kt, bt], axis=0),
            vst=jnp.concatenate([v * m1, v * m0], axis=0), v=v, state=state, gc=e_inc[c - 1:c, :]))
    sc0 = [_dot_nt(p['lr'] * m0, p['bk']) for p in pre]
    sc1 = [_dot_nt(p['rl'] * m1, p['kb']) for p in pre]
    xs = [_dot_nt(p['lr'], p['state']) for p in pre]
    ab = [sc0[i] * msk['ab0'] + sc1[i] * msk['ab1'] for i in nu]
    ak = [sc0[i] * msk['ak0'] + sc1[i] * msk['ak1'] for i in nu]
    rbk = [jnp.concatenate([sc0[i] * msk['rb0'] + sc1[i] * msk['rb1'],
                            sc0[i] * msk['rk0'] + sc1[i] * msk['rk1']], axis=1) for i in nu]
    akv = [_dot(ak[i], pre[i]['vst']) for i in nu]
    tinv = [msk['eye'] + ab[i] for i in nu]
    pw = ab
    for _ in range(n_rounds):
        pw = [_dot(pw[i], pw[i]) for i in nu]
        tinv = [tinv[i] + _dot(tinv[i], pw[i]) for i in nu]
    zst = [jnp.concatenate([xs[i][:c] * m0, xs[i][:c] * m1], axis=0) + akv[i] for i in nu]
    w = [_dot(tinv[i], zst[i]) for i in nu]
    ys = [_dot(rbk[i], jnp.concatenate([w[i], pre[i]['vst']], axis=0)) for i in nu]
    ds = [_dot_tn(jnp.concatenate([w[i][:c] + w[i][c:], pre[i]['v']], axis=0), pre[i]['bk']) for i in nu]
    outs = []
    for i in nu:
        y = xs[i][c:] + ys[i][:c] + ys[i][c:]
        outs.append((y, (pre[i]['state'] + ds[i] * bdmask) * pre[i]['gc']))
    return outs


def _wkv_kernel(c, n_chunks, bb, npair, r_ref, lw_ref, k_ref, v_ref, a_ref, b_ref, g_ref, rk_ref, lnw_ref,
                lnb_ref, seg_ref, o_ref, s_out, state, ybuf):
    tstep = pl.program_id(2)
    lane = lax.broadcasted_iota(jnp.int32, (1, LANES), 1)
    m0 = jnp.where(lane < HEAD_DIM, 1.0, 0.0).astype(F32)
    m1 = 1.0 - m0
    rowi = lax.broadcasted_iota(jnp.int32, (LANES, LANES), 0)
    coli = lax.broadcasted_iota(jnp.int32, (LANES, LANES), 1)
    bdmask = jnp.where((rowi >= HEAD_DIM) == (coli >= HEAD_DIM), 1.0, 0.0).astype(F32)
    half = LANES // 2

    @pl.when(tstep == 0)
    def _():
        state[...] = jnp.zeros_like(state)

    msk = _wkv_masks(c)
    n_rounds = max(c.bit_length() - 2, 0)

    def chunk_body(ci, carry):
        t0 = pl.multiple_of(ci * c, c)
        rows = pl.ds(t0, c)
        units = [(ib, ip) for ib in range(bb) for ip in range(npair)]
        ins = []
        for ib, ip in units:
            ls = slice(ip * LANES, (ip + 1) * LANES)
            ins.append((r_ref[ib, rows, ls], lw_ref[ib, rows, ls], k_ref[ib, rows, ls],
                        v_ref[ib, rows, ls], a_ref[ib, rows, ls], b_ref[ib, rows, ls],
                        state[ib * npair + ip]))
        outs = _wkv_chunk(c, n_rounds, ins, msk, m0, m1, bdmask)
        for (ib, ip), (y, ns) in zip(units, outs):
            state[ib * npair + ip] = ns
            ybuf[ib, rows, slice(ip * LANES, (ip + 1) * LANES)] = y
        return carry

    lax.fori_loop(0, n_chunks, chunk_body, 0)

    seg = seg_ref[...]
    for ib in range(bb):
        for ip in range(npair):
            ls = slice(ip * LANES, (ip + 1) * LANES)
            y = ybuf[ib, :, ls]
            mu = _split_dot(y, seg) * (1.0 / HEAD_DIM)
            yc = y - mu
            var = _split_dot(yc * yc, seg) * (1.0 / HEAD_DIM)
            yn = yc * lax.rsqrt(var + GN_EPS) * lnw_ref[:, ls] + lnb_ref[:, ls]
            rr = r_ref[ib, :, ls]
            bonus = _split_dot(rr * k_ref[ib, :, ls] * rk_ref[:, ls], seg) * v_ref[ib, :, ls]
            o_ref[ib, :, ls] = ((yn + bonus) * g_ref[ib, :, ls]).astype(o_ref.dtype)

    @pl.when(tstep == pl.num_programs(2) - 1)
    def _():
        for ib in range(bb):
            for ip in range(npair):
                st = state[ib * npair + ip]
                s_out[ib, ip] = st[:half] + st[half:]


def _wkv(r, lw, k, v, a, b, g, w, chunk, tblk, bb, npair):
    bsz, t, d = r.shape
    np_total = d // LANES
    assert bsz % bb == 0 and np_total % npair == 0 and t % tblk == 0 and tblk % chunk == 0
    blk = pl.BlockSpec((bb, tblk, npair * LANES), lambda ib, ip, it: (ib, it, ip))
    par = pl.BlockSpec((1, npair * LANES), lambda ib, ip, it: (0, ip))
    st_spec = pl.BlockSpec((bb, npair, HEAD_DIM, LANES), lambda ib, ip, it: (ib, ip, 0, 0))
    return pl.pallas_call(
        functools.partial(_wkv_kernel, chunk, tblk // chunk, bb, npair),
        out_shape=[jax.ShapeDtypeStruct((bsz, t, d), BF16),
                   jax.ShapeDtypeStruct((bsz, np_total, HEAD_DIM, LANES), F32)],
        grid=(bsz // bb, np_total // npair, t // tblk),
        in_specs=[blk] * 7 + [par] * 3 + [_const_spec(w['seg2'].shape)],
        out_specs=[blk, st_spec],
        scratch_shapes=[pltpu.VMEM((bb * npair, LANES, LANES), F32),
                        pltpu.VMEM((bb, tblk, npair * LANES), F32)],
        compiler_params=_cparams(3),
        name="wkv",
    )(r, lw, k, v, a, b, g, w['rk'], w['lnw'], w['lnb'], w['seg2'])


def _wkv_decode_kernel(n_tok, hb, r_ref, lw_ref, k_ref, v_ref, a_ref, b_ref, g_ref, rk_ref, lnw_ref, lnb_ref,
                       s_in, o_ref, s_out, w_s, y_s):
    bsz = r_ref.shape[2]
    w_s[...] = jnp.exp(lw_ref[...])
    for h in range(hb):
        rows = slice(h * HEAD_DIM, (h + 1) * HEAD_DIM)

        def body(i, carry, h=h, rows=rows):
            s = s_in[h, i]
            for t in range(n_tok):
                sa = jnp.sum(s * a_ref[t, rows, :], axis=0, keepdims=True)
                vi = v_ref[t, pl.ds(h * HEAD_DIM + i, 1), :]
                s = s * w_s[t, rows, :] + sa * b_ref[t, rows, :] + vi * k_ref[t, rows, :]
                y_s[t, pl.ds(h * HEAD_DIM + i, 1), :] = jnp.sum(s * r_ref[t, rows, :], axis=0, keepdims=True)
            s_out[h, i] = s
            return carry

        lax.fori_loop(0, HEAD_DIM, body, 0)

    for t in range(n_tok):
        outs = []
        for h in range(hb):
            rows = slice(h * HEAD_DIM, (h + 1) * HEAD_DIM)
            y = y_s[t, rows, :]
            mu = jnp.mean(y, axis=0, keepdims=True)
            yc = y - mu
            var = jnp.mean(yc * yc, axis=0, keepdims=True)
            yn = yc * lax.rsqrt(var + GN_EPS) * lnw_ref[rows, :] + lnb_ref[rows, :]
            bonus = jnp.sum(r_ref[t, rows, :] * k_ref[t, rows, :] * rk_ref[rows, :], axis=0,
                            keepdims=True) * v_ref[t, rows, :]
            outs.append((yn + bonus) * g_ref[t, rows, :])
        o_ref[t * bsz:(t + 1) * bsz, :] = jnp.concatenate(outs, axis=0).T.astype(o_ref.dtype)


def _wkv_decode(r, lw, k, v, a, b, g, rk, lnw, lnb, state, hb):
    n_tok, d, bsz = r.shape
    n_heads = d // HEAD_DIM
    assert n_heads % hb == 0
    blk = pl.BlockSpec((n_tok, hb * HEAD_DIM, bsz), lambda i: (0, i, 0))
    par = pl.BlockSpec((hb * HEAD_DIM, bsz), lambda i: (i, 0))
    st = pl.BlockSpec((hb, HEAD_DIM, HEAD_DIM, bsz), lambda i: (i, 0, 0, 0))
    return pl.pallas_call(
        functools.partial(_wkv_decode_kernel, n_tok, hb),
        out_shape=[jax.ShapeDtypeStruct((n_tok * bsz, d), BF16), jax.ShapeDtypeStruct(state.shape, F32)],
        grid=(n_heads // hb,),
        in_specs=[blk] * 7 + [par] * 3 + [st],
        out_specs=[pl.BlockSpec((n_tok * bsz, hb * HEAD_DIM), lambda i: (0, i)), st],
        scratch_shapes=[pltpu.VMEM((n_tok, hb * HEAD_DIM, bsz), F32)] * 2,
        compiler_params=_cparams(1),
        name="wkv_decode",
    )(r, lw, k, v, a, b, g, rk, lnw, lnb, state)


def _ffn_kernel(block_mode, tiles_per_seq, fc, final_norm, *refs):
    refs = list(refs)
    x_ref, o_ref, p_ref = refs[:3]
    pos = 3
    if block_mode:
        st_ref = refs[pos]
        pos += 1
    (wo_ref, gffn_ref, wup_ref, cw_ref, cb_ref, wd_ref, wg_ref, wp_ref, gout_ref) = refs[pos:pos + 9]
    pos += 9
    y_o, u_o = refs[pos:pos + 2]
    pos += 2
    x1_s, hn_s, act_s = refs[pos:pos + 3]
    pos += 3
    i = pl.program_id(0)
    dff = wd_ref.shape[0]
    n_f = dff // fc
    tm = x_ref.shape[0]
    if block_mode:
        c1_s, c2_s = refs[pos:pos + 2]

        @pl.when(i == 0)
        def _():
            c2_s[...] = st_ref[:, 0:2 * dff]
            c1_s[...] = st_ref[:, 2 * dff:4 * dff]
    else:
        c_s = refs[pos]

        @pl.when(i % tiles_per_seq == 0)
        def _():
            c_s[...] = jnp.zeros_like(c_s)

    x1 = x_ref[...] + jnp.dot(o_ref[...], wo_ref[...], preferred_element_type=F32)
    x1_s[...] = x1
    hn_s[...] = _rms(x1, gffn_ref[...]).astype(BF16)

    def up(f):
        hn = hn_s[...]
        return tuple(jnp.dot(hn, wup_ref[:, half * dff + f * fc:half * dff + (f + 1) * fc],
                             preferred_element_type=F32) for half in range(2))

    u_next = up(0)
    for f in range(n_f):
        u_cur = u_next
        if f + 1 < n_f:
            u_next = up(f + 1)
        conv = []
        for half in range(2):
            cols = slice(half * dff + f * fc, half * dff + (f + 1) * fc)
            u = u_cur[half]
            if block_mode:
                u1, u2 = c1_s[:, cols], c2_s[:, cols]
                c2_s[:, cols] = u1
                c1_s[:, cols] = u
                u_o[0, :, cols] = u
            else:
                carry = c_s[:, cols]
                u1, u2 = _prev_rows_carry(u, 1, carry), _prev_rows_carry(u, 2, carry)
                c_s[:, cols] = u[tm - SUBLANES:tm, :]
                u_o[0, :, cols] = u[tm - SUBLANES:tm, :]
            conv.append(cb_ref[:, cols] + cw_ref[0:1, cols] * u2 + cw_ref[1:2, cols] * u1 + cw_ref[2:3, cols] * u)
        gate, val = conv
        act_s[:, f * fc:(f + 1) * fc] = (gate * _sigmoid(gate) * val).astype(BF16)

    x2 = x1_s[...] + jnp.dot(act_s[...], wd_ref[...], preferred_element_type=F32)
    x3 = x2 + _sigmoid(_dot(x2, wg_ref[...])) * _dot(p_ref[...], wp_ref[...])
    if final_norm:
        x3 = _rms(x3, gout_ref[...])
    y_o[...] = x3


def _ffn(x, o, p, layer, seq_len, conv_state, w, final_norm, tm, fc):
    n, d = x.shape
    dff = w['wd'].shape[0]
    pdim = p.shape[2]
    block_mode = conv_state is not None
    assert dff % fc == 0
    if block_mode:
        tm = n // seq_len
        tiles_per_seq = 1
    else:
        tm = min(tm, seq_len)
        assert seq_len % tm == 0 and n % seq_len == 0
        tiles_per_seq = seq_len // tm
    rowd = pl.BlockSpec((tm, d), lambda i: (i, 0))
    rowp = pl.BlockSpec((None, tm, pdim), lambda i: (layer, i, 0))

    def resident(a):
        nd = a.ndim
        return pl.BlockSpec(a.shape, lambda i: (0,) * nd, pipeline_mode=pl.Buffered(1))

    in_specs = [rowd, rowd, rowp]
    args = [x, o, p]
    if block_mode:
        in_specs.append(resident(conv_state))
        args.append(conv_state)
    consts = [w['wo'], w['gffn'], w['wup'], w['cw'], w['cb'], w['wd'], w['wg'], w['wp'], w['gout']]
    in_specs += [resident(c) for c in consts]
    args += consts
    if block_mode:
        u_shape = jax.ShapeDtypeStruct((seq_len, tm, 2 * dff), F32)
        u_spec = pl.BlockSpec((1, tm, 2 * dff), lambda i: (i, 0, 0))
        scratch_c = [pltpu.VMEM((tm, 2 * dff), F32)] * 2
    else:
        u_shape = jax.ShapeDtypeStruct((n // tm, SUBLANES, 2 * dff), F32)
        u_spec = pl.BlockSpec((1, SUBLANES, 2 * dff), lambda i: (i, 0, 0))
        scratch_c = [pltpu.VMEM((SUBLANES, 2 * dff), F32)]
    return pl.pallas_call(
        functools.partial(_ffn_kernel, block_mode, tiles_per_seq, fc, final_norm),
        out_shape=[jax.ShapeDtypeStruct((n, d), F32), u_shape],
        grid=(n // tm,),
        in_specs=in_specs,
        out_specs=[rowd, u_spec],
        scratch_shapes=[pltpu.VMEM((tm, d), F32), pltpu.VMEM((tm, d), BF16), pltpu.VMEM((tm, dff), BF16)] + scratch_c,
        compiler_params=_cparams(1),
        name="ffn",
    )(*args)


def _qkv_kernel(prompt_mode, *refs):
    if prompt_mode:
        (x_ref, g_ref, w_ref, cos_ref, s1_ref, s2_ref, q_o, kt_o, vt_o, k16_o, vt16_o) = refs
    else:
        (x_ref, g_ref, w_ref, cos_ref, s1_ref, s2_ref, q_o, kt_o, vt_o, k_o, v_o) = refs
    hn = _rms(x_ref[...], g_ref[...]).astype(BF16)
    d = hn.shape[1]
    reps = d // LANES
    cos = jnp.concatenate([cos_ref[...]] * reps, axis=1)
    s1 = jnp.concatenate([s1_ref[...]] * reps, axis=1)
    s2 = jnp.concatenate([s2_ref[...]] * reps, axis=1)
    half = ROT_DIM // 2

    def rope(z):
        return z * cos + pltpu.roll(z, d - half, axis=1) * s1 + pltpu.roll(z, half, axis=1) * s2

    q = rope(jnp.dot(hn, w_ref[:, 0:d], preferred_element_type=F32)) * (HEAD_DIM ** -0.5)
    k = rope(jnp.dot(hn, w_ref[:, d:2 * d], preferred_element_type=F32))
    v = jnp.dot(hn, w_ref[:, 2 * d:3 * d], preferred_element_type=F32)
    vt = v.T
    kt_o[0] = k.T
    vt_o[0] = vt
    if prompt_mode:
        q_o[...] = (q * LOG2E).astype(BF16)
        k16_o[...] = k.astype(BF16)
        vt16_o[0] = vt.astype(BF16)
    else:
        q_o[...] = q
        k_o[...] = k
        v_o[...] = v


def _qkv(x, group, gmix, wqkv, cos, s1, s2, prompt_mode, tm):
    n, d = x.shape
    tm = min(tm, group)
    assert group % tm == 0 and n % group == 0
    tpg = group // tm
    row = pl.BlockSpec((tm, d), lambda i: (i, 0))
    rowt = pl.BlockSpec((tm, LANES), lambda i: (i, 0))
    tr = pl.BlockSpec((1, d, tm), lambda i: (i // tpg, 0, i % tpg))
    big = jax.ShapeDtypeStruct((n, d), F32)
    big16 = jax.ShapeDtypeStruct((n, d), BF16)
    bigt = jax.ShapeDtypeStruct((n // group, d, group), F32)
    if prompt_mode:
        out_shape = [big16, bigt, bigt, big16, jax.ShapeDtypeStruct((n // group, d, group), BF16)]
        out_specs = [row, tr, tr, row, tr]
    else:
        out_shape = [big, bigt, bigt, big, big]
        out_specs = [row, tr, tr, row, row]
    return pl.pallas_call(
        functools.partial(_qkv_kernel, prompt_mode),
        out_shape=out_shape,
        grid=(n // tm,),
        in_specs=[row, _const_spec(gmix.shape), _const_spec(wqkv.shape), rowt, rowt, rowt],
        out_specs=out_specs,
        compiler_params=_cparams(1),
        name="qkv",
    )(x, gmix, wqkv, cos, s1, s2)


def _moba_kernel(n_blocks, grp, tiles_per_step, q_ref, k_ref, vt_ref, o_ref, *scratch):
    def tile(t, carry):
        r0 = pl.multiple_of(t * MOBA_BLOCK, MOBA_BLOCK)
        qi = pl.program_id(2) * tiles_per_step + t
        o_ref[0, pl.ds(r0, MOBA_BLOCK), :] = _moba_tile(
            n_blocks, grp, qi, q_ref[0, pl.ds(r0, MOBA_BLOCK), :], k_ref, vt_ref, *scratch).astype(o_ref.dtype)
        return carry

    lax.fori_loop(0, tiles_per_step, tile, 0)


def _moba_tile(n_blocks, grp, qi, q, k_ref, vt_ref, o_s, m_s, l_s, g_s, w_s):
    blk = MOBA_BLOCK
    sub = min(MOBA_SUBGROUP, grp)
    lane = lax.broadcasted_iota(jnp.int32, (1, LANES), 1)
    zero16 = jnp.zeros_like(q)
    qh = [jnp.where(lane < HEAD_DIM, q, zero16), jnp.where(lane >= HEAD_DIM, q, zero16)]
    ones_rows = jnp.where(lax.broadcasted_iota(jnp.int32, (2 * SUBLANES, blk), 0) == 0, 1.0, 0.0).astype(BF16)
    neg_inf = jnp.float32(-jnp.inf)
    for h in range(2):
        m_s[h] = jnp.full((n_blocks, blk), NEG_BIG, F32)
        l_s[h] = jnp.zeros((n_blocks, blk), F32)
        g_s[h] = jnp.full((n_blocks, blk), neg_inf, F32)

    def scores(j0, nb):
        k0 = pl.multiple_of(j0 * blk, blk)
        kg = k_ref[0, pl.ds(k0, nb * blk), :]
        return [lax.dot_general(kg, qh[h], (((1,), (1,)), ((), ())), preferred_element_type=F32)
                for h in range(2)]

    def partials(j0, nb, st, diag):
        parts = []
        for h in range(2):
            for jb in range(nb):
                s = st[h][jb * blk:(jb + 1) * blk]
                gate = jnp.sum(s, axis=0, keepdims=True) * (1.0 / blk)
                s = s.astype(BF16)
                if diag:
                    kr = lax.broadcasted_iota(jnp.int32, (blk, blk), 0)
                    qc = lax.broadcasted_iota(jnp.int32, (blk, blk), 1)
                    s = jnp.where(kr <= qc, s, jnp.full_like(s, NEG_BIG))
                mj = jnp.max(s, axis=0, keepdims=True)
                p = jnp.exp2(s - mj)
                parts.append((h, jb, gate, mj.astype(F32), p))
        pvs = []
        for h, jb, gate, mj, p in parts:
            kb0 = pl.multiple_of((j0 + jb) * blk, blk)
            vt_ext = jnp.concatenate([vt_ref[0, h * HEAD_DIM:(h + 1) * HEAD_DIM, pl.ds(kb0, blk)], ones_rows],
                                     axis=0)
            pvs.append(jnp.dot(vt_ext, p, preferred_element_type=F32))
        for (h, jb, gate, mj, p), pv in zip(parts, pvs):
            j = j0 + jb
            o_s[h * n_blocks + j] = pv[:HEAD_DIM]
            l_s[h, pl.ds(j, 1), :] = pv[HEAD_DIM:HEAD_DIM + 1]
            m_s[h, pl.ds(j, 1), :] = mj
            g_s[h, pl.ds(j, 1), :] = gate

    def run_blocks(j0, nb):
        st = scores(j0, sub)
        for s in range(1, nb // sub):
            st_next = scores(j0 + s * sub, sub)
            partials(j0 + (s - 1) * sub, sub, st, False)
            st = st_next
        partials(j0 + nb - sub, sub, st, False)

    def group_body(gi, carry):
        run_blocks(gi * grp, grp)
        return carry

    n_full = qi // grp
    rem = qi - n_full * grp
    lax.fori_loop(0, n_full, group_body, 0)
    half = grp // 2
    if half >= sub:
        @pl.when(jnp.logical_and(rem > 0, rem <= half))
        def _():
            run_blocks(n_full * grp, half)

    @pl.when(rem > (half if half >= sub else 0))
    def _():
        run_blocks(n_full * grp, grp)

    partials(qi, 1, scores(qi, 1), True)

    jrow = lax.broadcasted_iota(jnp.int32, (n_blocks, 1), 0)
    jrow_f = jrow.astype(F32)
    lsum = []
    for h in range(2):
        cur = jnp.where(jrow < qi, g_s[h], neg_inf)
        sel = jrow == qi
        for _ in range(MOBA_TOPK):
            mx = jnp.max(cur, axis=0, keepdims=True)
            first = jnp.min(jnp.where(cur == mx, jrow_f, float(n_blocks)), axis=0, keepdims=True)
            onehot = jrow_f == first
            sel = jnp.logical_or(sel, jnp.logical_and(onehot, mx > neg_inf))
            cur = jnp.where(onehot, neg_inf, cur)
        m = m_s[h]
        mx = jnp.max(jnp.where(sel, m, NEG_BIG), axis=0, keepdims=True)
        w = jnp.where(sel, jnp.exp2(m - mx), 0.0)
        w_s[h] = w
        lsum.append(jnp.sum(w * l_s[h], axis=0, keepdims=True))

    def merge(j, accs):
        return tuple(accs[h] + w_s[h, pl.ds(j, 1), :] * o_s[h * n_blocks + j] for h in range(2))

    accs = lax.fori_loop(0, qi + 1, merge, tuple(jnp.zeros((HEAD_DIM, blk), F32) for _ in range(2)))
    return jnp.concatenate([accs[h] / lsum[h] for h in range(2)], axis=0).T


def _moba_prompt(q16, k16, vt16, grp, tiles_per_step):
    bsz, t, d = q16.shape
    n_blocks = t // MOBA_BLOCK
    assert n_blocks % grp == 0 and n_blocks % tiles_per_step == 0
    stat = pltpu.VMEM((2, n_blocks, MOBA_BLOCK), F32)
    rows = tiles_per_step * MOBA_BLOCK
    return pl.pallas_call(
        functools.partial(_moba_kernel, n_blocks, grp, tiles_per_step),
        out_shape=jax.ShapeDtypeStruct((bsz, t, d), BF16),
        grid=(bsz, d // LANES, n_blocks // tiles_per_step),
        in_specs=[pl.BlockSpec((1, rows, LANES), lambda b, p, i: (b, i, p)),
                  pl.BlockSpec((1, t, LANES), lambda b, p, i: (b, 0, p)),
                  pl.BlockSpec((1, LANES, t), lambda b, p, i: (b, p, 0))],
        out_specs=pl.BlockSpec((1, rows, LANES), lambda b, p, i: (b, i, p)),
        scratch_shapes=[pltpu.VMEM((2 * n_blocks, HEAD_DIM, MOBA_BLOCK), F32), stat, stat, stat, stat],
        compiler_params=_cparams(3),
        name="moba_prompt",
    )(q16, k16, vt16)


def _moba_decode_kernel(n_tok, n_heads, pages_per_step, n_steps, pt_ref, q_ref, kn_ref, vn_ref, *refs):
    kp = refs[:pages_per_step]
    vp = refs[pages_per_step:2 * pages_per_step]
    o_ref = refs[2 * pages_per_step]
    m_s, l_s, g_s, acc_s = refs[2 * pages_per_step + 1:]
    step = pl.program_id(1)
    d = q_ref.shape[2]
    page = kp[0].shape[2]
    pages_per_blk = MOBA_BLOCK // page
    blk_per_step = pages_per_step // pages_per_blk
    n_past = n_steps * blk_per_step
    nrow = n_tok * n_heads
    q = q_ref[0]
    lane_head = lax.broadcasted_iota(jnp.int32, (n_heads, d), 1) // HEAD_DIM
    row_head = lax.broadcasted_iota(jnp.int32, (n_heads, d), 0)
    hmask = lane_head == row_head
    qbd = jnp.concatenate(
        [jnp.where(hmask, jnp.broadcast_to(q[t:t + 1, :], (n_heads, d)), 0.0) for t in range(n_tok)], axis=0)
    qbd16 = qbd.astype(BF16)

    scores = [jnp.concatenate([_dot(qbd16, kp[jb * pages_per_blk + i][0]) for i in range(pages_per_blk)], axis=1)
              for jb in range(blk_per_step)]
    stats = []
    for s in scores:
        gate = jnp.sum(s, axis=1, keepdims=True) * (1.0 / MOBA_BLOCK)
        mj = jnp.max(s, axis=1, keepdims=True)
        p = jnp.exp(s - mj)
        stats.append((gate, mj, jnp.sum(p, axis=1, keepdims=True), p.astype(BF16)))
    outs = [sum(_dot_nt(p[:, i * page:(i + 1) * page], vp[jb * pages_per_blk + i][0]) for i in range(pages_per_blk))
            for jb, (_, _, _, p) in enumerate(stats)]
    for jb, ((gate, mj, lj, _), oj) in enumerate(zip(stats, outs)):
        slot = step * blk_per_step + jb
        m_s[slot] = jnp.broadcast_to(mj, (nrow, LANES))
        l_s[slot] = jnp.broadcast_to(lj, (nrow, LANES))
        g_s[slot] = jnp.broadcast_to(gate, (nrow, LANES))
        acc_s[slot] = oj

    @pl.when(step == n_steps - 1)
    def _():
        gates = [g_s[j][:, 0:1] for j in range(n_past)]
        kn = kn_ref[0]
        vn = vn_ref[0]
        trow = lax.broadcasted_iota(jnp.int32, (nrow, 1), 0) // n_heads
        s_own = [jnp.where(trow >= t, jnp.sum(qbd * kn[t:t + 1, :], axis=1, keepdims=True), NEG_BIG)
                 for t in range(n_tok)]
        m = functools.reduce(jnp.maximum, s_own)
        p_own = [jnp.exp(s - m) for s in s_own]
        l = sum(p_own)
        acc = sum(p_own[t] * vn[t:t + 1, :] for t in range(n_tok))
        for j in range(n_past):
            rank = sum(jnp.where(jnp.logical_or(gates[i] > gates[j],
                                                jnp.logical_and(gates[i] == gates[j], i < j)), 1.0, 0.0)
                       for i in range(n_past) if i != j)
            selj = rank < float(MOBA_TOPK)
            mj = m_s[j][:, 0:1]
            m_new = jnp.where(selj, jnp.maximum(m, mj), m)
            alpha = jnp.exp(m - m_new)
            beta = jnp.where(selj, jnp.exp(mj - m_new), 0.0)
            l = l * alpha + l_s[j][:, 0:1] * beta
            acc = acc * alpha + acc_s[j] * beta
            m = m_new
        lane_head_r = lax.broadcasted_iota(jnp.int32, (nrow, d), 1) // HEAD_DIM
        row_head_r = lax.broadcasted_iota(jnp.int32, (nrow, d), 0) % n_heads
        out = jnp.where(lane_head_r == row_head_r, acc / l, 0.0)
        o_ref[0] = jnp.concatenate(
            [jnp.sum(out[t * n_heads:(t + 1) * n_heads], axis=0, keepdims=True) for t in range(n_tok)],
            axis=0).astype(o_ref.dtype)


def _moba_decode(q, kn, vn, cache_kt, cache_vt, page_table, pages_per_step):
    bsz, n_tok, d = q.shape
    n_heads = d // HEAD_DIM
    pool, _, page = cache_kt.shape
    n_pages = page_table.shape[1]
    assert n_pages % pages_per_step == 0 and MOBA_BLOCK % page == 0
    assert (n_pages * page) % MOBA_BLOCK == 0 and n_tok <= MOBA_BLOCK
    assert pages_per_step % (MOBA_BLOCK // page) == 0
    n_steps = n_pages // pages_per_step
    n_past = n_pages * page // MOBA_BLOCK
    nrow = n_tok * n_heads
    tok = pl.BlockSpec((1, n_tok, d), lambda b, s, pt: (b, 0, 0))

    def page_spec(i):
        return pl.BlockSpec((1, d, page), lambda b, s, pt: (pt[b, s * pages_per_step + i], 0, 0))

    grid_spec = pltpu.PrefetchScalarGridSpec(
        num_scalar_prefetch=1,
        grid=(bsz, n_steps),
        in_specs=[tok, tok, tok] + [page_spec(i) for i in range(pages_per_step)] * 2,
        out_specs=tok,
        scratch_shapes=[pltpu.VMEM((n_past, nrow, LANES), F32)] * 3 + [pltpu.VMEM((n_past, nrow, d), F32)],
    )
    return pl.pallas_call(
        functools.partial(_moba_decode_kernel, n_tok, n_heads, pages_per_step, n_steps),
        out_shape=jax.ShapeDtypeStruct((bsz, n_tok, d), BF16),
        grid_spec=grid_spec,
        compiler_params=_cparams(2),
        name="moba_decode",
    )(page_table, q, kn, vn, *([cache_kt] * pages_per_step), *([cache_vt] * pages_per_step))


def _pad_cols(w, n):
    return jnp.pad(w, ((0, 0), (0, n - w.shape[1])))


def _pad_rows(w, n):
    return jnp.pad(w, ((0, n - w.shape[0]), (0, 0)))


def _rope_tables(pos):
    half = ROT_DIM // 2
    inv = ROPE_THETA ** (-2.0 * jnp.arange(half, dtype=F32) / ROT_DIM)
    ang = pos.astype(F32)[:, None] * inv[None, :]
    cos = jnp.cos(ang)
    sin = jnp.sin(ang)
    t = pos.shape[0]
    ones = jnp.ones((t, HEAD_DIM - ROT_DIM), F32)
    zeros_r = jnp.zeros((t, HEAD_DIM - ROT_DIM), F32)
    zeros_h = jnp.zeros((t, half), F32)
    c = jnp.concatenate([cos, cos, ones], axis=1)
    s1 = jnp.concatenate([-sin, zeros_h, zeros_r], axis=1)
    s2 = jnp.concatenate([zeros_h, sin, zeros_r], axis=1)
    rep = LANES // HEAD_DIM
    return jnp.tile(c, (1, rep)), jnp.tile(s1, (1, rep)), jnp.tile(s2, (1, rep))


def _unpack_state(s):
    b, hp, dv, dk2 = s.shape
    return s.reshape(b, hp, dv, 2, dk2 // 2).transpose(0, 1, 3, 2, 4).reshape(b, hp * 2, dv, dk2 // 2)


def kernel(x_prompt, x_sample, state_wkv, state_shift, cache_k, cache_v, state_conv, page_table, p_prompt, p_sample, rw_mix, rw_rkv, rw_w0, rw_w1, rw_w2, rw_a0, rw_a1, rw_a2, rw_g1, rw_g2, rw_kk, rw_ka, rw_rk, rw_lnw, rw_lnb, rw_wo, mb_wqkv, mb_wo, norm_mix, norm_ffn, ff_wup, ff_conv_w, ff_conv_b, ff_wdown, ple_wp, ple_wg, norm_out):
    bp, tp, d = x_prompt.shape
    bs, ts, _ = x_sample.shape
    depth = norm_mix.shape[0]
    n_heads = d // HEAD_DIM
    dff = ff_wdown.shape[1]
    pool, page = cache_k.shape[1], cache_k.shape[2]
    past_len = page_table.shape[1] * page
    assert d % LANES == 0 and depth == 2 and ts >= 2

    lane_head = jnp.arange(d) // HEAD_DIM
    segr = (lane_head[:, None] == jnp.arange(LANES)[None, :]).astype(BF16)
    segb = segr.T
    seg2 = ((jnp.arange(LANES) // HEAD_DIM)[:, None] == (jnp.arange(LANES) // HEAD_DIM)[None, :]).astype(BF16)

    def row(v):
        return v.reshape(1, -1).astype(F32)

    def col(v):
        return jnp.broadcast_to(v.reshape(-1, 1).astype(F32), (v.size, bs))

    rw = dict(
        gmix=row(norm_mix[0]), mix=rw_mix[0], wrkv=rw_rkv[0].astype(BF16), w0=row(rw_w0[0]),
        w1=_pad_cols(rw_w1[0], LANES).astype(BF16), w2=_pad_rows(rw_w2[0], LANES).astype(BF16),
        a0=row(rw_a0[0]), a1=_pad_cols(rw_a1[0], LANES).astype(BF16), a2=_pad_rows(rw_a2[0], LANES).astype(BF16),
        g1=_pad_cols(rw_g1[0], 2 * LANES).astype(BF16), g2=_pad_rows(rw_g2[0], 2 * LANES).astype(BF16),
        kk=row(rw_kk[0]), ka=row(rw_ka[0]), segr=segr, segb=segb,
        rk=row(rw_rk[0]), lnw=row(rw_lnw[0]), lnb=row(rw_lnb[0]), seg2=seg2)

    def ffw(i, wo):
        return dict(wo=wo.astype(BF16), gffn=row(norm_ffn[i]), wup=ff_wup[i].astype(BF16), cw=ff_conv_w[i],
                    cb=row(ff_conv_b[i]), wd=ff_wdown[i].astype(BF16), wg=ple_wg[i].astype(BF16),
                    wp=ple_wp[i].astype(BF16), gout=row(norm_out))

    fw0 = ffw(0, rw_wo[0])
    fw1 = ffw(1, mb_wo[0])
    wqkv = mb_wqkv[0].astype(BF16)
    gmix1 = row(norm_mix[1])

    xp = x_prompt.reshape(bp * tp, d)
    pp = p_prompt.reshape(depth, bp * tp, -1)
    xs = x_sample.transpose(1, 0, 2).reshape(ts * bs, d)
    ps = p_sample.transpose(0, 2, 1, 3).reshape(depth, ts * bs, -1)

    def conv_state_p(u):
        return u.reshape(bp, -1, SUBLANES, 2 * dff)[:, -1, SUBLANES - 2:]

    def conv_state_s(u):
        return u[ts - 2:].transpose(1, 0, 2)

    r, lw, k, v, a, b, g, shift_p = _rwkv_proj(xp, tp, None, rw)
    sh = lambda z: z.reshape(bp, tp, d)
    o_p, wkv_p = _wkv(sh(r), sh(lw), sh(k), sh(v), sh(a), sh(b), sh(g), rw,
                      chunk=WKV_CHUNK, tblk=min(WKV_TIME_BLOCK, tp), bb=bp, npair=WKV_PAIRS_PER_STEP)
    xp, up_ = _ffn(xp, o_p.reshape(bp * tp, d), pp, 0, tp, None, fw0, False, FFN_ROW_TILE, FFN_CHUNK)
    conv_p0 = conv_state_p(up_)

    r, lw, k, v, a, b, g, shift_s = _rwkv_proj(xs, ts, state_shift[0], rw)
    st_in = jnp.transpose(state_wkv[0], (1, 2, 3, 0))
    o_s, st_out = _wkv_decode(r, lw, k, v, a, b, g, col(rw_rk[0]), col(rw_lnw[0]), col(rw_lnb[0]), st_in,
                              WKV_DECODE_HEADS)
    wkv_s = jnp.transpose(st_out, (3, 0, 1, 2))
    xs, us_ = _ffn(xs, o_s, ps, 0, ts, state_conv[0].reshape(bs, 4 * dff), fw0, False, FFN_ROW_TILE, FFN_CHUNK)
    conv_s0 = conv_state_s(us_)

    cos, s1, s2 = _rope_tables(jnp.arange(tp, dtype=jnp.int32))
    cos_p, s1_p, s2_p = (jnp.tile(z, (bp, 1)) for z in (cos, s1, s2))
    q16, kt, vt, k16, vt16 = _qkv(xp, tp, gmix1, wqkv, cos_p, s1_p, s2_p, True, PROJ_ROW_TILE)
    k_p = kt.reshape(bp, n_heads, HEAD_DIM, tp).transpose(0, 3, 1, 2)
    v_p = vt.reshape(bp, n_heads, HEAD_DIM, tp).transpose(0, 3, 1, 2)
    o_p = _moba_prompt(q16.reshape(bp, tp, d), k16.reshape(bp, tp, d), vt16,
                       min(MOBA_GROUP, tp // MOBA_BLOCK), min(MOBA_TILES_PER_STEP, tp // MOBA_BLOCK))
    xp, up_ = _ffn(xp, o_p.reshape(bp * tp, d), pp, 1, tp, None, fw1, True, FFN_ROW_TILE, FFN_CHUNK)
    conv_p1 = conv_state_p(up_)

    cos, s1, s2 = _rope_tables(past_len + jnp.arange(ts, dtype=jnp.int32))
    cos_s, s1_s, s2_s = (jnp.repeat(z, bs, axis=0) for z in (cos, s1, s2))
    q, kt, vt, k, v = _qkv(xs, bs, gmix1, wqkv, cos_s, s1_s, s2_s, False, PROJ_ROW_TILE)
    k_s = kt.reshape(ts, n_heads, HEAD_DIM, bs).transpose(3, 0, 1, 2)
    v_s = vt.reshape(ts, n_heads, HEAD_DIM, bs).transpose(3, 0, 1, 2)
    by_seq = lambda z: z.reshape(ts, bs, d).transpose(1, 0, 2)
    cache_kt = jnp.transpose(cache_k[0], (0, 2, 3, 1)).reshape(pool, d, page)
    cache_vt = jnp.transpose(cache_v[0], (0, 2, 3, 1)).reshape(pool, d, page)
    o_s = _moba_decode(by_seq(q), by_seq(k), by_seq(v), cache_kt, cache_vt, page_table,
                       min(DECODE_PAGES_PER_STEP, page_table.shape[1]))
    o_s = o_s.transpose(1, 0, 2).reshape(ts * bs, d)
    xs, us_ = _ffn(xs, o_s, ps, 1, ts, state_conv[1].reshape(bs, 4 * dff), fw1, True, FFN_ROW_TILE, FFN_CHUNK)
    conv_s1 = conv_state_s(us_)

    return (xp.reshape(bp, tp, d), by_seq(xs),
            _unpack_state(wkv_p)[None], shift_p.reshape(1, bp, d), k_p[None], v_p[None],
            jnp.stack([conv_p0, conv_p1]),
            wkv_s[None], shift_s[None], k_s[None], v_s[None],
            jnp.stack([conv_s0, conv_s1]))
```
